```python
import math
import jax, jax.numpy as jnp
from jax import lax
import numpy as np

D_MODEL = 1024
BATCH = 8
SEQ = 2048
DEPTH = 1
DEC_BATCH = 2
DEC_SEQ = 16384
PAST_LEN = 128

GRID_W = 64
N_META = 16
WIN_R = 8
WIN_C = 16
NA_QB = 16
NA_KB = 2 * WIN_C
NA_HEADS = 8
NA_HEAD_DIM = 64
MLA_HEADS = 8
QK_NOPE = 64
QK_ROPE = 32
V_HEAD = 64
Q_LORA = 768
KV_LORA = 256
ROPE_THETA = 10000.0
N_EXPERTS = 16
EC_CAPACITY = 2
D_EXPERT = 2048
QBLK = 128
NORM_EPS = 1e-6
NA_WIDTH = NA_HEADS * NA_HEAD_DIM
MLA_WIDTH = MLA_HEADS * V_HEAD
MIX_WIDTH = NA_WIDTH + MLA_WIDTH
IN_COLS = 3 * NA_WIDTH + Q_LORA + KV_LORA + QK_ROPE

kernel_name = "hymba_natten_mla_ec_encoder"


def rmsnorm(x, g):
    xf = x.astype(jnp.float32)
    y = xf * lax.rsqrt(jnp.mean(xf * xf, axis=-1, keepdims=True) + NORM_EPS)
    return (y * g.astype(jnp.float32)).astype(x.dtype)


def apply_rope(x, n_pos):
    half = QK_ROPE // 2
    inv_freq = ROPE_THETA ** (-jnp.arange(half, dtype=jnp.float32) * 2.0 / QK_ROPE)
    ang = jnp.arange(n_pos, dtype=jnp.float32)[:, None] * inv_freq[None, :]
    cos = jnp.cos(ang)[:, None, :].astype(x.dtype)
    sin = jnp.sin(ang)[:, None, :].astype(x.dtype)
    x1, x2 = x[..., :half], x[..., half:]
    return jnp.concatenate([x1 * cos - x2 * sin, x2 * cos + x1 * sin], axis=-1)


def neighbourhood_attention(q, k, v, rpb):
    B, T, H, d = q.shape
    S = T - N_META
    rows = S // GRID_W
    wr = min(WIN_R, rows)
    nqb = GRID_W // NA_QB
    scale = 1.0 / math.sqrt(d)
    q_m, k_m, v_m = q[:, :N_META], k[:, :N_META], v[:, :N_META]
    qg = q[:, N_META:].reshape(B, rows, nqb, NA_QB, H, d)
    kg = k[:, N_META:].reshape(B, rows, GRID_W, H, d)
    vg = v[:, N_META:].reshape(B, rows, GRID_W, H, d)
    r_idx = jnp.arange(rows)
    row_start = jnp.clip(r_idx - wr // 2, 0, rows - wr)
    key_rows = row_start[:, None] + jnp.arange(wr)[None, :]
    blk = jnp.arange(nqb)
    col_start = jnp.clip(blk * NA_QB - WIN_C // 2, 0, GRID_W - NA_KB)
    key_cols = col_start[:, None] + jnp.arange(NA_KB)[None, :]
    ri = key_rows[:, None, :, None]
    ci = key_cols[None, :, None, :]
    nk = wr * NA_KB
    k_blk = kg[:, ri, ci].reshape(B, rows, nqb, nk, H, d)
    v_blk = vg[:, ri, ci].reshape(B, rows, nqb, nk, H, d)
    q_cols = blk[:, None] * NA_QB + jnp.arange(NA_QB)[None, :]
    q_cstart = jnp.clip(q_cols - WIN_C // 2, 0, GRID_W - WIN_C)
    kc = key_cols[:, None, :]
    valid = (kc >= q_cstart[..., None]) & (kc < q_cstart[..., None] + WIN_C)
    valid = jnp.broadcast_to(valid[:, :, None, :], (nqb, NA_QB, wr, NA_KB)).reshape(nqb, NA_QB, nk)
    dr = key_rows - r_idx[:, None] + (WIN_R - 1)
    dc = jnp.clip(kc - q_cols[..., None] + (WIN_C - 1), 0, 2 * WIN_C - 2)
    bias = rpb[:, dr[:, None, None, :, None], dc[None, :, :, None, :]]
    bias = bias.reshape(H, rows, nqb, NA_QB, nk).transpose(1, 2, 0, 3, 4).astype(jnp.float32)
    s_win = jnp.einsum('brnqhd,brnkhd->brnhqk', qg, k_blk).astype(jnp.float32) * scale + bias[None]
    s_win = jnp.where(valid[None, None, :, None], s_win, -jnp.inf)
    s_met = jnp.einsum('brnqhd,bmhd->brnhqm', qg, k_m).astype(jnp.float32) * scale
    p = jax.nn.softmax(jnp.concatenate([s_win, s_met], axis=-1), axis=-1).astype(v.dtype)
    o_grid = (jnp.einsum('brnhqk,brnkhd->brnqhd', p[..., :nk], v_blk)
              + jnp.einsum('brnhqm,bmhd->brnqhd', p[..., nk:], v_m))
    o_grid = o_grid.reshape(B, S, H, d)
    s_mm = jnp.einsum('bqhd,bkhd->bhqk', q_m, k_m).astype(jnp.float32) * scale
    p_mm = jax.nn.softmax(s_mm, axis=-1).astype(v.dtype)
    o_meta = jnp.einsum('bhqk,bkhd->bqhd', p_mm, v_m)
    return jnp.concatenate([o_meta, o_grid], axis=1)


def dense_block_attention(q, k, v):
    B, T, H, dq = q.shape
    dv = v.shape[-1]
    nblk = -(-T // QBLK)
    scale = 1.0 / math.sqrt(dq)
    qp = jnp.pad(q, ((0, 0), (0, nblk * QBLK - T), (0, 0), (0, 0)))
    qb = qp.reshape(B, nblk, QBLK, H, dq).transpose(1, 0, 2, 3, 4)

    def attend(qi):
        s = jnp.einsum('bqhd,bkhd->bhqk', qi, k).astype(jnp.float32) * scale
        p = jax.nn.softmax(s, axis=-1).astype(v.dtype)
        return jnp.einsum('bhqk,bkhd->bqhd', p, v)

    o = lax.map(attend, qb)
    return o.transpose(1, 0, 2, 3, 4).reshape(B, nblk * QBLK, H, dv)[:, :T]


def expert_choice_ffn(h, w_router, w_gate, w_up, w_down):
    N = h.shape[0]
    cap = EC_CAPACITY * N // N_EXPERTS
    aff = jax.nn.softmax(jnp.einsum('nd,de->ne', h, w_router).astype(jnp.float32), axis=-1)
    gates, idx = lax.top_k(aff.T, cap)
    xe = h[idx]
    hid = jax.nn.silu(jnp.einsum('ecd,edf->ecf', xe, w_gate)) * jnp.einsum('ecd,edf->ecf', xe, w_up)
    ye = jnp.einsum('ecf,efd->ecd', hid, w_down) * gates[..., None].astype(h.dtype)
    return jnp.zeros_like(h).at[idx.reshape(-1)].add(ye.reshape(-1, h.shape[-1]))


def hybrid_layer(h, g_attn, w_in, na_rpb, g_q, w_uq, g_kv, w_ukv, w_o, g_ffn, w_router, w_gate, w_up, w_down):
    B, T, D = h.shape
    a = rmsnorm(h, g_attn)
    proj = jnp.einsum('btd,dc->btc', a, w_in)
    cuts = [NA_WIDTH, 2 * NA_WIDTH, 3 * NA_WIDTH, 3 * NA_WIDTH + Q_LORA, 3 * NA_WIDTH + Q_LORA + KV_LORA]
    q_na, k_na, v_na, c_q, c_kv, k_rope = jnp.split(proj, cuts, axis=-1)
    o_na = neighbourhood_attention(q_na.reshape(B, T, NA_HEADS, NA_HEAD_DIM),
                                   k_na.reshape(B, T, NA_HEADS, NA_HEAD_DIM),
                                   v_na.reshape(B, T, NA_HEADS, NA_HEAD_DIM), na_rpb)
    q = jnp.einsum('btc,ch->bth', rmsnorm(c_q, g_q), w_uq).reshape(B, T, MLA_HEADS, QK_NOPE + QK_ROPE)
    kv = jnp.einsum('btc,ch->bth', rmsnorm(c_kv, g_kv), w_ukv).reshape(B, T, MLA_HEADS, QK_NOPE + V_HEAD)
    q_full = jnp.concatenate([q[..., :QK_NOPE], apply_rope(q[..., QK_NOPE:], T)], axis=-1)
    k_r = apply_rope(k_rope[:, :, None, :], T)
    k_full = jnp.concatenate([kv[..., :QK_NOPE], jnp.broadcast_to(k_r, (B, T, MLA_HEADS, QK_ROPE))], axis=-1)
    o_mla = dense_block_attention(q_full, k_full, kv[..., QK_NOPE:])
    mix = jnp.concatenate([o_na.reshape(B, T, NA_WIDTH), o_mla.reshape(B, T, MLA_WIDTH)], axis=-1)
    h = h + jnp.einsum('btc,cd->btd', mix, w_o)
    f = expert_choice_ffn(rmsnorm(h, g_ffn).reshape(B * T, D), w_router, w_gate, w_up, w_down)
    return h + f.reshape(B, T, D)


def run_group(x, meta_tokens, g_attn, w_in, na_rpb, g_q, w_uq, g_kv, w_ukv, w_o, g_ffn,
              w_router, w_gate, w_up, w_down, g_final):
    B = x.shape[0]
    meta = jnp.broadcast_to(meta_tokens[None].astype(x.dtype), (B, N_META, x.shape[-1]))
    h = jnp.concatenate([meta, x], axis=1)
    for l in range(DEPTH):
        h = hybrid_layer(h, g_attn[l], w_in[l], na_rpb[l], g_q[l], w_uq[l], g_kv[l], w_ukv[l], w_o[l],
                         g_ffn[l], w_router[l], w_gate[l], w_up[l], w_down[l])
    return rmsnorm(h, g_final)[:, N_META:]


def setup_inputs(seed: int = 0) -> dict:
    key = jax.random.key(seed)
    ks = jax.random.split(key, 18)
    f32 = jnp.float32

    def nrm(k, shape, scale):
        return jax.random.normal(k, shape, f32) * scale

    def gain(k, shape):
        return 1.0 + 0.02 * jax.random.normal(k, shape, f32)

    return {
        "x_prompt": nrm(ks[0], (BATCH, SEQ, D_MODEL), 1.0),
        "x_sample": nrm(ks[1], (DEC_BATCH, DEC_SEQ, D_MODEL), 1.0),
        "meta_tokens": nrm(ks[2], (N_META, D_MODEL), 1.0),
        "g_attn": gain(ks[3], (DEPTH, D_MODEL)),
        "w_in": nrm(ks[4], (DEPTH, D_MODEL, IN_COLS), D_MODEL ** -0.5),
        "na_rpb": nrm(ks[5], (DEPTH, NA_HEADS, 2 * WIN_R - 1, 2 * WIN_C - 1), 0.1),
        "g_q": gain(ks[6], (DEPTH, Q_LORA)),
        "w_uq": nrm(ks[7], (DEPTH, Q_LORA, MLA_HEADS * (QK_NOPE + QK_ROPE)), Q_LORA ** -0.5),
        "g_kv": gain(ks[8], (DEPTH, KV_LORA)),
        "w_ukv": nrm(ks[9], (DEPTH, KV_LORA, MLA_HEADS * (QK_NOPE + V_HEAD)), KV_LORA ** -0.5),
        "w_o": nrm(ks[10], (DEPTH, MIX_WIDTH, D_MODEL), MIX_WIDTH ** -0.5),
        "g_ffn": gain(ks[11], (DEPTH, D_MODEL)),
        "w_router": nrm(ks[12], (DEPTH, D_MODEL, N_EXPERTS), D_MODEL ** -0.5),
        "w_gate": nrm(ks[13], (DEPTH, N_EXPERTS, D_MODEL, D_EXPERT), D_MODEL ** -0.5),
        "w_up": nrm(ks[14], (DEPTH, N_EXPERTS, D_MODEL, D_EXPERT), D_MODEL ** -0.5),
        "w_down": nrm(ks[15], (DEPTH, N_EXPERTS, D_EXPERT, D_MODEL), D_EXPERT ** -0.5),
        "g_final": gain(ks[16], (D_MODEL,)),
    }


def reference(x_prompt, x_sample, meta_tokens, g_attn, w_in, na_rpb, g_q, w_uq, g_kv, w_ukv, w_o,
              g_ffn, w_router, w_gate, w_up, w_down, g_final):
    y_prompt = run_group(x_prompt, meta_tokens, g_attn, w_in, na_rpb, g_q, w_uq, g_kv, w_ukv, w_o,
                         g_ffn, w_router, w_gate, w_up, w_down, g_final)
    y_sample = run_group(x_sample, meta_tokens, g_attn, w_in, na_rpb, g_q, w_uq, g_kv, w_ukv, w_o,
                         g_ffn, w_router, w_gate, w_up, w_down, g_final)
    return (y_prompt, y_sample)
```

```python
import functools
import math

import jax
import jax.numpy as jnp
import numpy as np
from jax import lax
from jax.experimental import pallas as pl
from jax.experimental.pallas import tpu as pltpu

D_MODEL = 1024
GRID_W = 64
N_META = 16
WIN_R = 8
WIN_C = 16
NA_HEADS = 8
NA_HEAD_DIM = 64
MLA_HEADS = 8
QK_NOPE = 64
QK_ROPE = 32
V_HEAD = 64
Q_LORA = 768
KV_LORA = 256
ROPE_THETA = 10000.0
N_EXPERTS = 16
EC_CAPACITY = 2
D_EXPERT = 2048
NORM_EPS = 1e-6
NA_WIDTH = NA_HEADS * NA_HEAD_DIM
MLA_WIDTH = MLA_HEADS * V_HEAD

LANES = 128
ROW_TILE = 256
NA_ROWS = 4
NA_BLK = NA_ROWS * GRID_W
NEG = -1e30
LOG2E = 1.4426950408889634
VMEM_LIMIT = 56 * 1024 * 1024

_BF = jnp.bfloat16
_F32 = jnp.float32


def _dot(a, b):
    return jnp.dot(a, b, preferred_element_type=_F32)


def _dot_nt(a, b):
    return lax.dot_general(a, b, (((1,), (1,)), ((), ())), preferred_element_type=_F32)


def _rms(x, g):
    return x * lax.rsqrt(jnp.mean(x * x, axis=-1, keepdims=True) + NORM_EPS) * g


def _proj_kernel(x_ref, cos_ref, sin_ref, g_attn_ref, g_q_ref, g_kv_ref, w1_ref, w2_ref, w3k_ref, w3v_ref,
                 qkv_ref, q_ref, k_ref, v_ref):
    a = _rms(x_ref[...], g_attn_ref[...]).astype(_BF)
    p = _dot(a, w1_ref[...])
    qkv_ref[...] = p[:, :3 * NA_WIDTH].astype(_BF)
    o = 3 * NA_WIDTH
    cqn = _rms(p[:, o:o + Q_LORA], g_q_ref[...]).astype(_BF)
    o += Q_LORA
    ckvn = _rms(p[:, o:o + KV_LORA], g_kv_ref[...]).astype(_BF)
    o += KV_LORA
    cos = cos_ref[...]
    sin = sin_ref[...]
    kr = p[:, o:o + LANES] * cos + p[:, o + LANES:o + 2 * LANES] * sin
    q2 = _dot(cqn, w2_ref[...])
    k3 = _dot(ckvn, w3k_ref[...])
    hw = MLA_HEADS * LANES
    for h in range(MLA_HEADS):
        sl = slice(h * LANES, (h + 1) * LANES)
        q_ref[:, sl] = (q2[:, sl] * cos + q2[:, hw + h * LANES:hw + (h + 1) * LANES] * sin).astype(_BF)
        k_ref[:, sl] = (k3[:, sl] + kr).astype(_BF)
    v_ref[...] = _dot(ckvn, w3v_ref[...]).astype(_BF)


def _proj(x, cos, sin, g_attn, g_q, g_kv, w1, w2, w3k, w3v):
    np_rows = x.shape[0]
    row = lambda i: (i, 0)
    full = lambda i: (0, 0)
    w = lambda a: pl.BlockSpec(a.shape, full)
    return pl.pallas_call(
        _proj_kernel,
        grid=(np_rows // ROW_TILE,),
        in_specs=[pl.BlockSpec((ROW_TILE, D_MODEL), row), pl.BlockSpec((ROW_TILE, LANES), row),
                  pl.BlockSpec((ROW_TILE, LANES), row), w(g_attn), w(g_q), w(g_kv), w(w1), w(w2), w(w3k), w(w3v)],
        out_specs=[pl.BlockSpec((ROW_TILE, 3 * NA_WIDTH), row), pl.BlockSpec((ROW_TILE, MLA_HEADS * LANES), row),
                   pl.BlockSpec((ROW_TILE, MLA_HEADS * LANES), row), pl.BlockSpec((ROW_TILE, MLA_WIDTH), row)],
        out_shape=[jax.ShapeDtypeStruct((np_rows, 3 * NA_WIDTH), _BF),
                   jax.ShapeDtypeStruct((np_rows, MLA_HEADS * LANES), _BF),
                   jax.ShapeDtypeStruct((np_rows, MLA_HEADS * LANES), _BF),
                   jax.ShapeDtypeStruct((np_rows, MLA_WIDTH), _BF)],
        compiler_params=pltpu.CompilerParams(dimension_semantics=("parallel",), vmem_limit_bytes=VMEM_LIMIT),
        name="proj",
    )(x, cos, sin, g_attn, g_q, g_kv, w1, w2, w3k, w3v)


def _na_kernel(q_ref, kp_ref, kc_ref, kn_ref, vp_ref, vc_ref, vn_ref, km_ref, vm_ref, bias_ref, o_ref):
    ks = (kp_ref, kc_ref, kn_ref)
    vs = (vp_ref, vc_ref, vn_ref)
    for h in range(NA_HEADS):
        sl = slice(h * NA_HEAD_DIM, (h + 1) * NA_HEAD_DIM)
        q = q_ref[:, sl]
        s = [_dot_nt(q, ks[i][:, sl]) + bias_ref[h, :, i * NA_BLK:(i + 1) * NA_BLK] for i in range(3)]
        sm = _dot_nt(q, km_ref[:, sl])
        m = jnp.max(sm, axis=1, keepdims=True)
        for si in s:
            m = jnp.maximum(m, jnp.max(si, axis=1, keepdims=True))
        pm = jnp.exp(sm - m)
        l = jnp.sum(pm, axis=1, keepdims=True)
        acc = _dot(pm.astype(_BF), vm_ref[:, sl])
        for i in range(3):
            p = jnp.exp(s[i] - m)
            l = l + jnp.sum(p, axis=1, keepdims=True)
            acc = acc + _dot(p.astype(_BF), vs[i][:, sl])
        o_ref[:, sl] = (acc / l).astype(_BF)


def _na_attention(qkv, bias, batch, seq):
    nblk = seq // NA_BLK
    meta_blk0 = batch * seq // N_META

    def kv_spec(off, col):
        return pl.BlockSpec((NA_BLK, NA_WIDTH),
                            lambda b, j: (b * nblk + jnp.clip(j + off, 0, nblk - 1), col))

    meta = lambda col: pl.BlockSpec((N_META, NA_WIDTH), lambda b, j: (meta_blk0 + b, col))
    cls = lambda b, j: (jnp.where(j == 0, 0, jnp.where(j == nblk - 1, 2, 1)), 0, 0, 0)
    return pl.pallas_call(
        _na_kernel,
        grid=(batch, nblk),
        in_specs=[pl.BlockSpec((NA_BLK, NA_WIDTH), lambda b, j: (b * nblk + j, 0)),
                  kv_spec(-1, 1), kv_spec(0, 1), kv_spec(1, 1), kv_spec(-1, 2), kv_spec(0, 2), kv_spec(1, 2),
                  meta(1), meta(2),
                  pl.BlockSpec((None, NA_HEADS, NA_BLK, 3 * NA_BLK), cls)],
        out_specs=pl.BlockSpec((NA_BLK, NA_WIDTH), lambda b, j: (b * nblk + j, 0)),
        out_shape=jax.ShapeDtypeStruct((batch * seq, NA_WIDTH), _BF),
        compiler_params=pltpu.CompilerParams(dimension_semantics=("parallel", "arbitrary"),
                                             vmem_limit_bytes=VMEM_LIMIT),
        name="na_attn",
    )(qkv, qkv, qkv, qkv, qkv, qkv, qkv, qkv, qkv, bias)


def _na_meta_kernel(q_ref, k_ref, v_ref, o_ref):
    for h in range(NA_HEADS):
        sl = slice(h * NA_HEAD_DIM, (h + 1) * NA_HEAD_DIM)
        s = _dot_nt(q_ref[:, sl], k_ref[:, sl])
        p = jnp.exp(s - jnp.max(s, axis=1, keepdims=True))
        l = jnp.sum(p, axis=1, keepdims=True)
        o_ref[:, sl] = (_dot(p.astype(_BF), v_ref[:, sl]) / l).astype(_BF)


def _na_meta_attention(qkv, batch, seq):
    meta_blk0 = batch * seq // N_META
    spec = lambda col: pl.BlockSpec((N_META, NA_WIDTH), lambda b: (meta_blk0 + b, col))
    return pl.pallas_call(
        _na_meta_kernel,
        grid=(batch,),
        in_specs=[spec(0), spec(1), spec(2)],
        out_specs=pl.BlockSpec((N_META, NA_WIDTH), lambda b: (b, 0)),
        out_shape=jax.ShapeDtypeStruct((batch * N_META, NA_WIDTH), _BF),
        compiler_params=pltpu.CompilerParams(dimension_semantics=("parallel",)),
        name="na_meta_attn",
    )(qkv, qkv, qkv)


def _mla_kernel(q_ref, k_ref, v_ref, km_ref, vm_ref, o_ref, m_sc, l_sc, acc_sc):
    j = pl.program_id(2)

    def update(kb_ref, vb_ref):
        for h in range(MLA_HEADS):
            q = q_ref[:, h * LANES:(h + 1) * LANES]
            s = _dot_nt(q, kb_ref[:, h * LANES:(h + 1) * LANES])
            m_prev = m_sc[h]
            m_new = jnp.maximum(m_prev, jnp.max(s, axis=1, keepdims=True))
            alpha = jnp.exp2(m_prev - m_new)
            p = jnp.exp2(s - m_new[:, :1])
            l_sc[h] = alpha * l_sc[h] + jnp.sum(p, axis=1, keepdims=True)
            acc_sc[h] = acc_sc[h] * alpha[:, :V_HEAD] + _dot(p.astype(_BF), vb_ref[:, h * V_HEAD:(h + 1) * V_HEAD])
            m_sc[h] = m_new

    @pl.when(j == 0)
    def _():
        m_sc[...] = jnp.full(m_sc.shape, NEG, _F32)
        l_sc[...] = jnp.zeros(l_sc.shape, _F32)
        acc_sc[...] = jnp.zeros(acc_sc.shape, _F32)
        update(km_ref, vm_ref)

    update(k_ref, v_ref)

    @pl.when(j == pl.num_programs(2) - 1)
    def _():
        for h in range(MLA_HEADS):
            o_ref[:, h * V_HEAD:(h + 1) * V_HEAD] = (acc_sc[h] / l_sc[h][:, :V_HEAD]).astype(_BF)


def _mla_attention(q, k, v, batch, seq, q_rows, q_blk0, tq, tk):
    nq = q_rows // tq
    nk = seq // tk
    meta_blk0 = batch * seq // N_META
    hw = MLA_HEADS * LANES
    return pl.pallas_call(
        _mla_kernel,
        grid=(batch, nq, nk),
        in_specs=[pl.BlockSpec((tq, hw), lambda b, i, j: (q_blk0 + b * nq + i, 0)),
                  pl.BlockSpec((tk, hw), lambda b, i, j: (b * nk + j, 0)),
                  pl.BlockSpec((tk, MLA_WIDTH), lambda b, i, j: (b * nk + j, 0)),
                  pl.BlockSpec((N_META, hw), lambda b, i, j: (meta_blk0 + b, 0)),
                  pl.BlockSpec((N_META, MLA_WIDTH), lambda b, i, j: (meta_blk0 + b, 0))],
        out_specs=pl.BlockSpec((tq, MLA_WIDTH), lambda b, i, j: (b * nq + i, 0)),
        out_shape=jax.ShapeDtypeStruct((batch * q_rows, MLA_WIDTH), _BF),
        scratch_shapes=[pltpu.VMEM((MLA_HEADS, tq, LANES), _F32), pltpu.VMEM((MLA_HEADS, tq, LANES), _F32),
                        pltpu.VMEM((MLA_HEADS, tq, V_HEAD), _F32)],
        compiler_params=pltpu.CompilerParams(dimension_semantics=("parallel", "parallel", "arbitrary"),
                                             vmem_limit_bytes=VMEM_LIMIT),
        name="mla_attn",
    )(q, k, v, k, v)


def _outproj_kernel(x_ref, ong_ref, onm_ref, omg_ref, omm_ref, wo_na_ref, wo_mla_ref, g_ref, wr_hi_ref, wr_lo_ref,
                    h_ref, hn_ref, aff_ref, *, n_grid_tiles, n_valid):
    i = pl.program_id(0)
    is_meta = i == n_grid_tiles
    o_na = jnp.where(is_meta, onm_ref[...], ong_ref[...])
    o_mla = jnp.where(is_meta, omm_ref[...], omg_ref[...])
    h = x_ref[...] + _dot(o_na, wo_na_ref[...]) + _dot(o_mla, wo_mla_ref[...])
    h_ref[...] = h
    hn = _rms(h, g_ref[...])
    hn_hi = hn.astype(_BF)
    hn_ref[...] = hn_hi
    hn_lo = (hn - hn_hi.astype(_F32)).astype(_BF)
    wr_hi = wr_hi_ref[...]
    logits = _dot_nt(wr_hi, hn_hi) + _dot_nt(wr_hi, hn_lo) + _dot_nt(wr_lo_ref[...], hn_hi)
    e = jnp.exp(logits - jnp.max(logits, axis=0, keepdims=True))
    aff = e / jnp.sum(e, axis=0, keepdims=True)
    row = i * ROW_TILE + lax.broadcasted_iota(jnp.int32, aff.shape, 1)
    aff_ref[...] = jnp.where(row < n_valid, aff, -1.0)


def _outproj(x, o_na_g, o_na_m, o_mla_g, o_mla_m, wo_na, wo_mla, g_ffn, wr_hi, wr_lo, n_valid):
    np_rows = x.shape[0]
    n_tiles = np_rows // ROW_TILE
    n_grid_tiles = n_tiles - 1
    row = lambda i: (i, 0)
    grid_row = lambda i: (jnp.minimum(i, n_grid_tiles - 1), 0)
    full = lambda i: (0, 0)
    w = lambda a: pl.BlockSpec(a.shape, full)
    return pl.pallas_call(
        functools.partial(_outproj_kernel, n_grid_tiles=n_grid_tiles, n_valid=n_valid),
        grid=(n_tiles,),
        in_specs=[pl.BlockSpec((ROW_TILE, D_MODEL), row),
                  pl.BlockSpec((ROW_TILE, NA_WIDTH), grid_row), w(o_na_m),
                  pl.BlockSpec((ROW_TILE, MLA_WIDTH), grid_row), w(o_mla_m),
                  w(wo_na), w(wo_mla), w(g_ffn), w(wr_hi), w(wr_lo)],
        out_specs=[pl.BlockSpec((ROW_TILE, D_MODEL), row), pl.BlockSpec((ROW_TILE, D_MODEL), row),
                   pl.BlockSpec((N_EXPERTS, ROW_TILE), lambda i: (0, i))],
        out_shape=[jax.ShapeDtypeStruct((np_rows, D_MODEL), _F32), jax.ShapeDtypeStruct((np_rows, D_MODEL), _BF),
                   jax.ShapeDtypeStruct((N_EXPERTS, np_rows), _F32)],
        compiler_params=pltpu.CompilerParams(dimension_semantics=("parallel",), vmem_limit_bytes=VMEM_LIMIT),
        name="outproj_router",
    )(x, o_na_g, o_na_m, o_mla_g, o_mla_m, wo_na, wo_mla, g_ffn, wr_hi, wr_lo)


def _expert_kernel(x_ref, gate_ref, wg_ref, wu_ref, wd_ref, y_ref):
    x = x_ref[...]
    g = _dot(x, wg_ref[...])
    u = _dot(x, wu_ref[...])
    hid = (g * jax.nn.sigmoid(g) * u).astype(_BF)
    y_ref[...] = _dot(hid, wd_ref[...]) * gate_ref[...]


def _experts(xe, gates, wg, wu, wd, tile):
    n_e, cap, _ = xe.shape
    wspec = lambda a: pl.BlockSpec((None,) + a.shape[1:], lambda e, t: (e, 0, 0))
    return pl.pallas_call(
        _expert_kernel,
        grid=(n_e, cap // tile),
        in_specs=[pl.BlockSpec((None, tile, D_MODEL), lambda e, t: (e, t, 0)),
                  pl.BlockSpec((None, tile, 1), lambda e, t: (e, t, 0)), wspec(wg), wspec(wu), wspec(wd)],
        out_specs=pl.BlockSpec((None, tile, D_MODEL), lambda e, t: (e, t, 0)),
        out_shape=jax.ShapeDtypeStruct((n_e, cap, D_MODEL), _F32),
        compiler_params=pltpu.CompilerParams(dimension_semantics=("parallel", "arbitrary"),
                                             vmem_limit_bytes=VMEM_LIMIT),
        name="experts",
    )(xe, gates, wg, wu, wd)


def _final_kernel(h_ref, g_ref, o_ref):
    o_ref[...] = _rms(h_ref[...], g_ref[...])


def _final_norm(h, g_final, n_rows):
    row = lambda i: (i, 0)
    return pl.pallas_call(
        _final_kernel,
        grid=(n_rows // ROW_TILE,),
        in_specs=[pl.BlockSpec((ROW_TILE, D_MODEL), row), pl.BlockSpec(g_final.shape, lambda i: (0, 0))],
        out_specs=pl.BlockSpec((ROW_TILE, D_MODEL), row),
        out_shape=jax.ShapeDtypeStruct((n_rows, D_MODEL), _F32),
        compiler_params=pltpu.CompilerParams(dimension_semantics=("parallel",)),
        name="final_norm",
    )(h, g_final)


def _rotate_half_cols(w):
    half = QK_ROPE // 2
    return jnp.concatenate([-w[..., half:], w[..., :half]], axis=-1)


def _prepare_weights(w_in, g_attn, g_q, w_uq, g_kv, w_ukv, w_o, g_ffn, w_router, w_gate, w_up, w_down, na_rpb):
    na_scale = 1.0 / math.sqrt(NA_HEAD_DIM)
    c0, c1, c2 = 3 * NA_WIDTH, 3 * NA_WIDTH + Q_LORA, 3 * NA_WIDTH + Q_LORA + KV_LORA
    w_kr = w_in[:, c2:]
    zpad = lambda n: jnp.zeros((D_MODEL, n), _F32)
    slot = lambda w: jnp.concatenate([zpad(QK_NOPE), w, zpad(LANES - QK_NOPE - QK_ROPE)], axis=1)
    w1 = jnp.concatenate([w_in[:, :NA_WIDTH] * na_scale, w_in[:, NA_WIDTH:c2], slot(w_kr),
                          slot(_rotate_half_cols(w_kr))], axis=1).astype(_BF)
    q_scale = LOG2E / math.sqrt(QK_NOPE + QK_ROPE)
    wq = (w_uq * q_scale).reshape(Q_LORA, MLA_HEADS, QK_NOPE + QK_ROPE)
    pad = LANES - QK_NOPE - QK_ROPE
    plain = jnp.pad(wq, ((0, 0), (0, 0), (0, pad)))
    rot = jnp.pad(_rotate_half_cols(wq[..., QK_NOPE:]), ((0, 0), (0, 0), (QK_NOPE, pad)))
    w2 = jnp.concatenate([plain.reshape(Q_LORA, -1), rot.reshape(Q_LORA, -1)], axis=1).astype(_BF)
    wkv = w_ukv.reshape(KV_LORA, MLA_HEADS, QK_NOPE + V_HEAD)
    w3k = jnp.pad(wkv[..., :QK_NOPE], ((0, 0), (0, 0), (0, LANES - QK_NOPE))).reshape(KV_LORA, -1).astype(_BF)
    w3v = wkv[..., QK_NOPE:].reshape(KV_LORA, -1).astype(_BF)
    wr = w_router.T
    wr_hi = wr.astype(_BF)
    wr_lo = (wr - wr_hi.astype(_F32)).astype(_BF)
    return dict(
        w1=w1, w2=w2, w3k=w3k, w3v=w3v,
        g_attn=g_attn[None], g_q=g_q[None], g_kv=g_kv[None], g_ffn=g_ffn[None],
        wo_na=w_o[:NA_WIDTH].astype(_BF), wo_mla=w_o[NA_WIDTH:].astype(_BF),
        wr_hi=wr_hi, wr_lo=wr_lo,
        wg=w_gate.astype(_BF), wu=w_up.astype(_BF), wd=w_down.astype(_BF),
        na_bias=_na_bias_table(na_rpb),
    )


def _na_bias_table(rpb):
    ql = np.arange(NA_BLK)
    kl = np.arange(3 * NA_BLK)
    ri, qc = ql // GRID_W, ql % GRID_W
    kj, kc = kl // GRID_W, kl % GRID_W
    dr = kj[None, :] - ri[:, None] + (WIN_R - 1) - NA_ROWS
    qcs = np.clip(qc - WIN_C // 2, 0, GRID_W - WIN_C)
    col_ok = (kc[None, :] >= qcs[:, None]) & (kc[None, :] < qcs[:, None] + WIN_C)
    dc = np.clip(kc[None, :] - qc[:, None] + (WIN_C - 1), 0, 2 * WIN_C - 2)
    lo = np.stack([np.full(NA_BLK, NA_ROWS), ri, np.zeros(NA_BLK, np.int64)])
    row_ok = (kj[None, None, :] >= lo[:, :, None]) & (kj[None, None, :] < lo[:, :, None] + WIN_R)
    ok = row_ok & col_ok[None]
    vals = rpb[:, np.clip(dr, 0, 2 * WIN_R - 2), dc].astype(_F32)
    return jnp.where(jnp.asarray(ok)[:, None], vals[None], NEG)


def _rope_tables(batch, seq, np_rows):
    half = QK_ROPE // 2
    inv_freq = ROPE_THETA ** (-jnp.arange(half, dtype=_F32) * 2.0 / QK_ROPE)
    pos_grid = jnp.tile(jnp.arange(seq, dtype=_F32) + N_META, batch)
    pos_meta = jnp.tile(jnp.arange(N_META, dtype=_F32), batch)
    pos = jnp.concatenate([pos_grid, pos_meta, jnp.zeros(np_rows - batch * (seq + N_META), _F32)])
    ang = pos[:, None] * inv_freq[None, :]
    cos, sin = jnp.cos(ang), jnp.sin(ang)
    pad = LANES - QK_NOPE - QK_ROPE
    cos_t = jnp.concatenate([jnp.ones((np_rows, QK_NOPE), _F32), cos, cos, jnp.zeros((np_rows, pad), _F32)], axis=1)
    sin_t = jnp.concatenate([jnp.zeros((np_rows, QK_NOPE), _F32), sin, sin, jnp.zeros((np_rows, pad), _F32)], axis=1)
    return cos_t, sin_t


def _expert_tile(cap):
    return min(range(16, 1025, 16), key=lambda t: (-(-cap // t) * t - cap, -t))


def _run_group(x, meta_tokens, prm, g_final, tq, tk):
    batch, seq, _ = x.shape
    n_grid = batch * seq
    n_valid = n_grid + batch * N_META
    np_rows = n_grid + ROW_TILE
    assert seq % NA_BLK == 0 and seq // GRID_W >= WIN_R and batch * N_META <= ROW_TILE
    meta = jnp.broadcast_to(meta_tokens[None], (batch, N_META, D_MODEL)).reshape(batch * N_META, D_MODEL)
    xt = jnp.concatenate([x.reshape(n_grid, D_MODEL), meta,
                          jnp.zeros((np_rows - n_valid, D_MODEL), _F32)], axis=0)
    cos_t, sin_t = _rope_tables(batch, seq, np_rows)
    qkv, q, k, v = _proj(xt, cos_t, sin_t, prm["g_attn"], prm["g_q"], prm["g_kv"],
                         prm["w1"], prm["w2"], prm["w3k"], prm["w3v"])
    o_na_g = _na_attention(qkv, prm["na_bias"], batch, seq)
    o_na_m = _na_meta_attention(qkv, batch, seq)
    o_mla_g = _mla_attention(q, k, v, batch, seq, seq, 0, tq, tk)
    o_mla_m = _mla_attention(q, k, v, batch, seq, N_META, n_grid // N_META, N_META, tk)
    pad_meta = lambda a: jnp.pad(a, ((0, ROW_TILE - a.shape[0]), (0, 0)))
    h, hn, aff_t = _outproj(xt, o_na_g, pad_meta(o_na_m), o_mla_g, pad_meta(o_mla_m), prm["wo_na"], prm["wo_mla"],
                            prm["g_ffn"], prm["wr_hi"], prm["wr_lo"], n_valid)
    cap = EC_CAPACITY * n_valid // N_EXPERTS
    gates, idx = lax.top_k(aff_t, cap)
    tile = _expert_tile(cap)
    cap_pad = -(-cap // tile) * tile
    if cap_pad != cap:
        idx = jnp.pad(idx, ((0, 0), (0, cap_pad - cap)), constant_values=np_rows - 1)
        gates = jnp.pad(gates, ((0, 0), (0, cap_pad - cap)))
    xe = hn[idx]
    ye = _experts(xe, gates[..., None], prm["wg"], prm["wu"], prm["wd"], tile)
    h2 = h.at[idx.reshape(-1)].add(ye.reshape(-1, D_MODEL))
    return _final_norm(h2, g_final[None], n_grid).reshape(batch, seq, D_MODEL)


def kernel(x_prompt, x_sample, meta_tokens, g_attn, w_in, na_rpb, g_q, w_uq, g_kv, w_ukv, w_o, g_ffn, w_router,
           w_gate, w_up, w_down, g_final):
    prm = _prepare_weights(w_in[0], g_attn[0], g_q[0], w_uq[0], g_kv[0], w_ukv[0], w_o[0], g_ffn[0], w_router[0],
                           w_gate[0], w_up[0], w_down[0], na_rpb[0])
    y_prompt = _run_group(x_prompt, meta_tokens, prm, g_final, 256, 512)
    y_sample = _run_group(x_sample, meta_tokens, prm, g_final, 256, 512)
    return (y_prompt, y_sample)
```

```python
import functools
import math

import jax
import jax.numpy as jnp
import numpy as np
from jax import lax
from jax.experimental import pallas as pl
from jax.experimental.pallas import tpu as pltpu

D_MODEL = 1024
GRID_W = 64
N_META = 16
WIN_R = 8
WIN_C = 16
NA_HEADS = 8
NA_HEAD_DIM = 64
MLA_HEADS = 8
QK_NOPE = 64
QK_ROPE = 32
V_HEAD = 64
Q_LORA = 768
KV_LORA = 256
ROPE_THETA = 10000.0
N_EXPERTS = 16
EC_CAPACITY = 2
D_EXPERT = 2048
NORM_EPS = 1e-6
NA_WIDTH = NA_HEADS * NA_HEAD_DIM
MLA_WIDTH = MLA_HEADS * V_HEAD

LANES = 128
ROW_TILE = 256
NA_ROWS = 4
NA_BLK = NA_ROWS * GRID_W
NEG = -1e30
LOG2E = 1.4426950408889634
VMEM_LIMIT = 56 * 1024 * 1024

_BF = jnp.bfloat16
_F32 = jnp.float32


def _dot(a, b):
    return jnp.dot(a, b, preferred_element_type=_F32)


def _dot_nt(a, b):
    return lax.dot_general(a, b, (((1,), (1,)), ((), ())), preferred_element_type=_F32)


def _rms(x, g):
    return x * lax.rsqrt(jnp.mean(x * x, axis=-1, keepdims=True) + NORM_EPS) * g


def _proj_kernel(x_ref, cos_ref, sin_ref, g_attn_ref, g_q_ref, g_kv_ref, w1_ref, w2_ref, w3k_ref, w3v_ref,
                 qkv_ref, q_ref, k_ref, v_ref):
    a = _rms(x_ref[...], g_attn_ref[...]).astype(_BF)
    p = _dot(a, w1_ref[...])
    qkv_ref[...] = p[:, :3 * NA_WIDTH].astype(_BF)
    o = 3 * NA_WIDTH
    cqn = _rms(p[:, o:o + Q_LORA], g_q_ref[...]).astype(_BF)
    o += Q_LORA
    ckvn = _rms(p[:, o:o + KV_LORA], g_kv_ref[...]).astype(_BF)
    o += KV_LORA
    cos = cos_ref[...]
    sin = sin_ref[...]
    kr = p[:, o:o + LANES] * cos + p[:, o + LANES:o + 2 * LANES] * sin
    q2 = _dot(cqn, w2_ref[...])
    k3 = _dot(ckvn, w3k_ref[...])
    hw = MLA_HEADS * LANES
    for h in range(MLA_HEADS):
        sl = slice(h * LANES, (h + 1) * LANES)
        q_ref[:, sl] = (q2[:, sl] * cos + q2[:, hw + h * LANES:hw + (h + 1) * LANES] * sin).astype(_BF)
        k_ref[:, sl] = (k3[:, sl] + kr).astype(_BF)
    v_ref[...] = _dot(ckvn, w3v_ref[...]).astype(_BF)


def _proj(x, cos, sin, g_attn, g_q, g_kv, w1, w2, w3k, w3v):
    np_rows = x.shape[0]
    row = lambda i: (i, 0)
    full = lambda i: (0, 0)
    w = lambda a: pl.BlockSpec(a.shape, full)
    return pl.pallas_call(
        _proj_kernel,
        grid=(np_rows // ROW_TILE,),
        in_specs=[pl.BlockSpec((ROW_TILE, D_MODEL), row), pl.BlockSpec((ROW_TILE, LANES), row),
                  pl.BlockSpec((ROW_TILE, LANES), row), w(g_attn), w(g_q), w(g_kv), w(w1), w(w2), w(w3k), w(w3v)],
        out_specs=[pl.BlockSpec((ROW_TILE, 3 * NA_WIDTH), row), pl.BlockSpec((ROW_TILE, MLA_HEADS * LANES), row),
                   pl.BlockSpec((ROW_TILE, MLA_HEADS * LANES), row), pl.BlockSpec((ROW_TILE, MLA_WIDTH), row)],
        out_shape=[jax.ShapeDtypeStruct((np_rows, 3 * NA_WIDTH), _BF),
                   jax.ShapeDtypeStruct((np_rows, MLA_HEADS * LANES), _BF),
                   jax.ShapeDtypeStruct((np_rows, MLA_HEADS * LANES), _BF),
                   jax.ShapeDtypeStruct((np_rows, MLA_WIDTH), _BF)],
        compiler_params=pltpu.CompilerParams(dimension_semantics=("parallel",), vmem_limit_bytes=VMEM_LIMIT),
        name="proj",
    )(x, cos, sin, g_attn, g_q, g_kv, w1, w2, w3k, w3v)


def _na_kernel(q_ref, kp_ref, kc_ref, kn_ref, vp_ref, vc_ref, vn_ref, km_ref, vm_ref, bias_ref, o_ref):
    ks = (kp_ref, kc_ref, kn_ref)
    vs = (vp_ref, vc_ref, vn_ref)
    for h in range(NA_HEADS):
        sl = slice(h * NA_HEAD_DIM, (h + 1) * NA_HEAD_DIM)
        q = q_ref[:, sl]
        s = [_dot_nt(q, ks[i][:, sl]) + bias_ref[h, :, i * NA_BLK:(i + 1) * NA_BLK] for i in range(3)]
        sm = _dot_nt(q, km_ref[:, sl])
        m = jnp.max(sm, axis=1, keepdims=True)
        for si in s:
            m = jnp.maximum(m, jnp.max(si, axis=1, keepdims=True))
        pm = jnp.exp(sm - m)
        l = jnp.sum(pm, axis=1, keepdims=True)
        acc = _dot(pm.astype(_BF), vm_ref[:, sl])
        for i in range(3):
            p = jnp.exp(s[i] - m)
            l = l + jnp.sum(p, axis=1, keepdims=True)
            acc = acc + _dot(p.astype(_BF), vs[i][:, sl])
        o_ref[:, sl] = (acc / l).astype(_BF)


def _na_attention(qkv, bias, batch, seq):
    nblk = seq // NA_BLK
    meta_blk0 = batch * seq // N_META

    def kv_spec(off, col):
        return pl.BlockSpec((NA_BLK, NA_WIDTH),
                            lambda b, j: (b * nblk + jnp.clip(j + off, 0, nblk - 1), col))

    meta = lambda col: pl.BlockSpec((N_META, NA_WIDTH), lambda b, j: (meta_blk0 + b, col))
    cls = lambda b, j: (jnp.where(j == 0, 0, jnp.where(j == nblk - 1, 2, 1)), 0, 0, 0)
    return pl.pallas_call(
        _na_kernel,
        grid=(batch, nblk),
        in_specs=[pl.BlockSpec((NA_BLK, NA_WIDTH), lambda b, j: (b * nblk + j, 0)),
                  kv_spec(-1, 1), kv_spec(0, 1), kv_spec(1, 1), kv_spec(-1, 2), kv_spec(0, 2), kv_spec(1, 2),
                  meta(1), meta(2),
                  pl.BlockSpec((None, NA_HEADS, NA_BLK, 3 * NA_BLK), cls)],
        out_specs=pl.BlockSpec((NA_BLK, NA_WIDTH), lambda b, j: (b * nblk + j, 0)),
        out_shape=jax.ShapeDtypeStruct((batch * seq, NA_WIDTH), _BF),
        compiler_params=pltpu.CompilerParams(dimension_semantics=("parallel", "arbitrary"),
                                             vmem_limit_bytes=VMEM_LIMIT),
        name="na_attn",
    )(qkv, qkv, qkv, qkv, qkv, qkv, qkv, qkv, qkv, bias)


def _na_meta_kernel(q_ref, k_ref, v_ref, o_ref):
    for h in range(NA_HEADS):
        sl = slice(h * NA_HEAD_DIM, (h + 1) * NA_HEAD_DIM)
        s = _dot_nt(q_ref[:, sl], k_ref[:, sl])
        p = jnp.exp(s - jnp.max(s, axis=1, keepdims=True))
        l = jnp.sum(p, axis=1, keepdims=True)
        o_ref[:, sl] = (_dot(p.astype(_BF), v_ref[:, sl]) / l).astype(_BF)


def _na_meta_attention(qkv, batch, seq):
    meta_blk0 = batch * seq // N_META
    spec = lambda col: pl.BlockSpec((N_META, NA_WIDTH), lambda b: (meta_blk0 + b, col))
    return pl.pallas_call(
        _na_meta_kernel,
        grid=(batch,),
        in_specs=[spec(0), spec(1), spec(2)],
        out_specs=pl.BlockSpec((N_META, NA_WIDTH), lambda b: (b, 0)),
        out_shape=jax.ShapeDtypeStruct((batch * N_META, NA_WIDTH), _BF),
        compiler_params=pltpu.CompilerParams(dimension_semantics=("parallel",)),
        name="na_meta_attn",
    )(qkv, qkv, qkv)


V_EXT = V_HEAD + 16
MLA_CHUNK = 256
MLA_LOOKAHEAD = 5
MLA_TQ = 256
MLA_TK = 2048


def _mla_kernel(qt_ref, k_ref, vt_ref, km_ref, vmt_ref, ot_ref, m_sc, acc_sc, *, chunk):
    j = pl.program_id(2)
    tq = qt_ref.shape[2]
    nchunks = k_ref.shape[1] // chunk

    @pl.when(j == 0)
    def _():
        def init_head(h, carry):
            s = _dot(km_ref[h], qt_ref[h])
            m = jnp.max(s, axis=0, keepdims=True)
            p = jnp.exp2(s - m).astype(_BF)
            acc_sc[h] = _dot(vmt_ref[h], p)
            m_sc[h] = jnp.broadcast_to(m, (8, tq))
            return carry
        lax.fori_loop(0, MLA_HEADS, init_head, 0)

    steps = [(h, c) for h in range(MLA_HEADS) for c in range(nchunks)]
    scores = lambda h, c: _dot(k_ref[h, c * chunk:(c + 1) * chunk, :], qt_ref[h])
    pending = [scores(*st) for st in steps[:MLA_LOOKAHEAD]]
    for n, (h, c) in enumerate(steps):
        if n + MLA_LOOKAHEAD < len(steps):
            pending.append(scores(*steps[n + MLA_LOOKAHEAD]))
        s = pending.pop(0)
        if c == 0:
            m = m_sc[h][:1]
            acc = acc_sc[h]
        m_new = jnp.maximum(m, jnp.max(s, axis=0, keepdims=True))
        p = jnp.exp2(s - m_new).astype(_BF)
        acc = acc * jnp.exp2(m - m_new) + _dot(vt_ref[h, :, c * chunk:(c + 1) * chunk], p)
        m = m_new
        if c == nchunks - 1:
            m_sc[h] = jnp.broadcast_to(m, (8, tq))
            acc_sc[h] = acc

    @pl.when(j == pl.num_programs(2) - 1)
    def _():
        for h in range(MLA_HEADS):
            acc = acc_sc[h]
            ot_ref[h] = (acc[:V_HEAD] / acc[V_HEAD:V_HEAD + 1]).astype(_BF)


def _mla_attention(qt, k, vt, km, vmt, batch, seq, q_rows, q_blk0, q_stride, tq, tk):
    nq = q_rows // tq
    nk = seq // tk
    return pl.pallas_call(
        functools.partial(_mla_kernel, chunk=min(MLA_CHUNK, tk)),
        grid=(batch, nq, nk),
        in_specs=[pl.BlockSpec((MLA_HEADS, LANES, tq), lambda b, i, j: (0, 0, q_blk0 + b * q_stride + i)),
                  pl.BlockSpec((MLA_HEADS, tk, LANES), lambda b, i, j: (0, b * nk + j, 0)),
                  pl.BlockSpec((MLA_HEADS, V_EXT, tk), lambda b, i, j: (0, 0, b * nk + j)),
                  pl.BlockSpec((None, MLA_HEADS, N_META, LANES), lambda b, i, j: (b, 0, 0, 0)),
                  pl.BlockSpec((None, MLA_HEADS, V_EXT, N_META), lambda b, i, j: (b, 0, 0, 0))],
        out_specs=pl.BlockSpec((MLA_HEADS, V_HEAD, tq), lambda b, i, j: (0, 0, b * nq + i)),
        out_shape=jax.ShapeDtypeStruct((MLA_HEADS, V_HEAD, batch * q_rows), _BF),
        scratch_shapes=[pltpu.VMEM((MLA_HEADS, 8, tq), _F32), pltpu.VMEM((MLA_HEADS, V_EXT, tq), _F32)],
        compiler_params=pltpu.CompilerParams(dimension_semantics=("parallel", "parallel", "arbitrary"),
                                             vmem_limit_bytes=VMEM_LIMIT),
        name="mla_attn",
    )(qt, k, vt, km, vmt)


def _outproj_kernel(x_ref, ong_ref, onm_ref, omg_ref, omm_ref, wo_na_ref, wo_mla_ref, g_ref, wr_hi_ref, wr_lo_ref,
                    h_ref, hn_ref, aff_ref, *, n_grid_tiles, n_valid):
    i = pl.program_id(0)
    is_meta = i == n_grid_tiles
    o_na = jnp.where(is_meta, onm_ref[...], ong_ref[...])
    o_mla = jnp.where(is_meta, omm_ref[...], omg_ref[...])
    h = x_ref[...] + _dot(o_na, wo_na_ref[...]) + _dot(o_mla, wo_mla_ref[...])
    h_ref[...] = h
    hn = _rms(h, g_ref[...])
    hn_hi = hn.astype(_BF)
    hn_ref[...] = hn_hi
    hn_lo = (hn - hn_hi.astype(_F32)).astype(_BF)
    wr_hi = wr_hi_ref[...]
    logits = _dot_nt(wr_hi, hn_hi) + _dot_nt(wr_hi, hn_lo) + _dot_nt(wr_lo_ref[...], hn_hi)
    e = jnp.exp(logits - jnp.max(logits, axis=0, keepdims=True))
    aff = e / jnp.sum(e, axis=0, keepdims=True)
    row = i * ROW_TILE + lax.broadcasted_iota(jnp.int32, aff.shape, 1)
    aff_ref[...] = jnp.where(row < n_valid, aff, -1.0)


def _outproj(x, o_na_g, o_na_m, o_mla_g, o_mla_m, wo_na, wo_mla, g_ffn, wr_hi, wr_lo, n_valid):
    np_rows = x.shape[0]
    n_tiles = np_rows // ROW_TILE
    n_grid_tiles = n_tiles - 1
    row = lambda i: (i, 0)
    grid_row = lambda i: (jnp.minimum(i, n_grid_tiles - 1), 0)
    full = lambda i: (0, 0)
    w = lambda a: pl.BlockSpec(a.shape, full)
    return pl.pallas_call(
        functools.partial(_outproj_kernel, n_grid_tiles=n_grid_tiles, n_valid=n_valid),
        grid=(n_tiles,),
        in_specs=[pl.BlockSpec((ROW_TILE, D_MODEL), row),
                  pl.BlockSpec((ROW_TILE, NA_WIDTH), grid_row), w(o_na_m),
                  pl.BlockSpec((ROW_TILE, MLA_WIDTH), grid_row), w(o_mla_m),
                  w(wo_na), w(wo_mla), w(g_ffn), w(wr_hi), w(wr_lo)],
        out_specs=[pl.BlockSpec((ROW_TILE, D_MODEL), row), pl.BlockSpec((ROW_TILE, D_MODEL), row),
                   pl.BlockSpec((N_EXPERTS, ROW_TILE), lambda i: (0, i))],
        out_shape=[jax.ShapeDtypeStruct((np_rows, D_MODEL), _F32), jax.ShapeDtypeStruct((np_rows, D_MODEL), _BF),
                   jax.ShapeDtypeStruct((N_EXPERTS, np_rows), _F32)],
        compiler_params=pltpu.CompilerParams(dimension_semantics=("parallel",), vmem_limit_bytes=VMEM_LIMIT),
        name="outproj_router",
    )(x, o_na_g, o_na_m, o_mla_g, o_mla_m, wo_na, wo_mla, g_ffn, wr_hi, wr_lo)


def _expert_kernel(x_ref, gate_ref, wg_ref, wu_ref, wd_ref, y_ref):
    x = x_ref[...]
    g = _dot(x, wg_ref[...])
    u = _dot(x, wu_ref[...])
    hid = (g * jax.nn.sigmoid(g) * u).astype(_BF)
    y_ref[...] = _dot(hid, wd_ref[...]) * gate_ref[...]


def _experts(xe, gates, wg, wu, wd, tile):
    n_e, cap, _ = xe.shape
    wspec = lambda a: pl.BlockSpec((None,) + a.shape[1:], lambda e, t: (e, 0, 0))
    return pl.pallas_call(
        _expert_kernel,
        grid=(n_e, cap // tile),
        in_specs=[pl.BlockSpec((None, tile, D_MODEL), lambda e, t: (e, t, 0)),
                  pl.BlockSpec((None, tile, 1), lambda e, t: (e, t, 0)), wspec(wg), wspec(wu), wspec(wd)],
        out_specs=pl.BlockSpec((None, tile, D_MODEL), lambda e, t: (e, t, 0)),
        out_shape=jax.ShapeDtypeStruct((n_e, cap, D_MODEL), _F32),
        compiler_params=pltpu.CompilerParams(dimension_semantics=("parallel", "arbitrary"),
                                             vmem_limit_bytes=VMEM_LIMIT),
        name="experts",
    )(xe, gates, wg, wu, wd)


def _final_kernel(h_ref, g_ref, o_ref):
    o_ref[...] = _rms(h_ref[...], g_ref[...])


def _final_norm(h, g_final, n_rows):
    row = lambda i: (i, 0)
    return pl.pallas_call(
        _final_kernel,
        grid=(n_rows // ROW_TILE,),
        in_specs=[pl.BlockSpec((ROW_TILE, D_MODEL), row), pl.BlockSpec(g_final.shape, lambda i: (0, 0))],
        out_specs=pl.BlockSpec((ROW_TILE, D_MODEL), row),
        out_shape=jax.ShapeDtypeStruct((n_rows, D_MODEL), _F32),
        compiler_params=pltpu.CompilerParams(dimension_semantics=("parallel",)),
        name="final_norm",
    )(h, g_final)


def _rotate_half_cols(w):
    half = QK_ROPE // 2
    return jnp.concatenate([-w[..., half:], w[..., :half]], axis=-1)


def _prepare_weights(w_in, g_attn, g_q, w_uq, g_kv, w_ukv, w_o, g_ffn, w_router, w_gate, w_up, w_down, na_rpb):
    na_scale = 1.0 / math.sqrt(NA_HEAD_DIM)
    c0, c1, c2 = 3 * NA_WIDTH, 3 * NA_WIDTH + Q_LORA, 3 * NA_WIDTH + Q_LORA + KV_LORA
    w_kr = w_in[:, c2:]
    zpad = lambda n: jnp.zeros((D_MODEL, n), _F32)
    slot = lambda w: jnp.concatenate([zpad(QK_NOPE), w, zpad(LANES - QK_NOPE - QK_ROPE)], axis=1)
    w1 = jnp.concatenate([w_in[:, :NA_WIDTH] * na_scale, w_in[:, NA_WIDTH:c2], slot(w_kr),
                          slot(_rotate_half_cols(w_kr))], axis=1).astype(_BF)
    q_scale = LOG2E / math.sqrt(QK_NOPE + QK_ROPE)
    wq = (w_uq * q_scale).reshape(Q_LORA, MLA_HEADS, QK_NOPE + QK_ROPE)
    pad = LANES - QK_NOPE - QK_ROPE
    plain = jnp.pad(wq, ((0, 0), (0, 0), (0, pad)))
    rot = jnp.pad(_rotate_half_cols(wq[..., QK_NOPE:]), ((0, 0), (0, 0), (QK_NOPE, pad)))
    w2 = jnp.concatenate([plain.reshape(Q_LORA, -1), rot.reshape(Q_LORA, -1)], axis=1).astype(_BF)
    wkv = w_ukv.reshape(KV_LORA, MLA_HEADS, QK_NOPE + V_HEAD)
    w3k = jnp.pad(wkv[..., :QK_NOPE], ((0, 0), (0, 0), (0, LANES - QK_NOPE))).reshape(KV_LORA, -1).astype(_BF)
    w3v = wkv[..., QK_NOPE:].reshape(KV_LORA, -1).astype(_BF)
    wr = w_router.T
    wr_hi = wr.astype(_BF)
    wr_lo = (wr - wr_hi.astype(_F32)).astype(_BF)
    return dict(
        w1=w1, w2=w2, w3k=w3k, w3v=w3v,
        g_attn=g_attn[None], g_q=g_q[None], g_kv=g_kv[None], g_ffn=g_ffn[None],
        wo_na=w_o[:NA_WIDTH].astype(_BF), wo_mla=w_o[NA_WIDTH:].astype(_BF),
        wr_hi=wr_hi, wr_lo=wr_lo,
        wg=w_gate.astype(_BF), wu=w_up.astype(_BF), wd=w_down.astype(_BF),
        na_bias=_na_bias_table(na_rpb),
    )


def _na_bias_table(rpb):
    ql = np.arange(NA_BLK)
    kl = np.arange(3 * NA_BLK)
    ri, qc = ql // GRID_W, ql % GRID_W
    kj, kc = kl // GRID_W, kl % GRID_W
    dr = kj[None, :] - ri[:, None] + (WIN_R - 1) - NA_ROWS
    qcs = np.clip(qc - WIN_C // 2, 0, GRID_W - WIN_C)
    col_ok = (kc[None, :] >= qcs[:, None]) & (kc[None, :] < qcs[:, None] + WIN_C)
    dc = np.clip(kc[None, :] - qc[:, None] + (WIN_C - 1), 0, 2 * WIN_C - 2)
    lo = np.stack([np.full(NA_BLK, NA_ROWS), ri, np.zeros(NA_BLK, np.int64)])
    row_ok = (kj[None, None, :] >= lo[:, :, None]) & (kj[None, None, :] < lo[:, :, None] + WIN_R)
    ok = row_ok & col_ok[None]
    vals = rpb[:, np.clip(dr, 0, 2 * WIN_R - 2), dc].astype(_F32)
    return jnp.where(jnp.asarray(ok)[:, None], vals[None], NEG)


def _rope_tables(batch, seq, np_rows):
    half = QK_ROPE // 2
    inv_freq = ROPE_THETA ** (-jnp.arange(half, dtype=_F32) * 2.0 / QK_ROPE)
    pos_grid = jnp.tile(jnp.arange(seq, dtype=_F32) + N_META, batch)
    pos_meta = jnp.tile(jnp.arange(N_META, dtype=_F32), batch)
    pos = jnp.concatenate([pos_grid, pos_meta, jnp.zeros(np_rows - batch * (seq + N_META), _F32)])
    ang = pos[:, None] * inv_freq[None, :]
    cos, sin = jnp.cos(ang), jnp.sin(ang)
    pad = LANES - QK_NOPE - QK_ROPE
    cos_t = jnp.concatenate([jnp.ones((np_rows, QK_NOPE), _F32), cos, cos, jnp.zeros((np_rows, pad), _F32)], axis=1)
    sin_t = jnp.concatenate([jnp.zeros((np_rows, QK_NOPE), _F32), sin, sin, jnp.zeros((np_rows, pad), _F32)], axis=1)
    return cos_t, sin_t


def _expert_tile(cap):
    if cap <= 1024:
        return -(-cap // 16) * 16
    return min(range(512, 1025, 16), key=lambda t: (-(-cap // t) * t - cap, -t))


def _run_group(x, meta_tokens, prm, g_final, tq, tk):
    batch, seq, _ = x.shape
    n_grid = batch * seq
    n_valid = n_grid + batch * N_META
    np_rows = n_grid + ROW_TILE
    assert seq % NA_BLK == 0 and seq // GRID_W >= WIN_R and batch * N_META <= ROW_TILE
    meta = jnp.broadcast_to(meta_tokens[None], (batch, N_META, D_MODEL)).reshape(batch * N_META, D_MODEL)
    xt = jnp.concatenate([x.reshape(n_grid, D_MODEL), meta,
                          jnp.zeros((np_rows - n_valid, D_MODEL), _F32)], axis=0)
    cos_t, sin_t = _rope_tables(batch, seq, np_rows)
    qkv, q, k, v = _proj(xt, cos_t, sin_t, prm["g_attn"], prm["g_q"], prm["g_kv"],
                         prm["w1"], prm["w2"], prm["w3k"], prm["w3v"])
    o_na_g = _na_attention(qkv, prm["na_bias"], batch, seq)
    o_na_m = _na_meta_attention(qkv, batch, seq)
    n_meta = batch * N_META
    qt = q.reshape(np_rows, MLA_HEADS, LANES).transpose(1, 2, 0)
    k3 = k.reshape(np_rows, MLA_HEADS, LANES).transpose(1, 0, 2)
    vt = v.reshape(np_rows, MLA_HEADS, V_HEAD).transpose(1, 2, 0)
    vt = jnp.concatenate([vt, jnp.ones((MLA_HEADS, 1, np_rows), _BF),
                          jnp.zeros((MLA_HEADS, V_EXT - V_HEAD - 1, np_rows), _BF)], axis=1)
    km = k3[:, n_grid:n_grid + n_meta].reshape(MLA_HEADS, batch, N_META, LANES).transpose(1, 0, 2, 3)
    vmt = vt[:, :, n_grid:n_grid + n_meta].reshape(MLA_HEADS, V_EXT, batch, N_META).transpose(2, 0, 1, 3)
    ot_g = _mla_attention(qt, k3, vt, km, vmt, batch, seq, seq, 0, seq // tq, tq, tk)
    o_mla_g = ot_g.transpose(2, 0, 1).reshape(n_grid, MLA_WIDTH)
    ot_m = _mla_attention(qt, k3, vt, km, vmt, batch, seq, ROW_TILE, n_grid // ROW_TILE, 0, ROW_TILE, tk)
    ot_m = ot_m.reshape(MLA_HEADS, V_HEAD, batch, ROW_TILE)
    o_mla_m = jnp.stack([ot_m[:, :, b, b * N_META:(b + 1) * N_META] for b in range(batch)])
    o_mla_m = o_mla_m.transpose(0, 3, 1, 2).reshape(n_meta, MLA_WIDTH)
    pad_meta = lambda a: jnp.pad(a, ((0, ROW_TILE - a.shape[0]), (0, 0)))
    h, hn, aff_t = _outproj(xt, o_na_g, pad_meta(o_na_m), o_mla_g, pad_meta(o_mla_m), prm["wo_na"], prm["wo_mla"],
                            prm["g_ffn"], prm["wr_hi"], prm["wr_lo"], n_valid)
    cap = EC_CAPACITY * n_valid // N_EXPERTS
    gates, idx = lax.top_k(aff_t, cap)
    tile = _expert_tile(cap)
    cap_pad = -(-cap // tile) * tile
    if cap_pad != cap:
        idx = jnp.pad(idx, ((0, 0), (0, cap_pad - cap)), constant_values=np_rows - 1)
        gates = jnp.pad(gates, ((0, 0), (0, cap_pad - cap)))
    xe = hn[idx]
    ye = _experts(xe, gates[..., None], prm["wg"], prm["wu"], prm["wd"], tile)
    h2 = h.at[idx.reshape(-1)].add(ye.reshape(-1, D_MODEL))
    return _final_norm(h2, g_final[None], n_grid).reshape(batch, seq, D_MODEL)


def kernel(x_prompt, x_sample, meta_tokens, g_attn, w_in, na_rpb, g_q, w_uq, g_kv, w_ukv, w_o, g_ffn, w_router,
           w_gate, w_up, w_down, g_final):
    prm = _prepare_weights(w_in[0], g_attn[0], g_q[0], w_uq[0], g_kv[0], w_ukv[0], w_o[0], g_ffn[0], w_router[0],
                           w_gate[0], w_up[0], w_down[0], na_rpb[0])
    y_prompt = _run_group(x_prompt, meta_tokens, prm, g_final, MLA_TQ, MLA_TK)
    y_sample = _run_group(x_sample, meta_tokens, prm, g_final, MLA_TQ, MLA_TK)
    return (y_prompt, y_sample)
```

```python
import functools
import math

import jax
import jax.numpy as jnp
import numpy as np
from jax import lax
from jax.experimental import pallas as pl
from jax.experimental.pallas import tpu as pltpu

D_MODEL = 1024
GRID_W = 64
N_META = 16
WIN_R = 8
WIN_C = 16
NA_HEADS = 8
NA_HEAD_DIM = 64
MLA_HEADS = 8
QK_NOPE = 64
QK_ROPE = 32
V_HEAD = 64
Q_LORA = 768
KV_LORA = 256
ROPE_THETA = 10000.0
N_EXPERTS = 16
EC_CAPACITY = 2
D_EXPERT = 2048
NORM_EPS = 1e-6
NA_WIDTH = NA_HEADS * NA_HEAD_DIM
MLA_WIDTH = MLA_HEADS * V_HEAD

LANES = 128
ROW_TILE = 256
NA_ROWS = 4
NA_BLK = NA_ROWS * GRID_W
NEG = -1e30
LOG2E = 1.4426950408889634
VMEM_LIMIT = 56 * 1024 * 1024

_BF = jnp.bfloat16
_F32 = jnp.float32


def _dot(a, b):
    return jnp.dot(a, b, preferred_element_type=_F32)


def _dot_nt(a, b):
    return lax.dot_general(a, b, (((1,), (1,)), ((), ())), preferred_element_type=_F32)


def _rms(x, g):
    return x * lax.rsqrt(jnp.mean(x * x, axis=-1, keepdims=True) + NORM_EPS) * g


def _proj_kernel(x_ref, cos_ref, sin_ref, g_attn_ref, g_q_ref, g_kv_ref, w1_ref, w2_ref, w3k_ref, w3v_ref,
                 qkv_ref, q_ref, k_ref, v_ref):
    a = _rms(x_ref[...], g_attn_ref[...]).astype(_BF)
    p = _dot(a, w1_ref[...])
    qkv_ref[...] = p[:, :3 * NA_WIDTH].astype(_BF)
    o = 3 * NA_WIDTH
    cqn = _rms(p[:, o:o + Q_LORA], g_q_ref[...]).astype(_BF)
    o += Q_LORA
    ckvn = _rms(p[:, o:o + KV_LORA], g_kv_ref[...]).astype(_BF)
    o += KV_LORA
    cos = cos_ref[...]
    sin = sin_ref[...]
    kr = p[:, o:o + LANES] * cos + p[:, o + LANES:o + 2 * LANES] * sin
    q2 = _dot(cqn, w2_ref[...])
    k3 = _dot(ckvn, w3k_ref[...])
    hw = MLA_HEADS * LANES
    for h in range(MLA_HEADS):
        sl = slice(h * LANES, (h + 1) * LANES)
        q_ref[:, sl] = (q2[:, sl] * cos + q2[:, hw + h * LANES:hw + (h + 1) * LANES] * sin).astype(_BF)
        k_ref[:, sl] = (k3[:, sl] + kr).astype(_BF)
    v_ref[...] = _dot(ckvn, w3v_ref[...]).astype(_BF)


def _proj(x, cos, sin, g_attn, g_q, g_kv, w1, w2, w3k, w3v):
    np_rows = x.shape[0]
    row = lambda i: (i, 0)
    full = lambda i: (0, 0)
    w = lambda a: pl.BlockSpec(a.shape, full)
    return pl.pallas_call(
        _proj_kernel,
        grid=(np_rows // ROW_TILE,),
        in_specs=[pl.BlockSpec((ROW_TILE, D_MODEL), row), pl.BlockSpec((ROW_TILE, LANES), row),
                  pl.BlockSpec((ROW_TILE, LANES), row), w(g_attn), w(g_q), w(g_kv), w(w1), w(w2), w(w3k), w(w3v)],
        out_specs=[pl.BlockSpec((ROW_TILE, 3 * NA_WIDTH), row), pl.BlockSpec((ROW_TILE, MLA_HEADS * LANES), row),
                   pl.BlockSpec((ROW_TILE, MLA_HEADS * LANES), row), pl.BlockSpec((ROW_TILE, MLA_WIDTH), row)],
        out_shape=[jax.ShapeDtypeStruct((np_rows, 3 * NA_WIDTH), _BF),
                   jax.ShapeDtypeStruct((np_rows, MLA_HEADS * LANES), _BF),
                   jax.ShapeDtypeStruct((np_rows, MLA_HEADS * LANES), _BF),
                   jax.ShapeDtypeStruct((np_rows, MLA_WIDTH), _BF)],
        compiler_params=pltpu.CompilerParams(dimension_semantics=("parallel",), vmem_limit_bytes=VMEM_LIMIT),
        name="proj",
    )(x, cos, sin, g_attn, g_q, g_kv, w1, w2, w3k, w3v)


def _na_kernel(q_ref, kp_ref, kc_ref, kn_ref, vp_ref, vc_ref, vn_ref, km_ref, vm_ref, bias_ref, o_ref):
    ks = (kp_ref, kc_ref, kn_ref)
    vs = (vp_ref, vc_ref, vn_ref)
    for h in range(NA_HEADS):
        sl = slice(h * NA_HEAD_DIM, (h + 1) * NA_HEAD_DIM)
        q = q_ref[:, sl]
        s = [_dot_nt(q, ks[i][:, sl]) + bias_ref[h, :, i * NA_BLK:(i + 1) * NA_BLK] for i in range(3)]
        sm = _dot_nt(q, km_ref[:, sl])
        m = jnp.max(sm, axis=1, keepdims=True)
        for si in s:
            m = jnp.maximum(m, jnp.max(si, axis=1, keepdims=True))
        pm = jnp.exp(sm - m)
        l = jnp.sum(pm, axis=1, keepdims=True)
        acc = _dot(pm.astype(_BF), vm_ref[:, sl])
        for i in range(3):
            p = jnp.exp(s[i] - m)
            l = l + jnp.sum(p, axis=1, keepdims=True)
            acc = acc + _dot(p.astype(_BF), vs[i][:, sl])
        o_ref[:, sl] = (acc / l).astype(_BF)


def _na_attention(qkv, bias, batch, seq):
    nblk = seq // NA_BLK
    meta_blk0 = batch * seq // N_META

    def kv_spec(off, col):
        return pl.BlockSpec((NA_BLK, NA_WIDTH),
                            lambda b, j: (b * nblk + jnp.clip(j + off, 0, nblk - 1), col))

    meta = lambda col: pl.BlockSpec((N_META, NA_WIDTH), lambda b, j: (meta_blk0 + b, col))
    cls = lambda b, j: (jnp.where(j == 0, 0, jnp.where(j == nblk - 1, 2, 1)), 0, 0, 0)
    return pl.pallas_call(
        _na_kernel,
        grid=(batch, nblk),
        in_specs=[pl.BlockSpec((NA_BLK, NA_WIDTH), lambda b, j: (b * nblk + j, 0)),
                  kv_spec(-1, 1), kv_spec(0, 1), kv_spec(1, 1), kv_spec(-1, 2), kv_spec(0, 2), kv_spec(1, 2),
                  meta(1), meta(2),
                  pl.BlockSpec((None, NA_HEADS, NA_BLK, 3 * NA_BLK), cls)],
        out_specs=pl.BlockSpec((NA_BLK, NA_WIDTH), lambda b, j: (b * nblk + j, 0)),
        out_shape=jax.ShapeDtypeStruct((batch * seq, NA_WIDTH), _BF),
        compiler_params=pltpu.CompilerParams(dimension_semantics=("parallel", "arbitrary"),
                                             vmem_limit_bytes=VMEM_LIMIT),
        name="na_attn",
    )(qkv, qkv, qkv, qkv, qkv, qkv, qkv, qkv, qkv, bias)


def _na_meta_kernel(q_ref, k_ref, v_ref, o_ref):
    for h in range(NA_HEADS):
        sl = slice(h * NA_HEAD_DIM, (h + 1) * NA_HEAD_DIM)
        s = _dot_nt(q_ref[:, sl], k_ref[:, sl])
        p = jnp.exp(s - jnp.max(s, axis=1, keepdims=True))
        l = jnp.sum(p, axis=1, keepdims=True)
        o_ref[:, sl] = (_dot(p.astype(_BF), v_ref[:, sl]) / l).astype(_BF)


def _na_meta_attention(qkv, batch, seq):
    meta_blk0 = batch * seq // N_META
    spec = lambda col: pl.BlockSpec((N_META, NA_WIDTH), lambda b: (meta_blk0 + b, col))
    return pl.pallas_call(
        _na_meta_kernel,
        grid=(batch,),
        in_specs=[spec(0), spec(1), spec(2)],
        out_specs=pl.BlockSpec((N_META, NA_WIDTH), lambda b: (b, 0)),
        out_shape=jax.ShapeDtypeStruct((batch * N_META, NA_WIDTH), _BF),
        compiler_params=pltpu.CompilerParams(dimension_semantics=("parallel",)),
        name="na_meta_attn",
    )(qkv, qkv, qkv)


V_EXT = V_HEAD + 16
MLA_CHUNK = 256
MLA_LOOKAHEAD = 5
MLA_TQ = 256
MLA_TK = 2048


def _mla_kernel(qt_ref, k_ref, vt_ref, km_ref, vmt_ref, ot_ref, m_sc, acc_sc, *, chunk):
    j = pl.program_id(2)
    tq = qt_ref.shape[2]
    nchunks = k_ref.shape[1] // chunk

    @pl.when(j == 0)
    def _():
        def init_head(h, carry):
            s = _dot(km_ref[h], qt_ref[h])
            m = jnp.max(s, axis=0, keepdims=True)
            p = jnp.exp2(s - m).astype(_BF)
            acc_sc[h] = _dot(vmt_ref[h], p)
            m_sc[h] = jnp.broadcast_to(m, (8, tq))
            return carry
        lax.fori_loop(0, MLA_HEADS, init_head, 0)

    steps = [(h, c) for h in range(MLA_HEADS) for c in range(nchunks)]
    scores = lambda h, c: _dot(k_ref[h, c * chunk:(c + 1) * chunk, :], qt_ref[h])
    pending = [scores(*st) for st in steps[:MLA_LOOKAHEAD]]
    for n, (h, c) in enumerate(steps):
        if n + MLA_LOOKAHEAD < len(steps):
            pending.append(scores(*steps[n + MLA_LOOKAHEAD]))
        s = pending.pop(0)
        if c == 0:
            m = m_sc[h][:1]
            acc = acc_sc[h]
        m_new = jnp.maximum(m, jnp.max(s, axis=0, keepdims=True))
        p = jnp.exp2(s - m_new).astype(_BF)
        acc = acc * jnp.exp2(m - m_new) + _dot(vt_ref[h, :, c * chunk:(c + 1) * chunk], p)
        m = m_new
        if c == nchunks - 1:
            m_sc[h] = jnp.broadcast_to(m, (8, tq))
            acc_sc[h] = acc

    @pl.when(j == pl.num_programs(2) - 1)
    def _():
        for h in range(MLA_HEADS):
            acc = acc_sc[h]
            ot_ref[h] = (acc[:V_HEAD] / acc[V_HEAD:V_HEAD + 1]).astype(_BF)


def _mla_attention(qt, k, vt, km, vmt, batch, seq, q_rows, q_blk0, q_stride, tq, tk):
    nq = q_rows // tq
    nk = seq // tk
    return pl.pallas_call(
        functools.partial(_mla_kernel, chunk=min(MLA_CHUNK, tk)),
        grid=(batch, nq, nk),
        in_specs=[pl.BlockSpec((MLA_HEADS, LANES, tq), lambda b, i, j: (0, 0, q_blk0 + b * q_stride + i)),
                  pl.BlockSpec((MLA_HEADS, tk, LANES), lambda b, i, j: (0, b * nk + j, 0)),
                  pl.BlockSpec((MLA_HEADS, V_EXT, tk), lambda b, i, j: (0, 0, b * nk + j)),
                  pl.BlockSpec((None, MLA_HEADS, N_META, LANES), lambda b, i, j: (b, 0, 0, 0)),
                  pl.BlockSpec((None, MLA_HEADS, V_EXT, N_META), lambda b, i, j: (b, 0, 0, 0))],
        out_specs=pl.BlockSpec((MLA_HEADS, V_HEAD, tq), lambda b, i, j: (0, 0, b * nq + i)),
        out_shape=jax.ShapeDtypeStruct((MLA_HEADS, V_HEAD, batch * q_rows), _BF),
        scratch_shapes=[pltpu.VMEM((MLA_HEADS, 8, tq), _F32), pltpu.VMEM((MLA_HEADS, V_EXT, tq), _F32)],
        compiler_params=pltpu.CompilerParams(dimension_semantics=("parallel", "parallel", "arbitrary"),
                                             vmem_limit_bytes=VMEM_LIMIT),
        name="mla_attn",
    )(qt, k, vt, km, vmt)


def _outproj_kernel(x_ref, ong_ref, onm_ref, omg_ref, omm_ref, wo_na_ref, wo_mla_ref, g_ref, wr_hi_ref, wr_lo_ref,
                    h_ref, hn_ref, aff_ref, *, n_grid_tiles, n_valid):
    i = pl.program_id(0)
    is_meta = i == n_grid_tiles
    o_na = jnp.where(is_meta, onm_ref[...], ong_ref[...])
    o_mla = jnp.where(is_meta, omm_ref[...], omg_ref[...])
    h = x_ref[...] + _dot(o_na, wo_na_ref[...]) + _dot(o_mla, wo_mla_ref[...])
    h_ref[...] = h
    hn = _rms(h, g_ref[...])
    hn_hi = hn.astype(_BF)
    hn_ref[...] = hn_hi
    hn_lo = (hn - hn_hi.astype(_F32)).astype(_BF)
    wr_hi = wr_hi_ref[...]
    logits = _dot_nt(wr_hi, hn_hi) + _dot_nt(wr_hi, hn_lo) + _dot_nt(wr_lo_ref[...], hn_hi)
    e = jnp.exp(logits - jnp.max(logits, axis=0, keepdims=True))
    aff = e / jnp.sum(e, axis=0, keepdims=True)
    row = i * ROW_TILE + lax.broadcasted_iota(jnp.int32, aff.shape, 1)
    aff_ref[...] = jnp.where(row < n_valid, aff, -1.0)


def _outproj(x, o_na_g, o_na_m, o_mla_g, o_mla_m, wo_na, wo_mla, g_ffn, wr_hi, wr_lo, n_valid):
    np_rows = x.shape[0]
    n_tiles = np_rows // ROW_TILE
    n_grid_tiles = n_tiles - 1
    row = lambda i: (i, 0)
    grid_row = lambda i: (jnp.minimum(i, n_grid_tiles - 1), 0)
    full = lambda i: (0, 0)
    w = lambda a: pl.BlockSpec(a.shape, full)
    return pl.pallas_call(
        functools.partial(_outproj_kernel, n_grid_tiles=n_grid_tiles, n_valid=n_valid),
        grid=(n_tiles,),
        in_specs=[pl.BlockSpec((ROW_TILE, D_MODEL), row),
                  pl.BlockSpec((ROW_TILE, NA_WIDTH), grid_row), w(o_na_m),
                  pl.BlockSpec((ROW_TILE, MLA_WIDTH), grid_row), w(o_mla_m),
                  w(wo_na), w(wo_mla), w(g_ffn), w(wr_hi), w(wr_lo)],
        out_specs=[pl.BlockSpec((ROW_TILE, D_MODEL), row), pl.BlockSpec((ROW_TILE, D_MODEL), row),
                   pl.BlockSpec((None, N_EXPERTS, ROW_TILE), lambda i: (i, 0, 0))],
        out_shape=[jax.ShapeDtypeStruct((np_rows, D_MODEL), _F32), jax.ShapeDtypeStruct((np_rows, D_MODEL), _BF),
                   jax.ShapeDtypeStruct((n_tiles, N_EXPERTS, ROW_TILE), _F32)],
        compiler_params=pltpu.CompilerParams(dimension_semantics=("parallel",), vmem_limit_bytes=VMEM_LIMIT),
        name="outproj_router",
    )(x, o_na_g, o_na_m, o_mla_g, o_mla_m, wo_na, wo_mla, g_ffn, wr_hi, wr_lo)


SLOT_W = 72
WIN = 80
MAX_PASSES = 4


def _route_kernel(aff_ref, slot_ref, lo_ref, *, cap):
    n_tiles = aff_ref.shape[0]
    bits = lambda: pltpu.bitcast(aff_ref[...], jnp.int32)

    def bit_step(i, cur):
        cand = cur | jnp.left_shift(jnp.int32(1), 30 - i)
        cnt = jnp.sum((bits() >= cand).astype(jnp.int32), axis=(0, 2), keepdims=True)
        return jnp.where(cnt >= cap, cand, cur)
    thr = lax.fori_loop(0, 31, bit_step, jnp.zeros((1, N_EXPERTS, 1), jnp.int32))
    n_gt = jnp.sum((bits() > thr).astype(jnp.int32), axis=(0, 2), keepdims=True)
    need = (cap - n_gt)[0].astype(_F32)
    thr = thr[0]
    r = lax.broadcasted_iota(jnp.int32, (ROW_TILE, ROW_TILE), 0)
    c = lax.broadcasted_iota(jnp.int32, (ROW_TILE, ROW_TILE), 1)
    tri = jnp.where(r <= c, 1.0, 0.0).astype(_BF)
    as_bf = lambda m: jnp.where(m, 1.0, 0.0).astype(_BF)

    def tile_step(t, carry):
        tie_before, kept_before = carry
        b = pltpu.bitcast(aff_ref[t], jnp.int32)
        eq = b == thr
        tie_rank = _dot(as_bf(eq), tri) + tie_before
        keep = (b > thr) | (eq & (tie_rank <= need))
        kept_incl = _dot(as_bf(keep), tri) + kept_before
        slot_ref[t] = jnp.where(keep, kept_incl - 1.0, -1.0).astype(jnp.int32)
        lo_ref[t] = jnp.broadcast_to(kept_before, (N_EXPERTS, LANES)).astype(jnp.int32)
        return tie_rank[:, ROW_TILE - 1:], kept_incl[:, ROW_TILE - 1:]
    zero = jnp.zeros((N_EXPERTS, 1), _F32)
    lax.fori_loop(0, n_tiles, tile_step, (zero, zero))


def _route(aff, cap):
    n_tiles = aff.shape[0]
    full = lambda shape: pl.BlockSpec(shape, lambda i: (0, 0, 0))
    return pl.pallas_call(
        functools.partial(_route_kernel, cap=cap),
        grid=(1,),
        in_specs=[full(aff.shape)],
        out_specs=[full(aff.shape), full((n_tiles, N_EXPERTS, LANES))],
        out_shape=[jax.ShapeDtypeStruct(aff.shape, jnp.int32),
                   jax.ShapeDtypeStruct((n_tiles, N_EXPERTS, LANES), jnp.int32)],
        compiler_params=pltpu.CompilerParams(dimension_semantics=("arbitrary",), vmem_limit_bytes=VMEM_LIMIT),
        name="route",
    )(aff)


def _tile_max(lo_ref, i, start_of):
    m = lo_ref[(i + 1) * N_EXPERTS] - start_of(0)
    for e in range(1, N_EXPERTS):
        m = jnp.maximum(m, lo_ref[(i + 1) * N_EXPERTS + e] - start_of(e))
    return m


def _dispatch_kernel(lo_ref, hn_ref, slot_ref, xe_ref, stage, stage_extra, carry, sem, sem_extra, *, zero_from,
                     zero_rows):
    i = pl.program_id(0)
    s = i % 2
    base = lambda e: (lo_ref[i * N_EXPERTS + e] // 8) * 8
    next_base = lambda e: (lo_ref[(i + 1) * N_EXPERTS + e] // 8) * 8

    def compact(k, dst):
        w = lax.broadcasted_iota(jnp.int32, (SLOT_W, ROW_TILE), 0) + k * SLOT_W
        onehot = jnp.concatenate(
            [jnp.where(slot_ref[e:e + 1, :] - base(e) == w, 1.0, 0.0).astype(_BF) for e in range(N_EXPERTS)], axis=0)
        dst[...] = _dot(onehot, hn_ref[...])

    def copy(e, src, k, sm):
        dst_row = pl.multiple_of(base(e) + k * SLOT_W, 8)
        return pltpu.make_async_copy(src.at[pl.ds(e * SLOT_W, SLOT_W)], xe_ref.at[e, pl.ds(dst_row, SLOT_W)], sm)

    def save_carry(k, src):
        for e in range(N_EXPERTS):
            g = next_base(e) - base(e) - k * SLOT_W

            @pl.when((g >= 0) & (g < SLOT_W))
            def _():
                carry[e] = src[pl.ds(pl.multiple_of(e * SLOT_W + g, 8), 8), :]

    @pl.when(i == 0)
    def _():
        carry[...] = jnp.zeros(carry.shape, _F32)
        stage_extra[...] = jnp.zeros(stage_extra.shape, _F32)
        zero_copies = [pltpu.make_async_copy(stage_extra.at[pl.ds(0, zero_rows)],
                                             xe_ref.at[e, pl.ds(zero_from, zero_rows)], sem_extra)
                       for e in range(N_EXPERTS)]
        for cp in zero_copies:
            cp.start()
        for cp in zero_copies:
            cp.wait()

    compact(0, stage.at[s])
    for e in range(N_EXPERTS):
        stage[s, e * SLOT_W:e * SLOT_W + 8, :] += carry[e]
    save_carry(0, stage.at[s])

    @pl.when(i > 0)
    def _():
        for e in range(N_EXPERTS):
            copy(e, stage.at[1 - s], 0, sem.at[1 - s]).wait()

    for e in range(N_EXPERTS):
        copy(e, stage.at[s], 0, sem.at[s]).start()

    reach = _tile_max(lo_ref, i, base)
    for k in range(1, MAX_PASSES):
        @pl.when(reach >= k * SLOT_W)
        def _():
            compact(k, stage_extra)
            save_carry(k, stage_extra)
            for wait in (False, True):
                for e in range(N_EXPERTS):
                    @pl.when(lo_ref[(i + 1) * N_EXPERTS + e] - base(e) >= k * SLOT_W)
                    def _():
                        cp = copy(e, stage_extra, k, sem_extra)
                        cp.wait() if wait else cp.start()

    @pl.when(i == pl.num_programs(0) - 1)
    def _():
        for e in range(N_EXPERTS):
            copy(e, stage.at[s], 0, sem.at[s]).wait()


def _dispatch(lo, hn, slot, cap, cap_pad):
    n_tiles = slot.shape[0]
    zero_from = cap // 8 * 8
    zero_rows = cap_pad + SLOT_W - zero_from
    assert zero_rows <= N_EXPERTS * SLOT_W and MAX_PASSES * SLOT_W >= ROW_TILE + 8 and SLOT_W % 8 == 0
    return pl.pallas_call(
        functools.partial(_dispatch_kernel, zero_from=zero_from, zero_rows=zero_rows),
        grid_spec=pltpu.PrefetchScalarGridSpec(
            num_scalar_prefetch=1,
            grid=(n_tiles,),
            in_specs=[pl.BlockSpec((ROW_TILE, D_MODEL), lambda i, lo: (i, 0)),
                      pl.BlockSpec((None, N_EXPERTS, ROW_TILE), lambda i, lo: (i, 0, 0))],
            out_specs=pl.BlockSpec(memory_space=pl.ANY),
            scratch_shapes=[pltpu.VMEM((2, N_EXPERTS * SLOT_W, D_MODEL), _F32),
                            pltpu.VMEM((N_EXPERTS * SLOT_W, D_MODEL), _F32),
                            pltpu.VMEM((N_EXPERTS, 8, D_MODEL), _F32),
                            pltpu.SemaphoreType.DMA((2,)), pltpu.SemaphoreType.DMA(())]),
        out_shape=jax.ShapeDtypeStruct((N_EXPERTS, cap_pad + SLOT_W, D_MODEL), _F32),
        compiler_params=pltpu.CompilerParams(dimension_semantics=("arbitrary",), vmem_limit_bytes=VMEM_LIMIT),
        name="dispatch",
    )(lo, hn, slot)


def _expert_kernel(x_ref, wg_ref, wu_ref, wd_ref, y_ref):
    x = x_ref[...].astype(_BF)
    g = _dot(x, wg_ref[...])
    u = _dot(x, wu_ref[...])
    hid = (g * jax.nn.sigmoid(g) * u).astype(_BF)
    y_ref[...] = _dot(hid, wd_ref[...]).astype(_BF)


def _experts(xe, wg, wu, wd, cap_pad, tile):
    n_e = xe.shape[0]
    wspec = lambda a: pl.BlockSpec((None,) + a.shape[1:], lambda e, t: (e, 0, 0))
    return pl.pallas_call(
        _expert_kernel,
        grid=(n_e, cap_pad // tile),
        in_specs=[pl.BlockSpec((None, tile, D_MODEL), lambda e, t: (e, t, 0)), wspec(wg), wspec(wu), wspec(wd)],
        out_specs=pl.BlockSpec((None, tile, D_MODEL), lambda e, t: (e, t, 0)),
        out_shape=jax.ShapeDtypeStruct((n_e, cap_pad, D_MODEL), _BF),
        compiler_params=pltpu.CompilerParams(dimension_semantics=("parallel", "arbitrary"),
                                             vmem_limit_bytes=VMEM_LIMIT),
        name="experts",
    )(xe, wg, wu, wd)


def _combine_kernel(lo_ref, h_ref, slot_ref, aff_ref, g_ref, ye_ref, o_ref, win, f_acc, sem, *, cap_pad):
    i = pl.program_id(0)
    first = lambda e: (lo_ref[i * N_EXPERTS + e] // 16) * 16

    def contribution(k):
        starts = [pl.multiple_of(jnp.minimum(first(e) + k * WIN, cap_pad - WIN), 16) for e in range(N_EXPERTS)]
        copies = [pltpu.make_async_copy(ye_ref.at[e, pl.ds(starts[e], WIN)], win.at[pl.ds(e * WIN, WIN)], sem)
                  for e in range(N_EXPERTS)]
        for cp in copies:
            cp.start()
        w = lax.broadcasted_iota(jnp.int32, (WIN, ROW_TILE), 0)
        parts = []
        for e in range(N_EXPERTS):
            slot = slot_ref[e:e + 1, :]
            hit = (slot == w + starts[e]) & (slot >= first(e) + k * WIN)
            parts.append(jnp.where(hit, aff_ref[e:e + 1, :], 0.0).astype(_BF))
        weights = jnp.concatenate(parts, axis=0)
        for cp in copies:
            cp.wait()
        return lax.dot_general(weights, win[...], (((0,), (0,)), ((), ())), preferred_element_type=_F32)

    f_acc[...] = contribution(0)
    reach = _tile_max(lo_ref, i, first)
    for k in range(1, MAX_PASSES):
        @pl.when(reach > k * WIN)
        def _():
            f_acc[...] += contribution(k)
    o_ref[...] = _rms(h_ref[...] + f_acc[...], g_ref[...])


def _combine(lo, h, slot, aff, g_final, ye, n_rows):
    cap_pad = ye.shape[1]
    assert MAX_PASSES * WIN >= ROW_TILE + 16 and cap_pad % 16 == 0
    tile3 = pl.BlockSpec((None, N_EXPERTS, ROW_TILE), lambda i, lo: (i, 0, 0))
    return pl.pallas_call(
        functools.partial(_combine_kernel, cap_pad=cap_pad),
        grid_spec=pltpu.PrefetchScalarGridSpec(
            num_scalar_prefetch=1,
            grid=(n_rows // ROW_TILE,),
            in_specs=[pl.BlockSpec((ROW_TILE, D_MODEL), lambda i, lo: (i, 0)), tile3, tile3,
                      pl.BlockSpec(g_final.shape, lambda i, lo: (0, 0)), pl.BlockSpec(memory_space=pl.ANY)],
            out_specs=pl.BlockSpec((ROW_TILE, D_MODEL), lambda i, lo: (i, 0)),
            scratch_shapes=[pltpu.VMEM((N_EXPERTS * WIN, D_MODEL), _BF), pltpu.VMEM((ROW_TILE, D_MODEL), _F32),
                            pltpu.SemaphoreType.DMA(())]),
        out_shape=jax.ShapeDtypeStruct((n_rows, D_MODEL), _F32),
        compiler_params=pltpu.CompilerParams(dimension_semantics=("parallel",), vmem_limit_bytes=VMEM_LIMIT),
        name="combine_norm",
    )(lo, h, slot, aff, g_final, ye)


def _rotate_half_cols(w):
    half = QK_ROPE // 2
    return jnp.concatenate([-w[..., half:], w[..., :half]], axis=-1)


def _prepare_weights(w_in, g_attn, g_q, w_uq, g_kv, w_ukv, w_o, g_ffn, w_router, w_gate, w_up, w_down, na_rpb):
    na_scale = 1.0 / math.sqrt(NA_HEAD_DIM)
    c0, c1, c2 = 3 * NA_WIDTH, 3 * NA_WIDTH + Q_LORA, 3 * NA_WIDTH + Q_LORA + KV_LORA
    w_kr = w_in[:, c2:]
    zpad = lambda n: jnp.zeros((D_MODEL, n), _F32)
    slot = lambda w: jnp.concatenate([zpad(QK_NOPE), w, zpad(LANES - QK_NOPE - QK_ROPE)], axis=1)
    w1 = jnp.concatenate([w_in[:, :NA_WIDTH] * na_scale, w_in[:, NA_WIDTH:c2], slot(w_kr),
                          slot(_rotate_half_cols(w_kr))], axis=1).astype(_BF)
    q_scale = LOG2E / math.sqrt(QK_NOPE + QK_ROPE)
    wq = (w_uq * q_scale).reshape(Q_LORA, MLA_HEADS, QK_NOPE + QK_ROPE)
    pad = LANES - QK_NOPE - QK_ROPE
    plain = jnp.pad(wq, ((0, 0), (0, 0), (0, pad)))
    rot = jnp.pad(_rotate_half_cols(wq[..., QK_NOPE:]), ((0, 0), (0, 0), (QK_NOPE, pad)))
    w2 = jnp.concatenate([plain.reshape(Q_LORA, -1), rot.reshape(Q_LORA, -1)], axis=1).astype(_BF)
    wkv = w_ukv.reshape(KV_LORA, MLA_HEADS, QK_NOPE + V_HEAD)
    w3k = jnp.pad(wkv[..., :QK_NOPE], ((0, 0), (0, 0), (0, LANES - QK_NOPE))).reshape(KV_LORA, -1).astype(_BF)
    w3v = wkv[..., QK_NOPE:].reshape(KV_LORA, -1).astype(_BF)
    wr = w_router.T
    wr_hi = wr.astype(_BF)
    wr_lo = (wr - wr_hi.astype(_F32)).astype(_BF)
    return dict(
        w1=w1, w2=w2, w3k=w3k, w3v=w3v,
        g_attn=g_attn[None], g_q=g_q[None], g_kv=g_kv[None], g_ffn=g_ffn[None],
        wo_na=w_o[:NA_WIDTH].astype(_BF), wo_mla=w_o[NA_WIDTH:].astype(_BF),
        wr_hi=wr_hi, wr_lo=wr_lo,
        wg=w_gate.astype(_BF), wu=w_up.astype(_BF), wd=w_down.astype(_BF),
        na_bias=_na_bias_table(na_rpb),
    )


def _na_bias_table(rpb):
    ql = np.arange(NA_BLK)
    kl = np.arange(3 * NA_BLK)
    ri, qc = ql // GRID_W, ql % GRID_W
    kj, kc = kl // GRID_W, kl % GRID_W
    dr = kj[None, :] - ri[:, None] + (WIN_R - 1) - NA_ROWS
    qcs = np.clip(qc - WIN_C // 2, 0, GRID_W - WIN_C)
    col_ok = (kc[None, :] >= qcs[:, None]) & (kc[None, :] < qcs[:, None] + WIN_C)
    dc = np.clip(kc[None, :] - qc[:, None] + (WIN_C - 1), 0, 2 * WIN_C - 2)
    lo = np.stack([np.full(NA_BLK, NA_ROWS), ri, np.zeros(NA_BLK, np.int64)])
    row_ok = (kj[None, None, :] >= lo[:, :, None]) & (kj[None, None, :] < lo[:, :, None] + WIN_R)
    ok = row_ok & col_ok[None]
    vals = rpb[:, np.clip(dr, 0, 2 * WIN_R - 2), dc].astype(_F32)
    return jnp.where(jnp.asarray(ok)[:, None], vals[None], NEG)


def _rope_tables(batch, seq, np_rows):
    half = QK_ROPE // 2
    inv_freq = ROPE_THETA ** (-jnp.arange(half, dtype=_F32) * 2.0 / QK_ROPE)
    pos_grid = jnp.tile(jnp.arange(seq, dtype=_F32) + N_META, batch)
    pos_meta = jnp.tile(jnp.arange(N_META, dtype=_F32), batch)
    pos = jnp.concatenate([pos_grid, pos_meta, jnp.zeros(np_rows - batch * (seq + N_META), _F32)])
    ang = pos[:, None] * inv_freq[None, :]
    cos, sin = jnp.cos(ang), jnp.sin(ang)
    pad = LANES - QK_NOPE - QK_ROPE
    cos_t = jnp.concatenate([jnp.ones((np_rows, QK_NOPE), _F32), cos, cos, jnp.zeros((np_rows, pad), _F32)], axis=1)
    sin_t = jnp.concatenate([jnp.zeros((np_rows, QK_NOPE), _F32), sin, sin, jnp.zeros((np_rows, pad), _F32)], axis=1)
    return cos_t, sin_t


def _expert_tile(cap):
    if cap <= 1024:
        return -(-cap // 16) * 16
    return min(range(512, 1025, 16), key=lambda t: (-(-cap // t) * t - cap, -t))


def _run_group(x, meta_tokens, prm, g_final, tq, tk):
    batch, seq, _ = x.shape
    n_grid = batch * seq
    n_valid = n_grid + batch * N_META
    np_rows = n_grid + ROW_TILE
    assert seq % NA_BLK == 0 and seq // GRID_W >= WIN_R and batch * N_META <= ROW_TILE
    meta = jnp.broadcast_to(meta_tokens[None], (batch, N_META, D_MODEL)).reshape(batch * N_META, D_MODEL)
    xt = jnp.concatenate([x.reshape(n_grid, D_MODEL), meta,
                          jnp.zeros((np_rows - n_valid, D_MODEL), _F32)], axis=0)
    cos_t, sin_t = _rope_tables(batch, seq, np_rows)
    qkv, q, k, v = _proj(xt, cos_t, sin_t, prm["g_attn"], prm["g_q"], prm["g_kv"],
                         prm["w1"], prm["w2"], prm["w3k"], prm["w3v"])
    o_na_g = _na_attention(qkv, prm["na_bias"], batch, seq)
    o_na_m = _na_meta_attention(qkv, batch, seq)
    n_meta = batch * N_META
    qt = q.reshape(np_rows, MLA_HEADS, LANES).transpose(1, 2, 0)
    k3 = k.reshape(np_rows, MLA_HEADS, LANES).transpose(1, 0, 2)
    vt = v.reshape(np_rows, MLA_HEADS, V_HEAD).transpose(1, 2, 0)
    vt = jnp.concatenate([vt, jnp.ones((MLA_HEADS, 1, np_rows), _BF),
                          jnp.zeros((MLA_HEADS, V_EXT - V_HEAD - 1, np_rows), _BF)], axis=1)
    km = k3[:, n_grid:n_grid + n_meta].reshape(MLA_HEADS, batch, N_META, LANES).transpose(1, 0, 2, 3)
    vmt = vt[:, :, n_grid:n_grid + n_meta].reshape(MLA_HEADS, V_EXT, batch, N_META).transpose(2, 0, 1, 3)
    ot_g = _mla_attention(qt, k3, vt, km, vmt, batch, seq, seq, 0, seq // tq, tq, tk)
    o_mla_g = ot_g.transpose(2, 0, 1).reshape(n_grid, MLA_WIDTH)
    ot_m = _mla_attention(qt, k3, vt, km, vmt, batch, seq, ROW_TILE, n_grid // ROW_TILE, 0, ROW_TILE, tk)
    ot_m = ot_m.reshape(MLA_HEADS, V_HEAD, batch, ROW_TILE)
    o_mla_m = jnp.stack([ot_m[:, :, b, b * N_META:(b + 1) * N_META] for b in range(batch)])
    o_mla_m = o_mla_m.transpose(0, 3, 1, 2).reshape(n_meta, MLA_WIDTH)
    pad_meta = lambda a: jnp.pad(a, ((0, ROW_TILE - a.shape[0]), (0, 0)))
    h, hn, aff = _outproj(xt, o_na_g, pad_meta(o_na_m), o_mla_g, pad_meta(o_mla_m), prm["wo_na"], prm["wo_mla"],
                          prm["g_ffn"], prm["wr_hi"], prm["wr_lo"], n_valid)
    cap = EC_CAPACITY * n_valid // N_EXPERTS
    tile = _expert_tile(cap)
    cap_pad = -(-cap // tile) * tile
    slot, lo3 = _route(aff, cap)
    lo = jnp.concatenate([lo3[:, :, 0], jnp.full((1, N_EXPERTS), cap, jnp.int32)]).reshape(-1)
    xe = _dispatch(lo, hn, slot, cap, cap_pad)
    ye = _experts(xe, prm["wg"], prm["wu"], prm["wd"], cap_pad, tile)
    return _combine(lo, h, slot, aff, g_final[None], ye, n_grid).reshape(batch, seq, D_MODEL)


def kernel(x_prompt, x_sample, meta_tokens, g_attn, w_in, na_rpb, g_q, w_uq, g_kv, w_ukv, w_o, g_ffn, w_router,
           w_gate, w_up, w_down, g_final):
    prm = _prepare_weights(w_in[0], g_attn[0], g_q[0], w_uq[0], g_kv[0], w_ukv[0], w_o[0], g_ffn[0], w_router[0],
                           w_gate[0], w_up[0], w_down[0], na_rpb[0])
    y_prompt = _run_group(x_prompt, meta_tokens, prm, g_final, MLA_TQ, MLA_TK)
    y_sample = _run_group(x_sample, meta_tokens, prm, g_final, MLA_TQ, MLA_TK)
    return (y_prompt, y_sample)
```

```python
import functools
import math

import jax
import jax.numpy as jnp
import numpy as np
from jax import lax
from jax.experimental import pallas as pl
from jax.experimental.pallas import tpu as pltpu

D_MODEL = 1024
GRID_W = 64
N_META = 16
WIN_R = 8
WIN_C = 16
NA_HEADS = 8
NA_HEAD_DIM = 64
MLA_HEADS = 8
QK_NOPE = 64
QK_ROPE = 32
V_HEAD = 64
Q_LORA = 768
KV_LORA = 256
ROPE_THETA = 10000.0
N_EXPERTS = 16
EC_CAPACITY = 2
D_EXPERT = 2048
NORM_EPS = 1e-6
NA_WIDTH = NA_HEADS * NA_HEAD_DIM
MLA_WIDTH = MLA_HEADS * V_HEAD

LANES = 128
ROW_TILE = 256
NA_ROWS = 4
NA_BLK = NA_ROWS * GRID_W
NEG = -1e30
LOG2E = 1.4426950408889634
VMEM_LIMIT = 56 * 1024 * 1024

_BF = jnp.bfloat16
_F32 = jnp.float32


def _dot(a, b):
    return jnp.dot(a, b, preferred_element_type=_F32)


def _dot_nt(a, b):
    return lax.dot_general(a, b, (((1,), (1,)), ((), ())), preferred_element_type=_F32)


def _rms(x, g):
    return x * lax.rsqrt(jnp.mean(x * x, axis=-1, keepdims=True) + NORM_EPS) * g


def _proj_kernel(x_ref, cos_ref, sin_ref, cost_ref, sint_ref, g_attn_ref, g_q_ref, g_kv_ref, w1_ref, w2t_ref, w3k_ref,
                 w3vt_ref, qkv_ref, qt_ref, k_ref, vt_ref):
    a = _rms(x_ref[...], g_attn_ref[...]).astype(_BF)
    p = _dot(a, w1_ref[...])
    qkv_ref[...] = p[:, :3 * NA_WIDTH].astype(_BF)
    o = 3 * NA_WIDTH
    cqn = _rms(p[:, o:o + Q_LORA], g_q_ref[...]).astype(_BF)
    o += Q_LORA
    ckvn = _rms(p[:, o:o + KV_LORA], g_kv_ref[...]).astype(_BF)
    o += KV_LORA
    kr = p[:, o:o + LANES] * cos_ref[...] + p[:, o + LANES:o + 2 * LANES] * sin_ref[...]
    k3 = _dot(ckvn, w3k_ref[...])
    q2t = _dot_nt(w2t_ref[...], cqn)
    vt = _dot_nt(w3vt_ref[...], ckvn)
    cos_t = cost_ref[...]
    sin_t = sint_ref[...]
    hw = MLA_HEADS * LANES
    ones_row = jnp.where(lax.broadcasted_iota(jnp.int32, (V_EXT - V_HEAD, x_ref.shape[0]), 0) == 0, 1.0, 0.0)
    for h in range(MLA_HEADS):
        sl = slice(h * LANES, (h + 1) * LANES)
        qt_ref[h] = (q2t[sl] * cos_t + q2t[hw + h * LANES:hw + (h + 1) * LANES] * sin_t).astype(_BF)
        k_ref[h] = (k3[:, sl] + kr).astype(_BF)
        vt_ref[h, :V_HEAD, :] = vt[h * V_HEAD:(h + 1) * V_HEAD].astype(_BF)
        vt_ref[h, V_HEAD:, :] = ones_row.astype(_BF)


def _proj(x, cos, sin, cos_t, sin_t, g_attn, g_q, g_kv, w1, w2t, w3k, w3vt):
    np_rows = x.shape[0]
    row = lambda i: (i, 0)
    col = lambda i: (0, i)
    full = lambda i: (0, 0)
    w = lambda a: pl.BlockSpec(a.shape, full)
    return pl.pallas_call(
        _proj_kernel,
        grid=(np_rows // ROW_TILE,),
        in_specs=[pl.BlockSpec((ROW_TILE, D_MODEL), row), pl.BlockSpec((ROW_TILE, LANES), row),
                  pl.BlockSpec((ROW_TILE, LANES), row), pl.BlockSpec((LANES, ROW_TILE), col),
                  pl.BlockSpec((LANES, ROW_TILE), col), w(g_attn), w(g_q), w(g_kv), w(w1), w(w2t), w(w3k), w(w3vt)],
        out_specs=[pl.BlockSpec((ROW_TILE, 3 * NA_WIDTH), row),
                   pl.BlockSpec((MLA_HEADS, LANES, ROW_TILE), lambda i: (0, 0, i)),
                   pl.BlockSpec((MLA_HEADS, ROW_TILE, LANES), lambda i: (0, i, 0)),
                   pl.BlockSpec((MLA_HEADS, V_EXT, ROW_TILE), lambda i: (0, 0, i))],
        out_shape=[jax.ShapeDtypeStruct((np_rows, 3 * NA_WIDTH), _BF),
                   jax.ShapeDtypeStruct((MLA_HEADS, LANES, np_rows), _BF),
                   jax.ShapeDtypeStruct((MLA_HEADS, np_rows, LANES), _BF),
                   jax.ShapeDtypeStruct((MLA_HEADS, V_EXT, np_rows), _BF)],
        compiler_params=pltpu.CompilerParams(dimension_semantics=("parallel",), vmem_limit_bytes=VMEM_LIMIT),
        name="proj",
    )(x, cos, sin, cos_t, sin_t, g_attn, g_q, g_kv, w1, w2t, w3k, w3vt)


def _na_kernel(q_ref, kp_ref, kc_ref, kn_ref, vp_ref, vc_ref, vn_ref, km_ref, vm_ref, bias_ref, o_ref):
    ks = (kp_ref, kc_ref, kn_ref)
    vs = (vp_ref, vc_ref, vn_ref)
    for h in range(NA_HEADS):
        sl = slice(h * NA_HEAD_DIM, (h + 1) * NA_HEAD_DIM)
        q = q_ref[:, sl]
        s = [_dot_nt(q, ks[i][:, sl]) + bias_ref[h, :, i * NA_BLK:(i + 1) * NA_BLK] for i in range(3)]
        sm = _dot_nt(q, km_ref[:, sl])
        m = jnp.max(sm, axis=1, keepdims=True)
        for si in s:
            m = jnp.maximum(m, jnp.max(si, axis=1, keepdims=True))
        pm = jnp.exp(sm - m)
        l = jnp.sum(pm, axis=1, keepdims=True)
        acc = _dot(pm.astype(_BF), vm_ref[:, sl])
        for i in range(3):
            p = jnp.exp(s[i] - m)
            l = l + jnp.sum(p, axis=1, keepdims=True)
            acc = acc + _dot(p.astype(_BF), vs[i][:, sl])
        o_ref[:, sl] = (acc / l).astype(_BF)


def _na_attention(qkv, bias, batch, seq):
    nblk = seq // NA_BLK
    meta_blk0 = batch * seq // N_META

    def kv_spec(off, col):
        return pl.BlockSpec((NA_BLK, NA_WIDTH),
                            lambda b, j: (b * nblk + jnp.clip(j + off, 0, nblk - 1), col))

    meta = lambda col: pl.BlockSpec((N_META, NA_WIDTH), lambda b, j: (meta_blk0 + b, col))
    cls = lambda b, j: (jnp.where(j == 0, 0, jnp.where(j == nblk - 1, 2, 1)), 0, 0, 0)
    return pl.pallas_call(
        _na_kernel,
        grid=(batch, nblk),
        in_specs=[pl.BlockSpec((NA_BLK, NA_WIDTH), lambda b, j: (b * nblk + j, 0)),
                  kv_spec(-1, 1), kv_spec(0, 1), kv_spec(1, 1), kv_spec(-1, 2), kv_spec(0, 2), kv_spec(1, 2),
                  meta(1), meta(2),
                  pl.BlockSpec((None, NA_HEADS, NA_BLK, 3 * NA_BLK), cls)],
        out_specs=pl.BlockSpec((NA_BLK, NA_WIDTH), lambda b, j: (b * nblk + j, 0)),
        out_shape=jax.ShapeDtypeStruct((batch * seq, NA_WIDTH), _BF),
        compiler_params=pltpu.CompilerParams(dimension_semantics=("parallel", "arbitrary"),
                                             vmem_limit_bytes=VMEM_LIMIT),
        name="na_attn",
    )(qkv, qkv, qkv, qkv, qkv, qkv, qkv, qkv, qkv, bias)


def _na_meta_kernel(q_ref, k_ref, v_ref, o_ref):
    for h in range(NA_HEADS):
        sl = slice(h * NA_HEAD_DIM, (h + 1) * NA_HEAD_DIM)
        s = _dot_nt(q_ref[:, sl], k_ref[:, sl])
        p = jnp.exp(s - jnp.max(s, axis=1, keepdims=True))
        l = jnp.sum(p, axis=1, keepdims=True)
        o_ref[:, sl] = (_dot(p.astype(_BF), v_ref[:, sl]) / l).astype(_BF)


def _na_meta_attention(qkv, batch, seq):
    meta_blk0 = batch * seq // N_META
    spec = lambda col: pl.BlockSpec((N_META, NA_WIDTH), lambda b: (meta_blk0 + b, col))
    return pl.pallas_call(
        _na_meta_kernel,
        grid=(batch,),
        in_specs=[spec(0), spec(1), spec(2)],
        out_specs=pl.BlockSpec((N_META, NA_WIDTH), lambda b: (b, 0)),
        out_shape=jax.ShapeDtypeStruct((batch * N_META, NA_WIDTH), _BF),
        compiler_params=pltpu.CompilerParams(dimension_semantics=("parallel",)),
        name="na_meta_attn",
    )(qkv, qkv, qkv)


V_EXT = V_HEAD + 16
MLA_CHUNK = 256
MLA_LOOKAHEAD = 5
MLA_TQ = 256
MLA_TK = 2048


def _mla_kernel(qt_ref, k_ref, vt_ref, km_ref, vmt_ref, ot_ref, m_sc, acc_sc, *, chunk):
    j = pl.program_id(2)
    tq = qt_ref.shape[2]
    nchunks = k_ref.shape[1] // chunk

    @pl.when(j == 0)
    def _():
        def init_head(h, carry):
            s = _dot(km_ref[h], qt_ref[h])
            m = jnp.max(s, axis=0, keepdims=True)
            p = jnp.exp2(s - m).astype(_BF)
            acc_sc[h] = _dot(vmt_ref[h], p)
            m_sc[h] = jnp.broadcast_to(m, (8, tq))
            return carry
        lax.fori_loop(0, MLA_HEADS, init_head, 0)

    steps = [(h, c) for h in range(MLA_HEADS) for c in range(nchunks)]
    scores = lambda h, c: _dot(k_ref[h, c * chunk:(c + 1) * chunk, :], qt_ref[h])
    pending = [scores(*st) for st in steps[:MLA_LOOKAHEAD]]
    for n, (h, c) in enumerate(steps):
        if n + MLA_LOOKAHEAD < len(steps):
            pending.append(scores(*steps[n + MLA_LOOKAHEAD]))
        s = pending.pop(0)
        if c == 0:
            m = m_sc[h][:1]
            acc = acc_sc[h]
        m_new = jnp.maximum(m, jnp.max(s, axis=0, keepdims=True))
        p = jnp.exp2(s - m_new).astype(_BF)
        acc = acc * jnp.exp2(m - m_new) + _dot(vt_ref[h, :, c * chunk:(c + 1) * chunk], p)
        m = m_new
        if c == nchunks - 1:
            m_sc[h] = jnp.broadcast_to(m, (8, tq))
            acc_sc[h] = acc

    @pl.when(j == pl.num_programs(2) - 1)
    def _():
        for h in range(MLA_HEADS):
            acc = acc_sc[h]
            ot_ref[h] = (acc[:V_HEAD] / acc[V_HEAD:V_HEAD + 1]).astype(_BF)


def _mla_attention(qt, k, vt, km, vmt, batch, seq, q_rows, q_blk0, q_stride, tq, tk):
    nq = q_rows // tq
    nk = seq // tk
    return pl.pallas_call(
        functools.partial(_mla_kernel, chunk=min(MLA_CHUNK, tk)),
        grid=(batch, nq, nk),
        in_specs=[pl.BlockSpec((MLA_HEADS, LANES, tq), lambda b, i, j: (0, 0, q_blk0 + b * q_stride + i)),
                  pl.BlockSpec((MLA_HEADS, tk, LANES), lambda b, i, j: (0, b * nk + j, 0)),
                  pl.BlockSpec((MLA_HEADS, V_EXT, tk), lambda b, i, j: (0, 0, b * nk + j)),
                  pl.BlockSpec((None, MLA_HEADS, N_META, LANES), lambda b, i, j: (b, 0, 0, 0)),
                  pl.BlockSpec((None, MLA_HEADS, V_EXT, N_META), lambda b, i, j: (b, 0, 0, 0))],
        out_specs=pl.BlockSpec((MLA_HEADS, V_HEAD, tq), lambda b, i, j: (0, 0, b * nq + i)),
        out_shape=jax.ShapeDtypeStruct((MLA_HEADS, V_HEAD, batch * q_rows), _BF),
        scratch_shapes=[pltpu.VMEM((MLA_HEADS, 8, tq), _F32), pltpu.VMEM((MLA_HEADS, V_EXT, tq), _F32)],
        compiler_params=pltpu.CompilerParams(dimension_semantics=("parallel", "parallel", "arbitrary"),
                                             vmem_limit_bytes=VMEM_LIMIT),
        name="mla_attn",
    )(qt, k, vt, km, vmt)


def _outproj_kernel(x_ref, ong_ref, onm_ref, omg_ref, omm_ref, wo_na_ref, wo_mla_ref, g_ref, wr_hi_ref, wr_lo_ref,
                    h_ref, hn_ref, aff_ref, *, n_grid_tiles, n_valid):
    i = pl.program_id(0)
    is_meta = i == n_grid_tiles
    o_na = jnp.where(is_meta, onm_ref[...], ong_ref[...])
    o_mla_t = jnp.where(is_meta, omm_ref[...], omg_ref[...])
    h = x_ref[...] + _dot(o_na, wo_na_ref[...]) + lax.dot_general(
        o_mla_t, wo_mla_ref[...], (((0,), (0,)), ((), ())), preferred_element_type=_F32)
    h_ref[...] = h
    hn = _rms(h, g_ref[...])
    hn_hi = hn.astype(_BF)
    hn_ref[...] = hn_hi
    hn_lo = (hn - hn_hi.astype(_F32)).astype(_BF)
    wr_hi = wr_hi_ref[...]
    logits = _dot_nt(wr_hi, hn_hi) + _dot_nt(wr_hi, hn_lo) + _dot_nt(wr_lo_ref[...], hn_hi)
    e = jnp.exp(logits - jnp.max(logits, axis=0, keepdims=True))
    aff = e / jnp.sum(e, axis=0, keepdims=True)
    row = i * ROW_TILE + lax.broadcasted_iota(jnp.int32, aff.shape, 1)
    aff_ref[...] = jnp.where(row < n_valid, aff, -1.0)


def _outproj(x, o_na_g, o_na_m, o_mla_g, o_mla_m, wo_na, wo_mla, g_ffn, wr_hi, wr_lo, n_valid):
    np_rows = x.shape[0]
    n_tiles = np_rows // ROW_TILE
    n_grid_tiles = n_tiles - 1
    row = lambda i: (i, 0)
    grid_row = lambda i: (jnp.minimum(i, n_grid_tiles - 1), 0)
    full = lambda i: (0, 0)
    w = lambda a: pl.BlockSpec(a.shape, full)
    return pl.pallas_call(
        functools.partial(_outproj_kernel, n_grid_tiles=n_grid_tiles, n_valid=n_valid),
        grid=(n_tiles,),
        in_specs=[pl.BlockSpec((ROW_TILE, D_MODEL), row),
                  pl.BlockSpec((ROW_TILE, NA_WIDTH), grid_row), w(o_na_m),
                  pl.BlockSpec((MLA_WIDTH, ROW_TILE), lambda i: (0, jnp.minimum(i, n_grid_tiles - 1))), w(o_mla_m),
                  w(wo_na), w(wo_mla), w(g_ffn), w(wr_hi), w(wr_lo)],
        out_specs=[pl.BlockSpec((ROW_TILE, D_MODEL), row), pl.BlockSpec((ROW_TILE, D_MODEL), row),
                   pl.BlockSpec((None, N_EXPERTS, ROW_TILE), lambda i: (i, 0, 0))],
        out_shape=[jax.ShapeDtypeStruct((np_rows, D_MODEL), _F32), jax.ShapeDtypeStruct((np_rows, D_MODEL), _BF),
                   jax.ShapeDtypeStruct((n_tiles, N_EXPERTS, ROW_TILE), _F32)],
        compiler_params=pltpu.CompilerParams(dimension_semantics=("parallel",), vmem_limit_bytes=VMEM_LIMIT),
        name="outproj_router",
    )(x, o_na_g, o_na_m, o_mla_g, o_mla_m, wo_na, wo_mla, g_ffn, wr_hi, wr_lo)


SLOT_W = 72
WIN = 64


def _route_kernel(aff_ref, slot_ref, lo_ref, *, cap):
    n_tiles = aff_ref.shape[0]
    bits = lambda: pltpu.bitcast(aff_ref[...], jnp.int32)

    def bit_step(i, cur):
        cand = cur | jnp.left_shift(jnp.int32(1), 30 - i)
        cnt = jnp.sum((bits() >= cand).astype(jnp.int32), axis=(0, 2), keepdims=True)
        return jnp.where(cnt >= cap, cand, cur)
    thr = lax.fori_loop(0, 31, bit_step, jnp.zeros((1, N_EXPERTS, 1), jnp.int32))
    n_gt = jnp.sum((bits() > thr).astype(jnp.int32), axis=(0, 2), keepdims=True)
    need = (cap - n_gt)[0].astype(_F32)
    thr = thr[0]
    r = lax.broadcasted_iota(jnp.int32, (ROW_TILE, ROW_TILE), 0)
    c = lax.broadcasted_iota(jnp.int32, (ROW_TILE, ROW_TILE), 1)
    tri = jnp.where(r <= c, 1.0, 0.0).astype(_BF)
    as_bf = lambda m: jnp.where(m, 1.0, 0.0).astype(_BF)

    def tile_step(t, carry):
        tie_before, kept_before = carry
        b = pltpu.bitcast(aff_ref[t], jnp.int32)
        eq = b == thr
        tie_rank = _dot(as_bf(eq), tri) + tie_before
        keep = (b > thr) | (eq & (tie_rank <= need))
        kept_incl = _dot(as_bf(keep), tri) + kept_before
        slot_ref[t] = jnp.where(keep, kept_incl - 1.0, -1.0).astype(jnp.int32)
        lo_ref[t] = jnp.broadcast_to(kept_before, (N_EXPERTS, LANES)).astype(jnp.int32)
        return tie_rank[:, ROW_TILE - 1:], kept_incl[:, ROW_TILE - 1:]
    zero = jnp.zeros((N_EXPERTS, 1), _F32)
    lax.fori_loop(0, n_tiles, tile_step, (zero, zero))


def _route(aff, cap):
    n_tiles = aff.shape[0]
    full = lambda shape: pl.BlockSpec(shape, lambda i: (0, 0, 0))
    return pl.pallas_call(
        functools.partial(_route_kernel, cap=cap),
        grid=(1,),
        in_specs=[full(aff.shape)],
        out_specs=[full(aff.shape), full((n_tiles, N_EXPERTS, LANES))],
        out_shape=[jax.ShapeDtypeStruct(aff.shape, jnp.int32),
                   jax.ShapeDtypeStruct((n_tiles, N_EXPERTS, LANES), jnp.int32)],
        compiler_params=pltpu.CompilerParams(dimension_semantics=("arbitrary",), vmem_limit_bytes=VMEM_LIMIT),
        name="route",
    )(aff)


def _tile_max(lo_ref, i, start_of):
    m = lo_ref[(i + 1) * N_EXPERTS] - start_of(0)
    for e in range(1, N_EXPERTS):
        m = jnp.maximum(m, lo_ref[(i + 1) * N_EXPERTS + e] - start_of(e))
    return m


def _dispatch_kernel(lo_ref, hn_ref, slot_ref, xe_ref, stage, stage_extra, carry, sem, sem_extra, *, zero_from,
                     zero_rows):
    i = pl.program_id(0)
    s = i % 2
    base = lambda e: (lo_ref[i * N_EXPERTS + e] // 8) * 8
    next_base = lambda e: (lo_ref[(i + 1) * N_EXPERTS + e] // 8) * 8

    def compact(k, dst):
        w = lax.broadcasted_iota(jnp.int32, (SLOT_W, ROW_TILE), 0) + k * SLOT_W
        onehot = jnp.concatenate(
            [jnp.where(slot_ref[e:e + 1, :] - base(e) == w, 1.0, 0.0).astype(_BF) for e in range(N_EXPERTS)], axis=0)
        dst[...] = _dot(onehot, hn_ref[...])

    def copy(e, src, k, sm):
        dst_row = pl.multiple_of(base(e) + k * SLOT_W, 8)
        return pltpu.make_async_copy(src.at[pl.ds(e * SLOT_W, SLOT_W)], xe_ref.at[e, pl.ds(dst_row, SLOT_W)], sm)

    def save_carry(k, src):
        for e in range(N_EXPERTS):
            g = next_base(e) - base(e) - k * SLOT_W

            @pl.when((g >= 0) & (g < SLOT_W))
            def _():
                carry[e] = src[pl.ds(pl.multiple_of(e * SLOT_W + g, 8), 8), :]

    @pl.when(i == 0)
    def _():
        carry[...] = jnp.zeros(carry.shape, _F32)
        stage_extra[...] = jnp.zeros(stage_extra.shape, _F32)
        zero_copies = [pltpu.make_async_copy(stage_extra.at[pl.ds(0, zero_rows)],
                                             xe_ref.at[e, pl.ds(zero_from, zero_rows)], sem_extra)
                       for e in range(N_EXPERTS)]
        for cp in zero_copies:
            cp.start()
        for cp in zero_copies:
            cp.wait()

    compact(0, stage.at[s])
    for e in range(N_EXPERTS):
        stage[s, e * SLOT_W:e * SLOT_W + 8, :] += carry[e]
    save_carry(0, stage.at[s])

    @pl.when(i > 0)
    def _():
        for e in range(N_EXPERTS):
            copy(e, stage.at[1 - s], 0, sem.at[1 - s]).wait()

    for e in range(N_EXPERTS):
        copy(e, stage.at[s], 0, sem.at[s]).start()

    def extra_pass(k, c):
        compact(k, stage_extra)
        save_carry(k, stage_extra)
        for wait in (False, True):
            for e in range(N_EXPERTS):
                @pl.when(lo_ref[(i + 1) * N_EXPERTS + e] - base(e) >= k * SLOT_W)
                def _():
                    cp = copy(e, stage_extra, k, sem_extra)
                    cp.wait() if wait else cp.start()
        return c
    lax.fori_loop(1, _tile_max(lo_ref, i, base) // SLOT_W + 1, extra_pass, 0)

    @pl.when(i == pl.num_programs(0) - 1)
    def _():
        for e in range(N_EXPERTS):
            copy(e, stage.at[s], 0, sem.at[s]).wait()


def _dispatch(lo, hn, slot, cap, cap_pad):
    n_tiles = slot.shape[0]
    zero_from = cap // 8 * 8
    zero_rows = cap_pad + SLOT_W - zero_from
    assert zero_rows <= N_EXPERTS * SLOT_W and SLOT_W % 8 == 0
    return pl.pallas_call(
        functools.partial(_dispatch_kernel, zero_from=zero_from, zero_rows=zero_rows),
        grid_spec=pltpu.PrefetchScalarGridSpec(
            num_scalar_prefetch=1,
            grid=(n_tiles,),
            in_specs=[pl.BlockSpec((ROW_TILE, D_MODEL), lambda i, lo: (i, 0)),
                      pl.BlockSpec((None, N_EXPERTS, ROW_TILE), lambda i, lo: (i, 0, 0))],
            out_specs=pl.BlockSpec(memory_space=pl.ANY),
            scratch_shapes=[pltpu.VMEM((2, N_EXPERTS * SLOT_W, D_MODEL), _F32),
                            pltpu.VMEM((N_EXPERTS * SLOT_W, D_MODEL), _F32),
                            pltpu.VMEM((N_EXPERTS, 8, D_MODEL), _F32),
                            pltpu.SemaphoreType.DMA((2,)), pltpu.SemaphoreType.DMA(())]),
        out_shape=jax.ShapeDtypeStruct((N_EXPERTS, cap_pad + SLOT_W, D_MODEL), _F32),
        compiler_params=pltpu.CompilerParams(dimension_semantics=("arbitrary",), vmem_limit_bytes=VMEM_LIMIT),
        name="dispatch",
    )(lo, hn, slot)


def _expert_kernel(x_ref, wg_ref, wu_ref, wd_ref, y_ref):
    x = x_ref[...].astype(_BF)
    g = _dot(x, wg_ref[...])
    u = _dot(x, wu_ref[...])
    hid = (g * jax.nn.sigmoid(g) * u).astype(_BF)
    y_ref[...] = _dot(hid, wd_ref[...]).astype(_BF)


def _experts(xe, wg, wu, wd, cap_pad, tile):
    n_e = xe.shape[0]
    wspec = lambda a: pl.BlockSpec((None,) + a.shape[1:], lambda e, t: (e, 0, 0))
    return pl.pallas_call(
        _expert_kernel,
        grid=(n_e, cap_pad // tile),
        in_specs=[pl.BlockSpec((None, tile, D_MODEL), lambda e, t: (e, t, 0)), wspec(wg), wspec(wu), wspec(wd)],
        out_specs=pl.BlockSpec((None, tile, D_MODEL), lambda e, t: (e, t, 0)),
        out_shape=jax.ShapeDtypeStruct((n_e, cap_pad, D_MODEL), _BF),
        compiler_params=pltpu.CompilerParams(dimension_semantics=("parallel", "arbitrary"),
                                             vmem_limit_bytes=VMEM_LIMIT),
        name="experts",
    )(xe, wg, wu, wd)


def _combine_kernel(lo_ref, h_ref, slot_ref, aff_ref, g_ref, ye_ref, o_ref, win, win_extra, f_acc, sem, sem_extra, *,
                    cap_pad):
    i = pl.program_id(0)
    s = i % 2
    first = lambda t, e: (lo_ref[t * N_EXPERTS + e] // 16) * 16

    def start_row(t, e, k):
        return pl.multiple_of(jnp.minimum(first(t, e) + k * WIN, cap_pad - WIN), 16)

    def window_copies(t, k, buf, sm):
        return [pltpu.make_async_copy(ye_ref.at[e, pl.ds(start_row(t, e, k), WIN)], buf.at[pl.ds(e * WIN, WIN)], sm)
                for e in range(N_EXPERTS)]

    def weights(k):
        w = lax.broadcasted_iota(jnp.int32, (WIN, ROW_TILE), 0)
        parts = []
        for e in range(N_EXPERTS):
            slot = slot_ref[e:e + 1, :]
            hit = (slot == w + start_row(i, e, k)) & (slot >= first(i, e) + k * WIN)
            parts.append(jnp.where(hit, aff_ref[e:e + 1, :], 0.0).astype(_BF))
        return jnp.concatenate(parts, axis=0)

    scatter = lambda wts, rows: lax.dot_general(wts, rows, (((0,), (0,)), ((), ())), preferred_element_type=_F32)

    @pl.when(i == 0)
    def _():
        for cp in window_copies(0, 0, win.at[0], sem.at[0]):
            cp.start()

    @pl.when(i + 1 < pl.num_programs(0))
    def _():
        for cp in window_copies(i + 1, 0, win.at[1 - s], sem.at[1 - s]):
            cp.start()

    wts = weights(0)
    for cp in window_copies(i, 0, win.at[s], sem.at[s]):
        cp.wait()
    f_acc[...] = scatter(wts, win[s])

    def extra_pass(k, c):
        copies = window_copies(i, k, win_extra, sem_extra)
        for cp in copies:
            cp.start()
        wts_k = weights(k)
        for cp in copies:
            cp.wait()
        f_acc[...] += scatter(wts_k, win_extra[...])
        return c
    reach = _tile_max(lo_ref, i, lambda e: first(i, e))
    lax.fori_loop(1, (reach + WIN - 1) // WIN, extra_pass, 0)
    o_ref[...] = _rms(h_ref[...] + f_acc[...], g_ref[...])


def _combine(lo, h, slot, aff, g_final, ye, n_rows):
    cap_pad = ye.shape[1]
    assert cap_pad % 16 == 0 and WIN % 16 == 0 and cap_pad >= WIN
    tile3 = pl.BlockSpec((None, N_EXPERTS, ROW_TILE), lambda i, lo: (i, 0, 0))
    return pl.pallas_call(
        functools.partial(_combine_kernel, cap_pad=cap_pad),
        grid_spec=pltpu.PrefetchScalarGridSpec(
            num_scalar_prefetch=1,
            grid=(n_rows // ROW_TILE,),
            in_specs=[pl.BlockSpec((ROW_TILE, D_MODEL), lambda i, lo: (i, 0)), tile3, tile3,
                      pl.BlockSpec(g_final.shape, lambda i, lo: (0, 0)), pl.BlockSpec(memory_space=pl.ANY)],
            out_specs=pl.BlockSpec((ROW_TILE, D_MODEL), lambda i, lo: (i, 0)),
            scratch_shapes=[pltpu.VMEM((2, N_EXPERTS * WIN, D_MODEL), _BF), pltpu.VMEM((N_EXPERTS * WIN, D_MODEL), _BF),
                            pltpu.VMEM((ROW_TILE, D_MODEL), _F32),
                            pltpu.SemaphoreType.DMA((2,)), pltpu.SemaphoreType.DMA(())]),
        out_shape=jax.ShapeDtypeStruct((n_rows, D_MODEL), _F32),
        compiler_params=pltpu.CompilerParams(dimension_semantics=("arbitrary",), vmem_limit_bytes=VMEM_LIMIT),
        name="combine_norm",
    )(lo, h, slot, aff, g_final, ye)


def _rotate_half_cols(w):
    half = QK_ROPE // 2
    return jnp.concatenate([-w[..., half:], w[..., :half]], axis=-1)


def _prepare_weights(w_in, g_attn, g_q, w_uq, g_kv, w_ukv, w_o, g_ffn, w_router, w_gate, w_up, w_down, na_rpb):
    na_scale = 1.0 / math.sqrt(NA_HEAD_DIM)
    c0, c1, c2 = 3 * NA_WIDTH, 3 * NA_WIDTH + Q_LORA, 3 * NA_WIDTH + Q_LORA + KV_LORA
    w_kr = w_in[:, c2:]
    zpad = lambda n: jnp.zeros((D_MODEL, n), _F32)
    slot = lambda w: jnp.concatenate([zpad(QK_NOPE), w, zpad(LANES - QK_NOPE - QK_ROPE)], axis=1)
    w1 = jnp.concatenate([w_in[:, :NA_WIDTH] * na_scale, w_in[:, NA_WIDTH:c2], slot(w_kr),
                          slot(_rotate_half_cols(w_kr))], axis=1).astype(_BF)
    q_scale = LOG2E / math.sqrt(QK_NOPE + QK_ROPE)
    wq = (w_uq * q_scale).reshape(Q_LORA, MLA_HEADS, QK_NOPE + QK_ROPE)
    pad = LANES - QK_NOPE - QK_ROPE
    plain = jnp.pad(wq, ((0, 0), (0, 0), (0, pad)))
    rot = jnp.pad(_rotate_half_cols(wq[..., QK_NOPE:]), ((0, 0), (0, 0), (QK_NOPE, pad)))
    w2t = jnp.concatenate([plain.reshape(Q_LORA, -1), rot.reshape(Q_LORA, -1)], axis=1).T.astype(_BF)
    wkv = w_ukv.reshape(KV_LORA, MLA_HEADS, QK_NOPE + V_HEAD)
    w3k = jnp.pad(wkv[..., :QK_NOPE], ((0, 0), (0, 0), (0, LANES - QK_NOPE))).reshape(KV_LORA, -1).astype(_BF)
    w3vt = wkv[..., QK_NOPE:].reshape(KV_LORA, -1).T.astype(_BF)
    wr = w_router.T
    wr_hi = wr.astype(_BF)
    wr_lo = (wr - wr_hi.astype(_F32)).astype(_BF)
    return dict(
        w1=w1, w2t=w2t, w3k=w3k, w3vt=w3vt,
        g_attn=g_attn[None], g_q=g_q[None], g_kv=g_kv[None], g_ffn=g_ffn[None],
        wo_na=w_o[:NA_WIDTH].astype(_BF), wo_mla=w_o[NA_WIDTH:].astype(_BF),
        wr_hi=wr_hi, wr_lo=wr_lo,
        wg=w_gate.astype(_BF), wu=w_up.astype(_BF), wd=w_down.astype(_BF),
        na_bias=_na_bias_table(na_rpb),
    )


def _na_bias_table(rpb):
    ql = np.arange(NA_BLK)
    kl = np.arange(3 * NA_BLK)
    ri, qc = ql // GRID_W, ql % GRID_W
    kj, kc = kl // GRID_W, kl % GRID_W
    qcs = np.clip(qc - WIN_C // 2, 0, GRID_W - WIN_C)
    col_ok = (kc[None, :] >= qcs[:, None]) & (kc[None, :] < qcs[:, None] + WIN_C)
    lo = np.stack([np.full(NA_BLK, NA_ROWS), ri, np.zeros(NA_BLK, np.int64)])
    row_ok = (kj[None, None, :] >= lo[:, :, None]) & (kj[None, None, :] < lo[:, :, None] + WIN_R)
    ok = row_ok & col_ok[None]
    rows_q, rows_k, cols = np.arange(NA_ROWS), np.arange(3 * NA_ROWS), np.arange(GRID_W)
    dr = np.clip(rows_k[None, :] - rows_q[:, None] + (WIN_R - 1) - NA_ROWS, 0, 2 * WIN_R - 2)
    dc = np.clip(cols[None, :] - cols[:, None] + (WIN_C - 1), 0, 2 * WIN_C - 2)
    pick_r = (dr[..., None] == np.arange(2 * WIN_R - 1)).astype(np.float32)
    pick_c = (dc[..., None] == np.arange(2 * WIN_C - 1)).astype(np.float32)
    vals = jnp.einsum("hrc,ijr,qkc->hiqjk", rpb.astype(_F32), pick_r, pick_c, precision=lax.Precision.HIGHEST)
    vals = vals.reshape(NA_HEADS, NA_BLK, 3 * NA_BLK)
    return jnp.where(jnp.asarray(ok)[:, None], vals[None], NEG)


def _rope_tables(batch, seq, np_rows):
    half = QK_ROPE // 2
    inv_freq = ROPE_THETA ** (-jnp.arange(half, dtype=_F32) * 2.0 / QK_ROPE)
    pos_grid = jnp.tile(jnp.arange(seq, dtype=_F32) + N_META, batch)
    pos_meta = jnp.tile(jnp.arange(N_META, dtype=_F32), batch)
    pos = jnp.concatenate([pos_grid, pos_meta, jnp.zeros(np_rows - batch * (seq + N_META), _F32)])
    ang = pos[:, None] * inv_freq[None, :]
    cos, sin = jnp.cos(ang), jnp.sin(ang)
    pad = LANES - QK_NOPE - QK_ROPE
    cos_t = jnp.concatenate([jnp.ones((np_rows, QK_NOPE), _F32), cos, cos, jnp.zeros((np_rows, pad), _F32)], axis=1)
    sin_t = jnp.concatenate([jnp.zeros((np_rows, QK_NOPE), _F32), sin, sin, jnp.zeros((np_rows, pad), _F32)], axis=1)
    return cos_t, sin_t


def _expert_tile(cap):
    if cap <= 1024:
        return -(-cap // 16) * 16
    return min(range(512, 1025, 16), key=lambda t: (-(-cap // t) * t - cap, -t))


def _run_group(x, meta_tokens, prm, g_final, tq, tk):
    batch, seq, _ = x.shape
    n_grid = batch * seq
    n_valid = n_grid + batch * N_META
    np_rows = n_grid + ROW_TILE
    assert seq % NA_BLK == 0 and seq // GRID_W >= WIN_R and batch * N_META <= ROW_TILE
    meta = jnp.broadcast_to(meta_tokens[None], (batch, N_META, D_MODEL)).reshape(batch * N_META, D_MODEL)
    xt = jnp.concatenate([x.reshape(n_grid, D_MODEL), meta,
                          jnp.zeros((np_rows - n_valid, D_MODEL), _F32)], axis=0)
    cos_r, sin_r = _rope_tables(batch, seq, np_rows)
    qkv, qt, k3, vt = _proj(xt, cos_r, sin_r, cos_r.T, sin_r.T, prm["g_attn"], prm["g_q"], prm["g_kv"],
                            prm["w1"], prm["w2t"], prm["w3k"], prm["w3vt"])
    o_na_g = _na_attention(qkv, prm["na_bias"], batch, seq)
    o_na_m = _na_meta_attention(qkv, batch, seq)
    n_meta = batch * N_META
    km = k3[:, n_grid:n_grid + n_meta].reshape(MLA_HEADS, batch, N_META, LANES).transpose(1, 0, 2, 3)
    vmt = vt[:, :, n_grid:n_grid + n_meta].reshape(MLA_HEADS, V_EXT, batch, N_META).transpose(2, 0, 1, 3)
    ot_g = _mla_attention(qt, k3, vt, km, vmt, batch, seq, seq, 0, seq // tq, tq, tk)
    ot_m = _mla_attention(qt, k3, vt, km, vmt, batch, seq, ROW_TILE, n_grid // ROW_TILE, 0, ROW_TILE, tk)
    ot_m = ot_m.reshape(MLA_WIDTH, batch, ROW_TILE)
    ot_m = jnp.concatenate([ot_m[:, b, b * N_META:(b + 1) * N_META] for b in range(batch)], axis=1)
    ot_m = jnp.pad(ot_m, ((0, 0), (0, ROW_TILE - n_meta)))
    o_na_m = jnp.pad(o_na_m, ((0, ROW_TILE - n_meta), (0, 0)))
    h, hn, aff = _outproj(xt, o_na_g, o_na_m, ot_g.reshape(MLA_WIDTH, n_grid), ot_m, prm["wo_na"], prm["wo_mla"],
                          prm["g_ffn"], prm["wr_hi"], prm["wr_lo"], n_valid)
    cap = EC_CAPACITY * n_valid // N_EXPERTS
    tile = _expert_tile(cap)
    cap_pad = -(-cap // tile) * tile
    slot, lo3 = _route(aff, cap)
    lo = jnp.concatenate([lo3[:, :, 0], jnp.full((1, N_EXPERTS), cap, jnp.int32)]).reshape(-1)
    xe = _dispatch(lo, hn, slot, cap, cap_pad)
    ye = _experts(xe, prm["wg"], prm["wu"], prm["wd"], cap_pad, tile)
    return _combine(lo, h, slot, aff, g_final[None], ye, n_grid).reshape(batch, seq, D_MODEL)


def kernel(x_prompt, x_sample, meta_tokens, g_attn, w_in, na_rpb, g_q, w_uq, g_kv, w_ukv, w_o, g_ffn, w_router,
           w_gate, w_up, w_down, g_final):
    prm = _prepare_weights(w_in[0], g_attn[0], g_q[0], w_uq[0], g_kv[0], w_ukv[0], w_o[0], g_ffn[0], w_router[0],
                           w_gate[0], w_up[0], w_down[0], na_rpb[0])
    y_prompt = _run_group(x_prompt, meta_tokens, prm, g_final, MLA_TQ, MLA_TK)
    y_sample = _run_group(x_sample, meta_tokens, prm, g_final, MLA_TQ, MLA_TK)
    return (y_prompt, y_sample)
```

```python
import functools
import math

import jax
import jax.numpy as jnp
import numpy as np
from jax import lax
from jax.experimental import pallas as pl
from jax.experimental.pallas import tpu as pltpu

D_MODEL = 1024
GRID_W = 64
N_META = 16
WIN_R = 8
WIN_C = 16
NA_HEADS = 8
NA_HEAD_DIM = 64
MLA_HEADS = 8
QK_NOPE = 64
QK_ROPE = 32
V_HEAD = 64
Q_LORA = 768
KV_LORA = 256
ROPE_THETA = 10000.0
N_EXPERTS = 16
EC_CAPACITY = 2
D_EXPERT = 2048
NORM_EPS = 1e-6
NA_WIDTH = NA_HEADS * NA_HEAD_DIM
MLA_WIDTH = MLA_HEADS * V_HEAD

LANES = 128
ROW_TILE = 256
NA_ROWS = 4
NA_BLK = NA_ROWS * GRID_W
NA_LOOKAHEAD = 4
NEG = -1e30
LOG2E = 1.4426950408889634
VMEM_LIMIT = 56 * 1024 * 1024

_BF = jnp.bfloat16
_F32 = jnp.float32


def _dot(a, b):
    return jnp.dot(a, b, preferred_element_type=_F32)


def _dot_nt(a, b):
    return lax.dot_general(a, b, (((1,), (1,)), ((), ())), preferred_element_type=_F32)


def _rms(x, g):
    return x * lax.rsqrt(jnp.mean(x * x, axis=-1, keepdims=True) + NORM_EPS) * g


def _proj_kernel(x_ref, cos_ref, sin_ref, cost_ref, sint_ref, g_attn_ref, g_q_ref, g_kv_ref, w1_ref, w2t_ref, w3k_ref,
                 w3vt_ref, wqvt_ref, qkv_ref, qt_ref, k_ref, vt_ref, qnat_ref, vnat_ref):
    a = _rms(x_ref[...], g_attn_ref[...]).astype(_BF)
    p = _dot(a, w1_ref[...])
    qkv_ref[...] = p[:, :3 * NA_WIDTH].astype(_BF)
    o = 3 * NA_WIDTH
    cqn = _rms(p[:, o:o + Q_LORA], g_q_ref[...]).astype(_BF)
    o += Q_LORA
    ckvn = _rms(p[:, o:o + KV_LORA], g_kv_ref[...]).astype(_BF)
    o += KV_LORA
    kr = p[:, o:o + LANES] * cos_ref[...] + p[:, o + LANES:o + 2 * LANES] * sin_ref[...]
    k3 = _dot(ckvn, w3k_ref[...])
    q2t = _dot_nt(w2t_ref[...], cqn)
    vt = _dot_nt(w3vt_ref[...], ckvn)
    cos_t = cost_ref[...]
    sin_t = sint_ref[...]
    hw = MLA_HEADS * LANES
    ones_row = jnp.where(lax.broadcasted_iota(jnp.int32, (V_EXT - V_HEAD, x_ref.shape[0]), 0) == 0, 1.0, 0.0)
    for h in range(MLA_HEADS):
        sl = slice(h * LANES, (h + 1) * LANES)
        qt_ref[h] = (q2t[sl] * cos_t + q2t[hw + h * LANES:hw + (h + 1) * LANES] * sin_t).astype(_BF)
        k_ref[h] = (k3[:, sl] + kr).astype(_BF)
        vt_ref[h, :V_HEAD, :] = vt[h * V_HEAD:(h + 1) * V_HEAD].astype(_BF)
        vt_ref[h, V_HEAD:, :] = ones_row.astype(_BF)
    qvt = _dot_nt(wqvt_ref[...], a)
    zero_half = jnp.zeros((NA_HEAD_DIM, x_ref.shape[0]), _BF)
    for h in range(NA_HEADS):
        own = slice((h % 2) * NA_HEAD_DIM, (h % 2 + 1) * NA_HEAD_DIM)
        other = slice((1 - h % 2) * NA_HEAD_DIM, (2 - h % 2) * NA_HEAD_DIM)
        qnat_ref[h, own, :] = qvt[h * NA_HEAD_DIM:(h + 1) * NA_HEAD_DIM].astype(_BF)
        qnat_ref[h, other, :] = zero_half
        vnat_ref[h, :NA_HEAD_DIM, :] = qvt[NA_WIDTH + h * NA_HEAD_DIM:NA_WIDTH + (h + 1) * NA_HEAD_DIM].astype(_BF)
        vnat_ref[h, NA_HEAD_DIM:, :] = ones_row.astype(_BF)


def _proj(x, cos, sin, cos_t, sin_t, g_attn, g_q, g_kv, w1, w2t, w3k, w3vt, wqvt):
    np_rows = x.shape[0]
    row = lambda i: (i, 0)
    col = lambda i: (0, i)
    full = lambda i: (0, 0)
    w = lambda a: pl.BlockSpec(a.shape, full)
    return pl.pallas_call(
        _proj_kernel,
        grid=(np_rows // ROW_TILE,),
        in_specs=[pl.BlockSpec((ROW_TILE, D_MODEL), row), pl.BlockSpec((ROW_TILE, LANES), row),
                  pl.BlockSpec((ROW_TILE, LANES), row), pl.BlockSpec((LANES, ROW_TILE), col),
                  pl.BlockSpec((LANES, ROW_TILE), col), w(g_attn), w(g_q), w(g_kv), w(w1), w(w2t), w(w3k), w(w3vt),
                  w(wqvt)],
        out_specs=[pl.BlockSpec((ROW_TILE, 3 * NA_WIDTH), row),
                   pl.BlockSpec((MLA_HEADS, LANES, ROW_TILE), lambda i: (0, 0, i)),
                   pl.BlockSpec((MLA_HEADS, ROW_TILE, LANES), lambda i: (0, i, 0)),
                   pl.BlockSpec((MLA_HEADS, V_EXT, ROW_TILE), lambda i: (0, 0, i)),
                   pl.BlockSpec((NA_HEADS, LANES, ROW_TILE), lambda i: (0, 0, i)),
                   pl.BlockSpec((NA_HEADS, V_EXT, ROW_TILE), lambda i: (0, 0, i))],
        out_shape=[jax.ShapeDtypeStruct((np_rows, 3 * NA_WIDTH), _BF),
                   jax.ShapeDtypeStruct((MLA_HEADS, LANES, np_rows), _BF),
                   jax.ShapeDtypeStruct((MLA_HEADS, np_rows, LANES), _BF),
                   jax.ShapeDtypeStruct((MLA_HEADS, V_EXT, np_rows), _BF),
                   jax.ShapeDtypeStruct((NA_HEADS, LANES, np_rows), _BF),
                   jax.ShapeDtypeStruct((NA_HEADS, V_EXT, np_rows), _BF)],
        compiler_params=pltpu.CompilerParams(dimension_semantics=("parallel",), vmem_limit_bytes=VMEM_LIMIT),
        name="proj",
    )(x, cos, sin, cos_t, sin_t, g_attn, g_q, g_kv, w1, w2t, w3k, w3vt, wqvt)


def _na_kernel(qt_ref, kp_ref, kc_ref, kn_ref, vtp_ref, vtc_ref, vtn_ref, km_ref, vmt_ref, bias_ref, ot_ref):
    ks = (kp_ref, kc_ref, kn_ref)
    vts = (vtp_ref, vtc_ref, vtn_ref)
    steps = [(h, c) for h in range(NA_HEADS) for c in range(3)]
    lanes = lambda h: slice((h // 2) * LANES, (h // 2 + 1) * LANES)
    scores = lambda h, c: _dot(ks[c][:, lanes(h)], qt_ref[h])
    meta_s = [_dot(km_ref[:, lanes(h)], qt_ref[h]) for h in range(NA_HEADS)]
    meta_m = [jnp.max(sm, axis=0, keepdims=True) for sm in meta_s]
    meta_acc = [_dot(vmt_ref[h], jnp.exp2(meta_s[h] - meta_m[h]).astype(_BF)) for h in range(NA_HEADS)]
    pending = [scores(*st) for st in steps[:NA_LOOKAHEAD]]
    for n, (h, c) in enumerate(steps):
        if n + NA_LOOKAHEAD < len(steps):
            pending.append(scores(*steps[n + NA_LOOKAHEAD]))
        s = bias_ref[h, c * NA_BLK:(c + 1) * NA_BLK, :] + pending.pop(0)
        if c == 0:
            m, acc = meta_m[h], meta_acc[h]
        m_new = jnp.maximum(m, jnp.max(s, axis=0, keepdims=True))
        p = jnp.exp2(s - m_new).astype(_BF)
        acc = acc * jnp.exp2(m - m_new) + _dot(vts[c][h], p)
        m = m_new
        if c == 2:
            ot_ref[h] = (acc[:NA_HEAD_DIM] / acc[NA_HEAD_DIM:NA_HEAD_DIM + 1]).astype(_BF)


def _na_attention(qt, qkv, vt, vmt, bias_t, batch, seq):
    nblk = seq // NA_BLK
    meta_blk0 = batch * seq // N_META
    near = lambda b, j, off: b * nblk + jnp.clip(j + off, 0, nblk - 1)
    k_spec = lambda off: pl.BlockSpec((NA_BLK, NA_WIDTH), lambda b, j: (near(b, j, off), 1))
    vt_spec = lambda off: pl.BlockSpec((NA_HEADS, V_EXT, NA_BLK), lambda b, j: (0, 0, near(b, j, off)))
    cls = lambda b, j: (jnp.where(j == 0, 0, jnp.where(j == nblk - 1, 2, 1)), 0, 0, 0)
    return pl.pallas_call(
        _na_kernel,
        grid=(batch, nblk),
        in_specs=[pl.BlockSpec((NA_HEADS, LANES, NA_BLK), lambda b, j: (0, 0, b * nblk + j)),
                  k_spec(-1), k_spec(0), k_spec(1), vt_spec(-1), vt_spec(0), vt_spec(1),
                  pl.BlockSpec((N_META, NA_WIDTH), lambda b, j: (meta_blk0 + b, 1)),
                  pl.BlockSpec((None, NA_HEADS, V_EXT, N_META), lambda b, j: (b, 0, 0, 0)),
                  pl.BlockSpec((None, NA_HEADS, 3 * NA_BLK, NA_BLK), cls)],
        out_specs=pl.BlockSpec((NA_HEADS, NA_HEAD_DIM, NA_BLK), lambda b, j: (0, 0, b * nblk + j)),
        out_shape=jax.ShapeDtypeStruct((NA_HEADS, NA_HEAD_DIM, batch * seq), _BF),
        compiler_params=pltpu.CompilerParams(dimension_semantics=("parallel", "arbitrary"),
                                             vmem_limit_bytes=VMEM_LIMIT),
        name="na_attn",
    )(qt, qkv, qkv, qkv, vt, vt, vt, qkv, vmt, bias_t)


def _na_meta_kernel(q_ref, k_ref, v_ref, o_ref):
    for h in range(NA_HEADS):
        sl = slice(h * NA_HEAD_DIM, (h + 1) * NA_HEAD_DIM)
        s = _dot_nt(q_ref[:, sl], k_ref[:, sl])
        p = jnp.exp(s - jnp.max(s, axis=1, keepdims=True))
        l = jnp.sum(p, axis=1, keepdims=True)
        o_ref[:, sl] = (_dot(p.astype(_BF), v_ref[:, sl]) / l).astype(_BF)


def _na_meta_attention(qkv, batch, seq):
    meta_blk0 = batch * seq // N_META
    spec = lambda col: pl.BlockSpec((N_META, NA_WIDTH), lambda b: (meta_blk0 + b, col))
    return pl.pallas_call(
        _na_meta_kernel,
        grid=(batch,),
        in_specs=[spec(0), spec(1), spec(2)],
        out_specs=pl.BlockSpec((N_META, NA_WIDTH), lambda b: (b, 0)),
        out_shape=jax.ShapeDtypeStruct((batch * N_META, NA_WIDTH), _BF),
        compiler_params=pltpu.CompilerParams(dimension_semantics=("parallel",)),
        name="na_meta_attn",
    )(qkv, qkv, qkv)


V_EXT = V_HEAD + 16
MLA_CHUNK = 256
MLA_LOOKAHEAD = 5
MLA_QSUB = 256
MLA_TQ = 512
MLA_TK = 2048


def _mla_kernel(qt_ref, k_ref, vt_ref, km_ref, vmt_ref, ot_ref, m_sc, acc_sc, *, chunk):
    j = pl.program_id(2)
    tq = qt_ref.shape[2]
    nchunks = k_ref.shape[1] // chunk

    @pl.when(j == 0)
    def _():
        def init_head(h, carry):
            s = _dot(km_ref[h], qt_ref[h])
            m = jnp.max(s, axis=0, keepdims=True)
            p = jnp.exp2(s - m).astype(_BF)
            acc_sc[h] = _dot(vmt_ref[h], p)
            m_sc[h] = jnp.broadcast_to(m, (8, tq))
            return carry
        lax.fori_loop(0, MLA_HEADS, init_head, 0)

    qsub = min(MLA_QSUB, tq)
    steps = [(qs, h, c) for qs in range(tq // qsub) for h in range(MLA_HEADS) for c in range(nchunks)]
    lanes = lambda qs: slice(qs * qsub, (qs + 1) * qsub)
    scores = lambda qs, h, c: _dot(k_ref[h, c * chunk:(c + 1) * chunk, :], qt_ref[h, :, lanes(qs)])
    pending = [scores(*st) for st in steps[:MLA_LOOKAHEAD]]
    for n, (qs, h, c) in enumerate(steps):
        if n + MLA_LOOKAHEAD < len(steps):
            pending.append(scores(*steps[n + MLA_LOOKAHEAD]))
        s = pending.pop(0)
        if c == 0:
            m = m_sc[h, :1, lanes(qs)]
            acc = acc_sc[h, :, lanes(qs)]
        m_new = jnp.maximum(m, jnp.max(s, axis=0, keepdims=True))
        p = jnp.exp2(s - m_new).astype(_BF)
        acc = acc * jnp.exp2(m - m_new) + _dot(vt_ref[h, :, c * chunk:(c + 1) * chunk], p)
        m = m_new
        if c == nchunks - 1:
            m_sc[h, :, lanes(qs)] = jnp.broadcast_to(m, (8, qsub))
            acc_sc[h, :, lanes(qs)] = acc

    @pl.when(j == pl.num_programs(2) - 1)
    def _():
        for h in range(MLA_HEADS):
            acc = acc_sc[h]
            ot_ref[h] = (acc[:V_HEAD] / acc[V_HEAD:V_HEAD + 1]).astype(_BF)


def _mla_attention(qt, k, vt, km, vmt, batch, seq, q_rows, q_blk0, q_stride, tq, tk):
    nq = q_rows // tq
    nk = seq // tk
    return pl.pallas_call(
        functools.partial(_mla_kernel, chunk=min(MLA_CHUNK, tk)),
        grid=(batch, nq, nk),
        in_specs=[pl.BlockSpec((MLA_HEADS, LANES, tq), lambda b, i, j: (0, 0, q_blk0 + b * q_stride + i)),
                  pl.BlockSpec((MLA_HEADS, tk, LANES), lambda b, i, j: (0, b * nk + j, 0)),
                  pl.BlockSpec((MLA_HEADS, V_EXT, tk), lambda b, i, j: (0, 0, b * nk + j)),
                  pl.BlockSpec((None, MLA_HEADS, N_META, LANES), lambda b, i, j: (b, 0, 0, 0)),
                  pl.BlockSpec((None, MLA_HEADS, V_EXT, N_META), lambda b, i, j: (b, 0, 0, 0))],
        out_specs=pl.BlockSpec((MLA_HEADS, V_HEAD, tq), lambda b, i, j: (0, 0, b * nq + i)),
        out_shape=jax.ShapeDtypeStruct((MLA_HEADS, V_HEAD, batch * q_rows), _BF),
        scratch_shapes=[pltpu.VMEM((MLA_HEADS, 8, tq), _F32), pltpu.VMEM((MLA_HEADS, V_EXT, tq), _F32)],
        compiler_params=pltpu.CompilerParams(dimension_semantics=("parallel", "parallel", "arbitrary"),
                                             vmem_limit_bytes=VMEM_LIMIT),
        name="mla_attn",
    )(qt, k, vt, km, vmt)


def _outproj_kernel(x_ref, ong_ref, onm_ref, omg_ref, omm_ref, wo_na_ref, wo_mla_ref, g_ref, wr_hi_ref, wr_lo_ref,
                    h_ref, hn_ref, aff_ref, *, n_grid_tiles, n_valid):
    i = pl.program_id(0)
    is_meta = i == n_grid_tiles
    tn = lambda a_t, b: lax.dot_general(a_t, b, (((0,), (0,)), ((), ())), preferred_element_type=_F32)
    o_na_t = jnp.where(is_meta, onm_ref[...], ong_ref[...])
    o_mla_t = jnp.where(is_meta, omm_ref[...], omg_ref[...])
    h = x_ref[...] + tn(o_na_t, wo_na_ref[...]) + tn(o_mla_t, wo_mla_ref[...])
    h_ref[...] = h
    hn = _rms(h, g_ref[...])
    hn_hi = hn.astype(_BF)
    hn_ref[...] = hn_hi
    hn_lo = (hn - hn_hi.astype(_F32)).astype(_BF)
    wr_hi = wr_hi_ref[...]
    logits = _dot_nt(wr_hi, hn_hi) + _dot_nt(wr_hi, hn_lo) + _dot_nt(wr_lo_ref[...], hn_hi)
    e = jnp.exp(logits - jnp.max(logits, axis=0, keepdims=True))
    aff = e / jnp.sum(e, axis=0, keepdims=True)
    row = i * ROW_TILE + lax.broadcasted_iota(jnp.int32, aff.shape, 1)
    aff_ref[...] = jnp.where(row < n_valid, aff, -1.0)


def _outproj(x, o_na_g, o_na_m, o_mla_g, o_mla_m, wo_na, wo_mla, g_ffn, wr_hi, wr_lo, n_valid):
    np_rows = x.shape[0]
    n_tiles = np_rows // ROW_TILE
    n_grid_tiles = n_tiles - 1
    row = lambda i: (i, 0)
    grid_col = lambda i: (0, jnp.minimum(i, n_grid_tiles - 1))
    full = lambda i: (0, 0)
    w = lambda a: pl.BlockSpec(a.shape, full)
    return pl.pallas_call(
        functools.partial(_outproj_kernel, n_grid_tiles=n_grid_tiles, n_valid=n_valid),
        grid=(n_tiles,),
        in_specs=[pl.BlockSpec((ROW_TILE, D_MODEL), row),
                  pl.BlockSpec((NA_WIDTH, ROW_TILE), grid_col), w(o_na_m),
                  pl.BlockSpec((MLA_WIDTH, ROW_TILE), grid_col), w(o_mla_m),
                  w(wo_na), w(wo_mla), w(g_ffn), w(wr_hi), w(wr_lo)],
        out_specs=[pl.BlockSpec((ROW_TILE, D_MODEL), row), pl.BlockSpec((ROW_TILE, D_MODEL), row),
                   pl.BlockSpec((None, N_EXPERTS, ROW_TILE), lambda i: (i, 0, 0))],
        out_shape=[jax.ShapeDtypeStruct((np_rows, D_MODEL), _F32), jax.ShapeDtypeStruct((np_rows, D_MODEL), _BF),
                   jax.ShapeDtypeStruct((n_tiles, N_EXPERTS, ROW_TILE), _F32)],
        compiler_params=pltpu.CompilerParams(dimension_semantics=("parallel",), vmem_limit_bytes=VMEM_LIMIT),
        name="outproj_router",
    )(x, o_na_g, o_na_m, o_mla_g, o_mla_m, wo_na, wo_mla, g_ffn, wr_hi, wr_lo)


SLOT_W = 72
WIN = 64


def _route_kernel(aff_ref, slot_ref, lo_ref, *, cap):
    n_tiles = aff_ref.shape[0]
    bits = lambda: pltpu.bitcast(aff_ref[...], jnp.int32)

    def bit_step(i, cur):
        cand = cur | jnp.left_shift(jnp.int32(1), 30 - i)
        cnt = jnp.sum((bits() >= cand).astype(jnp.int32), axis=(0, 2), keepdims=True)
        return jnp.where(cnt >= cap, cand, cur)
    thr = lax.fori_loop(0, 31, bit_step, jnp.zeros((1, N_EXPERTS, 1), jnp.int32))
    n_gt = jnp.sum((bits() > thr).astype(jnp.int32), axis=(0, 2), keepdims=True)
    need = (cap - n_gt)[0].astype(_F32)
    thr = thr[0]
    r = lax.broadcasted_iota(jnp.int32, (ROW_TILE, ROW_TILE), 0)
    c = lax.broadcasted_iota(jnp.int32, (ROW_TILE, ROW_TILE), 1)
    tri = jnp.where(r <= c, 1.0, 0.0).astype(_BF)
    as_bf = lambda m: jnp.where(m, 1.0, 0.0).astype(_BF)

    def tile_step(t, carry):
        tie_before, kept_before = carry
        b = pltpu.bitcast(aff_ref[t], jnp.int32)
        eq = b == thr
        tie_rank = _dot(as_bf(eq), tri) + tie_before
        keep = (b > thr) | (eq & (tie_rank <= need))
        kept_incl = _dot(as_bf(keep), tri) + kept_before
        slot_ref[t] = jnp.where(keep, kept_incl - 1.0, -1.0).astype(jnp.int32)
        lo_ref[t] = jnp.broadcast_to(kept_before, (N_EXPERTS, LANES)).astype(jnp.int32)
        return tie_rank[:, ROW_TILE - 1:], kept_incl[:, ROW_TILE - 1:]
    zero = jnp.zeros((N_EXPERTS, 1), _F32)
    lax.fori_loop(0, n_tiles, tile_step, (zero, zero))


def _route(aff, cap):
    n_tiles = aff.shape[0]
    full = lambda shape: pl.BlockSpec(shape, lambda i: (0, 0, 0))
    return pl.pallas_call(
        functools.partial(_route_kernel, cap=cap),
        grid=(1,),
        in_specs=[full(aff.shape)],
        out_specs=[full(aff.shape), full((n_tiles, N_EXPERTS, LANES))],
        out_shape=[jax.ShapeDtypeStruct(aff.shape, jnp.int32),
                   jax.ShapeDtypeStruct((n_tiles, N_EXPERTS, LANES), jnp.int32)],
        compiler_params=pltpu.CompilerParams(dimension_semantics=("arbitrary",), vmem_limit_bytes=VMEM_LIMIT),
        name="route",
    )(aff)


def _tile_max(lo_ref, i, start_of):
    m = lo_ref[(i + 1) * N_EXPERTS] - start_of(0)
    for e in range(1, N_EXPERTS):
        m = jnp.maximum(m, lo_ref[(i + 1) * N_EXPERTS + e] - start_of(e))
    return m


def _dispatch_kernel(lo_ref, hn_ref, slot_ref, xe_ref, stage, stage_extra, carry, sem, sem_extra, *, zero_from,
                     zero_rows):
    i = pl.program_id(0)
    s = i % 2
    base = lambda e: (lo_ref[i * N_EXPERTS + e] // 8) * 8
    next_base = lambda e: (lo_ref[(i + 1) * N_EXPERTS + e] // 8) * 8

    def compact(k, dst):
        w = lax.broadcasted_iota(jnp.int32, (SLOT_W, ROW_TILE), 0) + k * SLOT_W
        onehot = jnp.concatenate(
            [jnp.where(slot_ref[e:e + 1, :] - base(e) == w, 1.0, 0.0).astype(_BF) for e in range(N_EXPERTS)], axis=0)
        dst[...] = _dot(onehot, hn_ref[...])

    def copy(e, src, k, sm):
        dst_row = pl.multiple_of(base(e) + k * SLOT_W, 8)
        return pltpu.make_async_copy(src.at[pl.ds(e * SLOT_W, SLOT_W)], xe_ref.at[e, pl.ds(dst_row, SLOT_W)], sm)

    def save_carry(k, src):
        for e in range(N_EXPERTS):
            g = next_base(e) - base(e) - k * SLOT_W

            @pl.when((g >= 0) & (g < SLOT_W))
            def _():
                carry[e] = src[pl.ds(pl.multiple_of(e * SLOT_W + g, 8), 8), :]

    @pl.when(i == 0)
    def _():
        carry[...] = jnp.zeros(carry.shape, _F32)
        stage_extra[...] = jnp.zeros(stage_extra.shape, _F32)
        zero_copies = [pltpu.make_async_copy(stage_extra.at[pl.ds(0, zero_rows)],
                                             xe_ref.at[e, pl.ds(zero_from, zero_rows)], sem_extra)
                       for e in range(N_EXPERTS)]
        for cp in zero_copies:
            cp.start()
        for cp in zero_copies:
            cp.wait()

    compact(0, stage.at[s])
    for e in range(N_EXPERTS):
        stage[s, e * SLOT_W:e * SLOT_W + 8, :] += carry[e]
    save_carry(0, stage.at[s])

    @pl.when(i > 0)
    def _():
        for e in range(N_EXPERTS):
            copy(e, stage.at[1 - s], 0, sem.at[1 - s]).wait()

    for e in range(N_EXPERTS):
        copy(e, stage.at[s], 0, sem.at[s]).start()

    def extra_pass(k, c):
        compact(k, stage_extra)
        save_carry(k, stage_extra)
        for wait in (False, True):
            for e in range(N_EXPERTS):
                @pl.when(lo_ref[(i + 1) * N_EXPERTS + e] - base(e) >= k * SLOT_W)
                def _():
                    cp = copy(e, stage_extra, k, sem_extra)
                    cp.wait() if wait else cp.start()
        return c
    lax.fori_loop(1, _tile_max(lo_ref, i, base) // SLOT_W + 1, extra_pass, 0)

    @pl.when(i == pl.num_programs(0) - 1)
    def _():
        for e in range(N_EXPERTS):
            copy(e, stage.at[s], 0, sem.at[s]).wait()


def _dispatch(lo, hn, slot, cap, cap_pad):
    n_tiles = slot.shape[0]
    zero_from = cap // 8 * 8
    zero_rows = cap_pad + SLOT_W - zero_from
    assert zero_rows <= N_EXPERTS * SLOT_W and SLOT_W % 8 == 0
    return pl.pallas_call(
        functools.partial(_dispatch_kernel, zero_from=zero_from, zero_rows=zero_rows),
        grid_spec=pltpu.PrefetchScalarGridSpec(
            num_scalar_prefetch=1,
            grid=(n_tiles,),
            in_specs=[pl.BlockSpec((ROW_TILE, D_MODEL), lambda i, lo: (i, 0)),
                      pl.BlockSpec((None, N_EXPERTS, ROW_TILE), lambda i, lo: (i, 0, 0))],
            out_specs=pl.BlockSpec(memory_space=pl.ANY),
            scratch_shapes=[pltpu.VMEM((2, N_EXPERTS * SLOT_W, D_MODEL), _F32),
                            pltpu.VMEM((N_EXPERTS * SLOT_W, D_MODEL), _F32),
                            pltpu.VMEM((N_EXPERTS, 8, D_MODEL), _F32),
                            pltpu.SemaphoreType.DMA((2,)), pltpu.SemaphoreType.DMA(())]),
        out_shape=jax.ShapeDtypeStruct((N_EXPERTS, cap_pad + SLOT_W, D_MODEL), _F32),
        compiler_params=pltpu.CompilerParams(dimension_semantics=("arbitrary",), vmem_limit_bytes=VMEM_LIMIT),
        name="dispatch",
    )(lo, hn, slot)


def _expert_kernel(x_ref, wg_ref, wu_ref, wd_ref, y_ref):
    x = x_ref[...].astype(_BF)
    g = _dot(x, wg_ref[...])
    u = _dot(x, wu_ref[...])
    hid = (g * jax.nn.sigmoid(g) * u).astype(_BF)
    y_ref[...] = _dot(hid, wd_ref[...]).astype(_BF)


def _experts(xe, wg, wu, wd, cap_pad, tile):
    n_e = xe.shape[0]
    wspec = lambda a: pl.BlockSpec((None,) + a.shape[1:], lambda e, t: (e, 0, 0))
    return pl.pallas_call(
        _expert_kernel,
        grid=(n_e, cap_pad // tile),
        in_specs=[pl.BlockSpec((None, tile, D_MODEL), lambda e, t: (e, t, 0)), wspec(wg), wspec(wu), wspec(wd)],
        out_specs=pl.BlockSpec((None, tile, D_MODEL), lambda e, t: (e, t, 0)),
        out_shape=jax.ShapeDtypeStruct((n_e, cap_pad, D_MODEL), _BF),
        compiler_params=pltpu.CompilerParams(dimension_semantics=("parallel", "arbitrary"),
                                             vmem_limit_bytes=VMEM_LIMIT),
        name="experts",
    )(xe, wg, wu, wd)


def _combine_kernel(lo_ref, h_ref, slot_ref, aff_ref, g_ref, ye_ref, o_ref, win, win_extra, f_acc, sem, sem_extra, *,
                    cap_pad):
    i = pl.program_id(0)
    s = i % 2
    first = lambda t, e: (lo_ref[t * N_EXPERTS + e] // 16) * 16

    def start_row(t, e, k):
        return pl.multiple_of(jnp.minimum(first(t, e) + k * WIN, cap_pad - WIN), 16)

    def window_copies(t, k, buf, sm):
        return [pltpu.make_async_copy(ye_ref.at[e, pl.ds(start_row(t, e, k), WIN)], buf.at[pl.ds(e * WIN, WIN)], sm)
                for e in range(N_EXPERTS)]

    def weights(k):
        w = lax.broadcasted_iota(jnp.int32, (WIN, ROW_TILE), 0)
        parts = []
        for e in range(N_EXPERTS):
            slot = slot_ref[e:e + 1, :]
            hit = (slot == w + start_row(i, e, k)) & (slot >= first(i, e) + k * WIN)
            parts.append(jnp.where(hit, aff_ref[e:e + 1, :], 0.0).astype(_BF))
        return jnp.concatenate(parts, axis=0)

    scatter = lambda wts, rows: lax.dot_general(wts, rows, (((0,), (0,)), ((), ())), preferred_element_type=_F32)

    @pl.when(i == 0)
    def _():
        for cp in window_copies(0, 0, win.at[0], sem.at[0]):
            cp.start()

    @pl.when(i + 1 < pl.num_programs(0))
    def _():
        for cp in window_copies(i + 1, 0, win.at[1 - s], sem.at[1 - s]):
            cp.start()

    wts = weights(0)
    for cp in window_copies(i, 0, win.at[s], sem.at[s]):
        cp.wait()
    f_acc[...] = scatter(wts, win[s])

    def extra_pass(k, c):
        copies = window_copies(i, k, win_extra, sem_extra)
        for cp in copies:
            cp.start()
        wts_k = weights(k)
        for cp in copies:
            cp.wait()
        f_acc[...] += scatter(wts_k, win_extra[...])
        return c
    reach = _tile_max(lo_ref, i, lambda e: first(i, e))
    lax.fori_loop(1, (reach + WIN - 1) // WIN, extra_pass, 0)
    o_ref[...] = _rms(h_ref[...] + f_acc[...], g_ref[...])


def _combine(lo, h, slot, aff, g_final, ye, n_rows):
    cap_pad = ye.shape[1]
    assert cap_pad % 16 == 0 and WIN % 16 == 0 and cap_pad >= WIN
    tile3 = pl.BlockSpec((None, N_EXPERTS, ROW_TILE), lambda i, lo: (i, 0, 0))
    return pl.pallas_call(
        functools.partial(_combine_kernel, cap_pad=cap_pad),
        grid_spec=pltpu.PrefetchScalarGridSpec(
            num_scalar_prefetch=1,
            grid=(n_rows // ROW_TILE,),
            in_specs=[pl.BlockSpec((ROW_TILE, D_MODEL), lambda i, lo: (i, 0)), tile3, tile3,
                      pl.BlockSpec(g_final.shape, lambda i, lo: (0, 0)), pl.BlockSpec(memory_space=pl.ANY)],
            out_specs=pl.BlockSpec((ROW_TILE, D_MODEL), lambda i, lo: (i, 0)),
            scratch_shapes=[pltpu.VMEM((2, N_EXPERTS * WIN, D_MODEL), _BF), pltpu.VMEM((N_EXPERTS * WIN, D_MODEL), _BF),
                            pltpu.VMEM((ROW_TILE, D_MODEL), _F32),
                            pltpu.SemaphoreType.DMA((2,)), pltpu.SemaphoreType.DMA(())]),
        out_shape=jax.ShapeDtypeStruct((n_rows, D_MODEL), _F32),
        compiler_params=pltpu.CompilerParams(dimension_semantics=("arbitrary",), vmem_limit_bytes=VMEM_LIMIT),
        name="combine_norm",
    )(lo, h, slot, aff, g_final, ye)


def _rotate_half_cols(w):
    half = QK_ROPE // 2
    return jnp.concatenate([-w[..., half:], w[..., :half]], axis=-1)


def _prepare_weights(w_in, g_attn, g_q, w_uq, g_kv, w_ukv, w_o, g_ffn, w_router, w_gate, w_up, w_down, na_rpb):
    na_scale = 1.0 / math.sqrt(NA_HEAD_DIM)
    c0, c1, c2 = 3 * NA_WIDTH, 3 * NA_WIDTH + Q_LORA, 3 * NA_WIDTH + Q_LORA + KV_LORA
    w_kr = w_in[:, c2:]
    zpad = lambda n: jnp.zeros((D_MODEL, n), _F32)
    slot = lambda w: jnp.concatenate([zpad(QK_NOPE), w, zpad(LANES - QK_NOPE - QK_ROPE)], axis=1)
    w1 = jnp.concatenate([w_in[:, :NA_WIDTH] * na_scale, w_in[:, NA_WIDTH:c2], slot(w_kr),
                          slot(_rotate_half_cols(w_kr))], axis=1).astype(_BF)
    q_scale = LOG2E / math.sqrt(QK_NOPE + QK_ROPE)
    wq = (w_uq * q_scale).reshape(Q_LORA, MLA_HEADS, QK_NOPE + QK_ROPE)
    pad = LANES - QK_NOPE - QK_ROPE
    plain = jnp.pad(wq, ((0, 0), (0, 0), (0, pad)))
    rot = jnp.pad(_rotate_half_cols(wq[..., QK_NOPE:]), ((0, 0), (0, 0), (QK_NOPE, pad)))
    w2t = jnp.concatenate([plain.reshape(Q_LORA, -1), rot.reshape(Q_LORA, -1)], axis=1).T.astype(_BF)
    wkv = w_ukv.reshape(KV_LORA, MLA_HEADS, QK_NOPE + V_HEAD)
    w3k = jnp.pad(wkv[..., :QK_NOPE], ((0, 0), (0, 0), (0, LANES - QK_NOPE))).reshape(KV_LORA, -1).astype(_BF)
    w3vt = wkv[..., QK_NOPE:].reshape(KV_LORA, -1).T.astype(_BF)
    wr = w_router.T
    wr_hi = wr.astype(_BF)
    wr_lo = (wr - wr_hi.astype(_F32)).astype(_BF)
    return dict(
        w1=w1, w2t=w2t, w3k=w3k, w3vt=w3vt,
        g_attn=g_attn[None], g_q=g_q[None], g_kv=g_kv[None], g_ffn=g_ffn[None],
        wo_na=w_o[:NA_WIDTH].astype(_BF), wo_mla=w_o[NA_WIDTH:].astype(_BF),
        wr_hi=wr_hi, wr_lo=wr_lo,
        wg=w_gate.astype(_BF), wu=w_up.astype(_BF), wd=w_down.astype(_BF),
        wqvt=jnp.concatenate([w_in[:, :NA_WIDTH] * (na_scale * LOG2E), w_in[:, 2 * NA_WIDTH:3 * NA_WIDTH]],
                             axis=1).T.astype(_BF),
        na_bias_t=_na_bias_table(na_rpb),
    )


def _na_bias_table(rpb):
    ql = np.arange(NA_BLK)
    kl = np.arange(3 * NA_BLK)
    ri, qc = ql // GRID_W, ql % GRID_W
    kj, kc = kl // GRID_W, kl % GRID_W
    qcs = np.clip(qc - WIN_C // 2, 0, GRID_W - WIN_C)
    col_ok = (kc[None, :] >= qcs[:, None]) & (kc[None, :] < qcs[:, None] + WIN_C)
    lo = np.stack([np.full(NA_BLK, NA_ROWS), ri, np.zeros(NA_BLK, np.int64)])
    row_ok = (kj[None, None, :] >= lo[:, :, None]) & (kj[None, None, :] < lo[:, :, None] + WIN_R)
    ok = row_ok & col_ok[None]
    rows_q, rows_k, cols = np.arange(NA_ROWS), np.arange(3 * NA_ROWS), np.arange(GRID_W)
    dr = np.clip(rows_k[None, :] - rows_q[:, None] + (WIN_R - 1) - NA_ROWS, 0, 2 * WIN_R - 2)
    dc = np.clip(cols[None, :] - cols[:, None] + (WIN_C - 1), 0, 2 * WIN_C - 2)
    pick_r = (dr[..., None] == np.arange(2 * WIN_R - 1)).astype(np.float32)
    pick_c = (dc[..., None] == np.arange(2 * WIN_C - 1)).astype(np.float32)
    vals = jnp.einsum("hrc,ijr,qkc->hiqjk", rpb.astype(_F32), pick_r, pick_c, precision=lax.Precision.HIGHEST)
    vals = vals.reshape(NA_HEADS, NA_BLK, 3 * NA_BLK)
    return jnp.where(jnp.asarray(ok)[:, None], vals[None] * LOG2E, NEG).transpose(0, 1, 3, 2)


def _rope_tables(batch, seq, np_rows):
    half = QK_ROPE // 2
    inv_freq = ROPE_THETA ** (-jnp.arange(half, dtype=_F32) * 2.0 / QK_ROPE)
    pos_grid = jnp.tile(jnp.arange(seq, dtype=_F32) + N_META, batch)
    pos_meta = jnp.tile(jnp.arange(N_META, dtype=_F32), batch)
    pos = jnp.concatenate([pos_grid, pos_meta, jnp.zeros(np_rows - batch * (seq + N_META), _F32)])
    ang = pos[:, None] * inv_freq[None, :]
    cos, sin = jnp.cos(ang), jnp.sin(ang)
    pad = LANES - QK_NOPE - QK_ROPE
    cos_t = jnp.concatenate([jnp.ones((np_rows, QK_NOPE), _F32), cos, cos, jnp.zeros((np_rows, pad), _F32)], axis=1)
    sin_t = jnp.concatenate([jnp.zeros((np_rows, QK_NOPE), _F32), sin, sin, jnp.zeros((np_rows, pad), _F32)], axis=1)
    return cos_t, sin_t


def _expert_tile(cap):
    if cap <= 1024:
        return -(-cap // 16) * 16
    return min(range(512, 1025, 16), key=lambda t: (-(-cap // t) * t - cap, -t))


def _run_group(x, meta_tokens, prm, g_final, tq, tk):
    batch, seq, _ = x.shape
    n_grid = batch * seq
    n_valid = n_grid + batch * N_META
    np_rows = n_grid + ROW_TILE
    assert seq % NA_BLK == 0 and seq // GRID_W >= WIN_R and batch * N_META <= ROW_TILE
    meta = jnp.broadcast_to(meta_tokens[None], (batch, N_META, D_MODEL)).reshape(batch * N_META, D_MODEL)
    xt = jnp.concatenate([x.reshape(n_grid, D_MODEL), meta,
                          jnp.zeros((np_rows - n_valid, D_MODEL), _F32)], axis=0)
    cos_r, sin_r = _rope_tables(batch, seq, np_rows)
    qkv, qt, k3, vt, qnat, vnat = _proj(xt, cos_r, sin_r, cos_r.T, sin_r.T, prm["g_attn"], prm["g_q"], prm["g_kv"],
                                        prm["w1"], prm["w2t"], prm["w3k"], prm["w3vt"], prm["wqvt"])
    n_meta = batch * N_META
    vmt_na = vnat[:, :, n_grid:n_grid + n_meta].reshape(NA_HEADS, V_EXT, batch, N_META).transpose(2, 0, 1, 3)
    o_na_g = _na_attention(qnat, qkv, vnat, vmt_na, prm["na_bias_t"], batch, seq).reshape(NA_WIDTH, n_grid)
    o_na_m = _na_meta_attention(qkv, batch, seq)
    km = k3[:, n_grid:n_grid + n_meta].reshape(MLA_HEADS, batch, N_META, LANES).transpose(1, 0, 2, 3)
    vmt = vt[:, :, n_grid:n_grid + n_meta].reshape(MLA_HEADS, V_EXT, batch, N_META).transpose(2, 0, 1, 3)
    ot_g = _mla_attention(qt, k3, vt, km, vmt, batch, seq, seq, 0, seq // tq, tq, tk)
    ot_m = _mla_attention(qt, k3, vt, km, vmt, batch, seq, ROW_TILE, n_grid // ROW_TILE, 0, ROW_TILE, tk)
    ot_m = ot_m.reshape(MLA_WIDTH, batch, ROW_TILE)
    ot_m = jnp.concatenate([ot_m[:, b, b * N_META:(b + 1) * N_META] for b in range(batch)], axis=1)
    ot_m = jnp.pad(ot_m, ((0, 0), (0, ROW_TILE - n_meta)))
    o_na_m = jnp.pad(o_na_m.T, ((0, 0), (0, ROW_TILE - n_meta)))
    h, hn, aff = _outproj(xt, o_na_g, o_na_m, ot_g.reshape(MLA_WIDTH, n_grid), ot_m, prm["wo_na"], prm["wo_mla"],
                          prm["g_ffn"], prm["wr_hi"], prm["wr_lo"], n_valid)
    cap = EC_CAPACITY * n_valid // N_EXPERTS
    tile = _expert_tile(cap)
    cap_pad = -(-cap // tile) * tile
    slot, lo3 = _route(aff, cap)
    lo = jnp.concatenate([lo3[:, :, 0], jnp.full((1, N_EXPERTS), cap, jnp.int32)]).reshape(-1)
    xe = _dispatch(lo, hn, slot, cap, cap_pad)
    ye = _experts(xe, prm["wg"], prm["wu"], prm["wd"], cap_pad, tile)
    return _combine(lo, h, slot, aff, g_final[None], ye, n_grid).reshape(batch, seq, D_MODEL)


def kernel(x_prompt, x_sample, meta_tokens, g_attn, w_in, na_rpb, g_q, w_uq, g_kv, w_ukv, w_o, g_ffn, w_router,
           w_gate, w_up, w_down, g_final):
    prm = _prepare_weights(w_in[0], g_attn[0], g_q[0], w_uq[0], g_kv[0], w_ukv[0], w_o[0], g_ffn[0], w_router[0],
                           w_gate[0], w_up[0], w_down[0], na_rpb[0])
    y_prompt = _run_group(x_prompt, meta_tokens, prm, g_final, MLA_TQ, MLA_TK)
    y_sample = _run_group(x_sample, meta_tokens, prm, g_final, MLA_TQ, MLA_TK)
    return (y_prompt, y_sample)
```

```python
import functools
import math

import jax
import jax.numpy as jnp
import numpy as np
from jax import lax
from jax.experimental import pallas as pl
from jax.experimental.pallas import tpu as pltpu

D_MODEL = 1024
GRID_W = 64
N_META = 16
WIN_R = 8
WIN_C = 16
NA_HEADS = 8
NA_HEAD_DIM = 64
MLA_HEADS = 8
QK_NOPE = 64
QK_ROPE = 32
V_HEAD = 64
Q_LORA = 768
KV_LORA = 256
ROPE_THETA = 10000.0
N_EXPERTS = 16
EC_CAPACITY = 2
D_EXPERT = 2048
NORM_EPS = 1e-6
NA_WIDTH = NA_HEADS * NA_HEAD_DIM
MLA_WIDTH = MLA_HEADS * V_HEAD

LANES = 128
ROW_TILE = 256
OUT_SUB = 128
NA_ROWS = 4
NA_BLK = NA_ROWS * GRID_W
NA_LOOKAHEAD = 4
NEG = -1e30
LOG2E = 1.4426950408889634
VMEM_LIMIT = 56 * 1024 * 1024

_BF = jnp.bfloat16
_F32 = jnp.float32


def _dot(a, b):
    return jnp.dot(a, b, preferred_element_type=_F32)


def _dot_nt(a, b):
    return lax.dot_general(a, b, (((1,), (1,)), ((), ())), preferred_element_type=_F32)


def _rms(x, g):
    return x * lax.rsqrt(jnp.mean(x * x, axis=-1, keepdims=True) + NORM_EPS) * g


def _proj_kernel(x_ref, xm_ref, cos_ref, sin_ref, cost_ref, sint_ref, g_attn_ref, g_q_ref, g_kv_ref, w1_ref, w2t_ref,
                 w3k_ref, w3vt_ref, wqvt_ref, kna_ref, qt_ref, k_ref, vt_ref, qnat_ref, vnat_ref):
    x = jnp.where(pl.program_id(0) == pl.num_programs(0) - 1, xm_ref[...], x_ref[...])
    a = _rms(x, g_attn_ref[...]).astype(_BF)
    p = _dot(a, w1_ref[...])
    kna_ref[...] = p[:, :NA_WIDTH].astype(_BF)
    o = NA_WIDTH
    cqn = _rms(p[:, o:o + Q_LORA], g_q_ref[...]).astype(_BF)
    o += Q_LORA
    ckvn = _rms(p[:, o:o + KV_LORA], g_kv_ref[...]).astype(_BF)
    o += KV_LORA
    kr = p[:, o:o + LANES] * cos_ref[...] + p[:, o + LANES:o + 2 * LANES] * sin_ref[...]
    k3 = _dot(ckvn, w3k_ref[...])
    q2t = _dot_nt(w2t_ref[...], cqn)
    vt = _dot_nt(w3vt_ref[...], ckvn)
    cos_t = cost_ref[...]
    sin_t = sint_ref[...]
    hw = MLA_HEADS * LANES
    ones_row = jnp.where(lax.broadcasted_iota(jnp.int32, (V_EXT - V_HEAD, x_ref.shape[0]), 0) == 0, 1.0, 0.0)
    for h in range(MLA_HEADS):
        sl = slice(h * LANES, (h + 1) * LANES)
        qt_ref[h] = (q2t[sl] * cos_t + q2t[hw + h * LANES:hw + (h + 1) * LANES] * sin_t).astype(_BF)
        k_ref[h] = (k3[:, sl] + kr).astype(_BF)
        vt_ref[h, :V_HEAD, :] = vt[h * V_HEAD:(h + 1) * V_HEAD].astype(_BF)
        vt_ref[h, V_HEAD:, :] = ones_row.astype(_BF)
    qvt = _dot_nt(wqvt_ref[...], a)
    zero_half = jnp.zeros((NA_HEAD_DIM, x_ref.shape[0]), _BF)
    for h in range(NA_HEADS):
        own = slice((h % 2) * NA_HEAD_DIM, (h % 2 + 1) * NA_HEAD_DIM)
        other = slice((1 - h % 2) * NA_HEAD_DIM, (2 - h % 2) * NA_HEAD_DIM)
        qnat_ref[h, own, :] = qvt[h * NA_HEAD_DIM:(h + 1) * NA_HEAD_DIM].astype(_BF)
        qnat_ref[h, other, :] = zero_half
        vnat_ref[h, :NA_HEAD_DIM, :] = qvt[NA_WIDTH + h * NA_HEAD_DIM:NA_WIDTH + (h + 1) * NA_HEAD_DIM].astype(_BF)
        vnat_ref[h, NA_HEAD_DIM:, :] = ones_row.astype(_BF)


def _proj(x, x_meta, cos, sin, cos_t, sin_t, g_attn, g_q, g_kv, w1, w2t, w3k, w3vt, wqvt):
    n_grid_tiles = x.shape[0] // ROW_TILE
    np_rows = x.shape[0] + ROW_TILE
    row = lambda i: (i, 0)
    col = lambda i: (0, i)
    full = lambda i: (0, 0)
    w = lambda a: pl.BlockSpec(a.shape, full)
    return pl.pallas_call(
        _proj_kernel,
        grid=(np_rows // ROW_TILE,),
        in_specs=[pl.BlockSpec((ROW_TILE, D_MODEL), lambda i: (jnp.minimum(i, n_grid_tiles - 1), 0)), w(x_meta),
                  pl.BlockSpec((ROW_TILE, LANES), row),
                  pl.BlockSpec((ROW_TILE, LANES), row), pl.BlockSpec((LANES, ROW_TILE), col),
                  pl.BlockSpec((LANES, ROW_TILE), col), w(g_attn), w(g_q), w(g_kv), w(w1), w(w2t), w(w3k), w(w3vt),
                  w(wqvt)],
        out_specs=[pl.BlockSpec((ROW_TILE, NA_WIDTH), row),
                   pl.BlockSpec((MLA_HEADS, LANES, ROW_TILE), lambda i: (0, 0, i)),
                   pl.BlockSpec((MLA_HEADS, ROW_TILE, LANES), lambda i: (0, i, 0)),
                   pl.BlockSpec((MLA_HEADS, V_EXT, ROW_TILE), lambda i: (0, 0, i)),
                   pl.BlockSpec((NA_HEADS, LANES, ROW_TILE), lambda i: (0, 0, i)),
                   pl.BlockSpec((NA_HEADS, V_EXT, ROW_TILE), lambda i: (0, 0, i))],
        out_shape=[jax.ShapeDtypeStruct((np_rows, NA_WIDTH), _BF),
                   jax.ShapeDtypeStruct((MLA_HEADS, LANES, np_rows), _BF),
                   jax.ShapeDtypeStruct((MLA_HEADS, np_rows, LANES), _BF),
                   jax.ShapeDtypeStruct((MLA_HEADS, V_EXT, np_rows), _BF),
                   jax.ShapeDtypeStruct((NA_HEADS, LANES, np_rows), _BF),
                   jax.ShapeDtypeStruct((NA_HEADS, V_EXT, np_rows), _BF)],
        compiler_params=pltpu.CompilerParams(dimension_semantics=("parallel",), vmem_limit_bytes=VMEM_LIMIT),
        name="proj",
    )(x, x_meta, cos, sin, cos_t, sin_t, g_attn, g_q, g_kv, w1, w2t, w3k, w3vt, wqvt)


def _na_kernel(qt_ref, kp_ref, kc_ref, kn_ref, vtp_ref, vtc_ref, vtn_ref, km_ref, vmt_ref, bias_ref, ot_ref):
    ks = (kp_ref, kc_ref, kn_ref)
    vts = (vtp_ref, vtc_ref, vtn_ref)
    steps = [(h, c) for h in range(NA_HEADS) for c in range(3)]
    lanes = lambda h: slice((h // 2) * LANES, (h // 2 + 1) * LANES)
    scores = lambda h, c: _dot(ks[c][:, lanes(h)], qt_ref[h])
    meta_s = [_dot(km_ref[:, lanes(h)], qt_ref[h]) for h in range(NA_HEADS)]
    meta_m = [jnp.max(sm, axis=0, keepdims=True) for sm in meta_s]
    meta_acc = [_dot(vmt_ref[h], jnp.exp2(meta_s[h] - meta_m[h]).astype(_BF)) for h in range(NA_HEADS)]
    pending = [scores(*st) for st in steps[:NA_LOOKAHEAD]]
    for n, (h, c) in enumerate(steps):
        if n + NA_LOOKAHEAD < len(steps):
            pending.append(scores(*steps[n + NA_LOOKAHEAD]))
        s = bias_ref[h, c * NA_BLK:(c + 1) * NA_BLK, :] + pending.pop(0)
        if c == 0:
            m, acc = meta_m[h], meta_acc[h]
        m_new = jnp.maximum(m, jnp.max(s, axis=0, keepdims=True))
        p = jnp.exp2(s - m_new).astype(_BF)
        acc = acc * jnp.exp2(m - m_new) + _dot(vts[c][h], p)
        m = m_new
        if c == 2:
            ot_ref[h] = (acc[:NA_HEAD_DIM] / acc[NA_HEAD_DIM:NA_HEAD_DIM + 1]).astype(_BF)


def _na_attention(qt, k, vt, vmt, bias_t, batch, seq):
    nblk = seq // NA_BLK
    meta_blk0 = batch * seq // N_META
    near = lambda b, j, off: b * nblk + jnp.clip(j + off, 0, nblk - 1)
    k_spec = lambda off: pl.BlockSpec((NA_BLK, NA_WIDTH), lambda b, j: (near(b, j, off), 0))
    vt_spec = lambda off: pl.BlockSpec((NA_HEADS, V_EXT, NA_BLK), lambda b, j: (0, 0, near(b, j, off)))
    cls = lambda b, j: (jnp.where(j == 0, 0, jnp.where(j == nblk - 1, 2, 1)), 0, 0, 0)
    return pl.pallas_call(
        _na_kernel,
        grid=(batch, nblk),
        in_specs=[pl.BlockSpec((NA_HEADS, LANES, NA_BLK), lambda b, j: (0, 0, b * nblk + j)),
                  k_spec(-1), k_spec(0), k_spec(1), vt_spec(-1), vt_spec(0), vt_spec(1),
                  pl.BlockSpec((N_META, NA_WIDTH), lambda b, j: (meta_blk0 + b, 0)),
                  pl.BlockSpec((None, NA_HEADS, V_EXT, N_META), lambda b, j: (b, 0, 0, 0)),
                  pl.BlockSpec((None, NA_HEADS, 3 * NA_BLK, NA_BLK), cls)],
        out_specs=pl.BlockSpec((NA_HEADS, NA_HEAD_DIM, NA_BLK), lambda b, j: (0, 0, b * nblk + j)),
        out_shape=jax.ShapeDtypeStruct((NA_HEADS, NA_HEAD_DIM, batch * seq), _BF),
        compiler_params=pltpu.CompilerParams(dimension_semantics=("parallel", "arbitrary"),
                                             vmem_limit_bytes=VMEM_LIMIT),
        name="na_attn",
    )(qt, k, k, k, vt, vt, vt, k, vmt, bias_t)


def _na_meta_kernel(q_ref, k_ref, v_ref, o_ref):
    for h in range(NA_HEADS):
        sl = slice(h * NA_HEAD_DIM, (h + 1) * NA_HEAD_DIM)
        s = _dot_nt(q_ref[:, sl], k_ref[:, sl])
        p = jnp.exp2(s - jnp.max(s, axis=1, keepdims=True))
        l = jnp.sum(p, axis=1, keepdims=True)
        o_ref[:, sl] = (_dot(p.astype(_BF), v_ref[:, sl]) / l).astype(_BF)


def _na_meta_attention(q, k, v, batch):
    spec = pl.BlockSpec((N_META, NA_WIDTH), lambda b: (b, 0))
    return pl.pallas_call(
        _na_meta_kernel,
        grid=(batch,),
        in_specs=[spec, spec, spec],
        out_specs=spec,
        out_shape=jax.ShapeDtypeStruct((batch * N_META, NA_WIDTH), _BF),
        compiler_params=pltpu.CompilerParams(dimension_semantics=("parallel",)),
        name="na_meta_attn",
    )(q, k, v)


V_EXT = V_HEAD + 16
MLA_CHUNK = 256
MLA_LOOKAHEAD = 5
MLA_QSUB = 256
MLA_TQ = 512
MLA_TK = 2048


def _mla_kernel(qt_ref, k_ref, vt_ref, km_ref, vmt_ref, ot_ref, m_sc, acc_sc, *, chunk):
    j = pl.program_id(2)
    tq = qt_ref.shape[2]
    nchunks = k_ref.shape[1] // chunk

    @pl.when(j == 0)
    def _():
        def init_head(h, carry):
            s = _dot(km_ref[h], qt_ref[h])
            m = jnp.max(s, axis=0, keepdims=True)
            p = jnp.exp2(s - m).astype(_BF)
            acc_sc[h] = _dot(vmt_ref[h], p)
            m_sc[h] = jnp.broadcast_to(m, (8, tq))
            return carry
        lax.fori_loop(0, MLA_HEADS, init_head, 0)

    qsub = min(MLA_QSUB, tq)
    steps = [(qs, h, c) for qs in range(tq // qsub) for h in range(MLA_HEADS) for c in range(nchunks)]
    lanes = lambda qs: slice(qs * qsub, (qs + 1) * qsub)
    scores = lambda qs, h, c: _dot(k_ref[h, c * chunk:(c + 1) * chunk, :], qt_ref[h, :, lanes(qs)])
    pending = [scores(*st) for st in steps[:MLA_LOOKAHEAD]]
    for n, (qs, h, c) in enumerate(steps):
        if n + MLA_LOOKAHEAD < len(steps):
            pending.append(scores(*steps[n + MLA_LOOKAHEAD]))
        s = pending.pop(0)
        if c == 0:
            m = m_sc[h, :1, lanes(qs)]
            acc = acc_sc[h, :, lanes(qs)]
        m_new = jnp.maximum(m, jnp.max(s, axis=0, keepdims=True))
        p = jnp.exp2(s - m_new).astype(_BF)
        acc = acc * jnp.exp2(m - m_new) + _dot(vt_ref[h, :, c * chunk:(c + 1) * chunk], p)
        m = m_new
        if c == nchunks - 1:
            m_sc[h, :, lanes(qs)] = jnp.broadcast_to(m, (8, qsub))
            acc_sc[h, :, lanes(qs)] = acc

    @pl.when(j == pl.num_programs(2) - 1)
    def _():
        for h in range(MLA_HEADS):
            acc = acc_sc[h]
            ot_ref[h] = (acc[:V_HEAD] / acc[V_HEAD:V_HEAD + 1]).astype(_BF)


def _mla_attention(qt, k, vt, km, vmt, batch, seq, q_rows, q_blk0, q_stride, tq, tk):
    nq = q_rows // tq
    nk = seq // tk
    return pl.pallas_call(
        functools.partial(_mla_kernel, chunk=min(MLA_CHUNK, tk)),
        grid=(batch, nq, nk),
        in_specs=[pl.BlockSpec((MLA_HEADS, LANES, tq), lambda b, i, j: (0, 0, q_blk0 + b * q_stride + i)),
                  pl.BlockSpec((MLA_HEADS, tk, LANES), lambda b, i, j: (0, b * nk + j, 0)),
                  pl.BlockSpec((MLA_HEADS, V_EXT, tk), lambda b, i, j: (0, 0, b * nk + j)),
                  pl.BlockSpec((None, MLA_HEADS, N_META, LANES), lambda b, i, j: (b, 0, 0, 0)),
                  pl.BlockSpec((None, MLA_HEADS, V_EXT, N_META), lambda b, i, j: (b, 0, 0, 0))],
        out_specs=pl.BlockSpec((MLA_HEADS, V_HEAD, tq), lambda b, i, j: (0, 0, b * nq + i)),
        out_shape=jax.ShapeDtypeStruct((MLA_HEADS, V_HEAD, batch * q_rows), _BF),
        scratch_shapes=[pltpu.VMEM((MLA_HEADS, 8, tq), _F32), pltpu.VMEM((MLA_HEADS, V_EXT, tq), _F32)],
        compiler_params=pltpu.CompilerParams(dimension_semantics=("parallel", "parallel", "arbitrary"),
                                             vmem_limit_bytes=VMEM_LIMIT),
        name="mla_attn",
    )(qt, k, vt, km, vmt)


def _outproj_kernel(x_ref, xm_ref, ong_ref, onm_ref, omg_ref, omm_ref, wo_na_ref, wo_mla_ref, g_ref, wr_hi_ref,
                    wr_lo_ref, h_ref, hn_ref, aff_ref, *, n_grid_tiles, n_valid):
    i = pl.program_id(0)
    is_meta = i == n_grid_tiles
    tn = lambda a_t, b: lax.dot_general(a_t, b, (((0,), (0,)), ((), ())), preferred_element_type=_F32)
    wr_hi = wr_hi_ref[...]
    subs = [slice(r0, r0 + OUT_SUB) for r0 in range(0, ROW_TILE, OUT_SUB)]
    hs = []
    for rows in subs:
        o_na_t = jnp.where(is_meta, onm_ref[:, rows], ong_ref[:, rows])
        o_mla_t = jnp.where(is_meta, omm_ref[:, rows], omg_ref[:, rows])
        x = jnp.where(is_meta, xm_ref[rows, :], x_ref[rows, :])
        hs.append(x + tn(o_na_t, wo_na_ref[...]) + tn(o_mla_t, wo_mla_ref[...]))
    for rows, h in zip(subs, hs):
        r0 = rows.start
        h_ref[rows, :] = h
        hn = _rms(h, g_ref[...])
        hn_hi = hn.astype(_BF)
        hn_ref[rows, :] = hn_hi
        hn_lo = (hn - hn_hi.astype(_F32)).astype(_BF)
        logits = _dot_nt(wr_hi, hn_hi) + _dot_nt(wr_hi, hn_lo) + _dot_nt(wr_lo_ref[...], hn_hi)
        e = jnp.exp(logits - jnp.max(logits, axis=0, keepdims=True))
        aff = e / jnp.sum(e, axis=0, keepdims=True)
        row = i * ROW_TILE + r0 + lax.broadcasted_iota(jnp.int32, aff.shape, 1)
        aff_ref[:, rows] = jnp.where(row < n_valid, aff, -1.0)


def _outproj(x, x_meta, o_na_g, o_na_m, o_mla_g, o_mla_m, wo_na, wo_mla, g_ffn, wr_hi, wr_lo, n_valid):
    n_grid_tiles = x.shape[0] // ROW_TILE
    n_tiles = n_grid_tiles + 1
    np_rows = n_tiles * ROW_TILE
    row = lambda i: (i, 0)
    grid_col = lambda i: (0, jnp.minimum(i, n_grid_tiles - 1))
    full = lambda i: (0, 0)
    w = lambda a: pl.BlockSpec(a.shape, full)
    return pl.pallas_call(
        functools.partial(_outproj_kernel, n_grid_tiles=n_grid_tiles, n_valid=n_valid),
        grid=(n_tiles,),
        in_specs=[pl.BlockSpec((ROW_TILE, D_MODEL), lambda i: (jnp.minimum(i, n_grid_tiles - 1), 0)), w(x_meta),
                  pl.BlockSpec((NA_WIDTH, ROW_TILE), grid_col), w(o_na_m),
                  pl.BlockSpec((MLA_WIDTH, ROW_TILE), grid_col), w(o_mla_m),
                  w(wo_na), w(wo_mla), w(g_ffn), w(wr_hi), w(wr_lo)],
        out_specs=[pl.BlockSpec((ROW_TILE, D_MODEL), row), pl.BlockSpec((ROW_TILE, D_MODEL), row),
                   pl.BlockSpec((None, N_EXPERTS, ROW_TILE), lambda i: (i, 0, 0))],
        out_shape=[jax.ShapeDtypeStruct((np_rows, D_MODEL), _F32), jax.ShapeDtypeStruct((np_rows, D_MODEL), _BF),
                   jax.ShapeDtypeStruct((n_tiles, N_EXPERTS, ROW_TILE), _F32)],
        compiler_params=pltpu.CompilerParams(dimension_semantics=("parallel",), vmem_limit_bytes=VMEM_LIMIT),
        name="outproj_router",
    )(x, x_meta, o_na_g, o_na_m, o_mla_g, o_mla_m, wo_na, wo_mla, g_ffn, wr_hi, wr_lo)


XE_ALIGN = 16
SLOT_W = 64
WIN = 64


def _route_kernel(aff_ref, slot_ref, lo_ref, *, cap):
    n_tiles = aff_ref.shape[0]
    bits = lambda: pltpu.bitcast(aff_ref[...], jnp.int32)

    def bit_step(i, cur):
        cand = cur | jnp.left_shift(jnp.int32(1), 30 - i)
        cnt = jnp.sum((bits() >= cand).astype(jnp.int32), axis=(0, 2), keepdims=True)
        return jnp.where(cnt >= cap, cand, cur)
    thr = lax.fori_loop(0, 31, bit_step, jnp.zeros((1, N_EXPERTS, 1), jnp.int32))
    n_gt = jnp.sum((bits() > thr).astype(jnp.int32), axis=(0, 2), keepdims=True)
    need = (cap - n_gt)[0].astype(_F32)
    thr = thr[0]
    r = lax.broadcasted_iota(jnp.int32, (ROW_TILE, ROW_TILE), 0)
    c = lax.broadcasted_iota(jnp.int32, (ROW_TILE, ROW_TILE), 1)
    tri = jnp.where(r <= c, 1.0, 0.0).astype(_BF)
    as_bf = lambda m: jnp.where(m, 1.0, 0.0).astype(_BF)

    def tile_step(t, carry):
        tie_before, kept_before = carry
        b = pltpu.bitcast(aff_ref[t], jnp.int32)
        eq = b == thr
        tie_rank = _dot(as_bf(eq), tri) + tie_before
        keep = (b > thr) | (eq & (tie_rank <= need))
        kept_incl = _dot(as_bf(keep), tri) + kept_before
        slot_ref[t] = jnp.where(keep, kept_incl - 1.0, -1.0).astype(jnp.int32)
        lo_ref[t] = jnp.broadcast_to(kept_before, (N_EXPERTS, LANES)).astype(jnp.int32)
        return tie_rank[:, ROW_TILE - 1:], kept_incl[:, ROW_TILE - 1:]
    zero = jnp.zeros((N_EXPERTS, 1), _F32)
    lax.fori_loop(0, n_tiles, tile_step, (zero, zero))


def _route(aff, cap):
    n_tiles = aff.shape[0]
    full = lambda shape: pl.BlockSpec(shape, lambda i: (0, 0, 0))
    return pl.pallas_call(
        functools.partial(_route_kernel, cap=cap),
        grid=(1,),
        in_specs=[full(aff.shape)],
        out_specs=[full(aff.shape), full((n_tiles, N_EXPERTS, LANES))],
        out_shape=[jax.ShapeDtypeStruct(aff.shape, jnp.int32),
                   jax.ShapeDtypeStruct((n_tiles, N_EXPERTS, LANES), jnp.int32)],
        compiler_params=pltpu.CompilerParams(dimension_semantics=("arbitrary",), vmem_limit_bytes=VMEM_LIMIT),
        name="route",
    )(aff)


def _tile_max(lo_ref, i, start_of):
    m = lo_ref[(i + 1) * N_EXPERTS] - start_of(0)
    for e in range(1, N_EXPERTS):
        m = jnp.maximum(m, lo_ref[(i + 1) * N_EXPERTS + e] - start_of(e))
    return m


def _dispatch_kernel(lo_ref, hn_ref, slot_ref, xe_ref, stage, stage_extra, carry, sem, sem_extra, *, zero_from,
                     zero_rows):
    i = pl.program_id(0)
    s = i % 2
    base = lambda e: (lo_ref[i * N_EXPERTS + e] // XE_ALIGN) * XE_ALIGN
    next_base = lambda e: (lo_ref[(i + 1) * N_EXPERTS + e] // XE_ALIGN) * XE_ALIGN

    def compact(k, dst):
        w = lax.broadcasted_iota(jnp.int32, (SLOT_W, ROW_TILE), 0) + k * SLOT_W
        onehot = jnp.concatenate(
            [jnp.where(slot_ref[e:e + 1, :] - base(e) == w, 1.0, 0.0).astype(_BF) for e in range(N_EXPERTS)], axis=0)
        dst[...] = _dot(onehot, hn_ref[...]).astype(_BF)

    def copy(e, src, k, sm):
        dst_row = pl.multiple_of(base(e) + k * SLOT_W, XE_ALIGN)
        return pltpu.make_async_copy(src.at[pl.ds(e * SLOT_W, SLOT_W)], xe_ref.at[e, pl.ds(dst_row, SLOT_W)], sm)

    def save_carry(k, src):
        for e in range(N_EXPERTS):
            g = next_base(e) - base(e) - k * SLOT_W

            @pl.when((g >= 0) & (g < SLOT_W))
            def _():
                carry[e] = src[pl.ds(pl.multiple_of(e * SLOT_W + g, XE_ALIGN), XE_ALIGN), :]

    @pl.when(i == 0)
    def _():
        carry[...] = jnp.zeros(carry.shape, _BF)
        stage_extra[...] = jnp.zeros(stage_extra.shape, _BF)
        zero_copies = [pltpu.make_async_copy(stage_extra.at[pl.ds(0, zero_rows)],
                                             xe_ref.at[e, pl.ds(zero_from, zero_rows)], sem_extra)
                       for e in range(N_EXPERTS)]
        for cp in zero_copies:
            cp.start()
        for cp in zero_copies:
            cp.wait()

    compact(0, stage.at[s])
    for e in range(N_EXPERTS):
        stage[s, e * SLOT_W:e * SLOT_W + XE_ALIGN, :] += carry[e]
    save_carry(0, stage.at[s])

    @pl.when(i > 0)
    def _():
        for e in range(N_EXPERTS):
            copy(e, stage.at[1 - s], 0, sem.at[1 - s]).wait()

    for e in range(N_EXPERTS):
        copy(e, stage.at[s], 0, sem.at[s]).start()

    def extra_pass(k, c):
        compact(k, stage_extra)
        save_carry(k, stage_extra)
        for wait in (False, True):
            for e in range(N_EXPERTS):
                @pl.when(lo_ref[(i + 1) * N_EXPERTS + e] - base(e) >= k * SLOT_W)
                def _():
                    cp = copy(e, stage_extra, k, sem_extra)
                    cp.wait() if wait else cp.start()
        return c
    lax.fori_loop(1, _tile_max(lo_ref, i, base) // SLOT_W + 1, extra_pass, 0)

    @pl.when(i == pl.num_programs(0) - 1)
    def _():
        for e in range(N_EXPERTS):
            copy(e, stage.at[s], 0, sem.at[s]).wait()


def _dispatch(lo, hn, slot, cap, cap_pad):
    n_tiles = slot.shape[0]
    zero_from = cap // XE_ALIGN * XE_ALIGN
    zero_rows = cap_pad + SLOT_W - zero_from
    assert zero_rows <= N_EXPERTS * SLOT_W and SLOT_W % XE_ALIGN == 0 and cap_pad % XE_ALIGN == 0
    return pl.pallas_call(
        functools.partial(_dispatch_kernel, zero_from=zero_from, zero_rows=zero_rows),
        grid_spec=pltpu.PrefetchScalarGridSpec(
            num_scalar_prefetch=1,
            grid=(n_tiles,),
            in_specs=[pl.BlockSpec((ROW_TILE, D_MODEL), lambda i, lo: (i, 0)),
                      pl.BlockSpec((None, N_EXPERTS, ROW_TILE), lambda i, lo: (i, 0, 0))],
            out_specs=pl.BlockSpec(memory_space=pl.ANY),
            scratch_shapes=[pltpu.VMEM((2, N_EXPERTS * SLOT_W, D_MODEL), _BF),
                            pltpu.VMEM((N_EXPERTS * SLOT_W, D_MODEL), _BF),
                            pltpu.VMEM((N_EXPERTS, XE_ALIGN, D_MODEL), _BF),
                            pltpu.SemaphoreType.DMA((2,)), pltpu.SemaphoreType.DMA(())]),
        out_shape=jax.ShapeDtypeStruct((N_EXPERTS, cap_pad + SLOT_W, D_MODEL), _BF),
        compiler_params=pltpu.CompilerParams(dimension_semantics=("arbitrary",), vmem_limit_bytes=VMEM_LIMIT),
        name="dispatch",
    )(lo, hn, slot)


def _expert_kernel(x_ref, wg_ref, wu_ref, wd_ref, y_ref):
    x = x_ref[...]
    g = _dot(x, wg_ref[...])
    u = _dot(x, wu_ref[...])
    hid = (g * jax.nn.sigmoid(g) * u).astype(_BF)
    y_ref[...] = _dot(hid, wd_ref[...]).astype(_BF)


def _experts(xe, wg, wu, wd, cap_pad, tile):
    n_e = xe.shape[0]
    wspec = lambda a: pl.BlockSpec((None,) + a.shape[1:], lambda e, t: (e, 0, 0))
    return pl.pallas_call(
        _expert_kernel,
        grid=(n_e, cap_pad // tile),
        in_specs=[pl.BlockSpec((None, tile, D_MODEL), lambda e, t: (e, t, 0)), wspec(wg), wspec(wu), wspec(wd)],
        out_specs=pl.BlockSpec((None, tile, D_MODEL), lambda e, t: (e, t, 0)),
        out_shape=jax.ShapeDtypeStruct((n_e, cap_pad, D_MODEL), _BF),
        compiler_params=pltpu.CompilerParams(dimension_semantics=("parallel", "arbitrary"),
                                             vmem_limit_bytes=VMEM_LIMIT),
        name="experts",
    )(xe, wg, wu, wd)


def _combine_kernel(lo_ref, h_ref, slot_ref, aff_ref, g_ref, ye_ref, o_ref, win, win_extra, f_acc, sem, sem_extra, *,
                    cap_pad):
    i = pl.program_id(0)
    s = i % 2
    first = lambda t, e: (lo_ref[t * N_EXPERTS + e] // 16) * 16

    def start_row(t, e, k):
        return pl.multiple_of(jnp.minimum(first(t, e) + k * WIN, cap_pad - WIN), 16)

    def window_copies(t, k, buf, sm):
        return [pltpu.make_async_copy(ye_ref.at[e, pl.ds(start_row(t, e, k), WIN)], buf.at[pl.ds(e * WIN, WIN)], sm)
                for e in range(N_EXPERTS)]

    def weights(k):
        w = lax.broadcasted_iota(jnp.int32, (WIN, ROW_TILE), 0)
        parts = []
        for e in range(N_EXPERTS):
            slot = slot_ref[e:e + 1, :]
            hit = (slot == w + start_row(i, e, k)) & (slot >= first(i, e) + k * WIN)
            parts.append(jnp.where(hit, aff_ref[e:e + 1, :], 0.0).astype(_BF))
        return jnp.concatenate(parts, axis=0)

    scatter = lambda wts, rows: lax.dot_general(wts, rows, (((0,), (0,)), ((), ())), preferred_element_type=_F32)

    @pl.when(i == 0)
    def _():
        for cp in window_copies(0, 0, win.at[0], sem.at[0]):
            cp.start()

    @pl.when(i + 1 < pl.num_programs(0))
    def _():
        for cp in window_copies(i + 1, 0, win.at[1 - s], sem.at[1 - s]):
            cp.start()

    wts = weights(0)
    for cp in window_copies(i, 0, win.at[s], sem.at[s]):
        cp.wait()
    f_acc[...] = scatter(wts, win[s])

    def extra_pass(k, c):
        copies = window_copies(i, k, win_extra, sem_extra)
        for cp in copies:
            cp.start()
        wts_k = weights(k)
        for cp in copies:
            cp.wait()
        f_acc[...] += scatter(wts_k, win_extra[...])
        return c
    reach = _tile_max(lo_ref, i, lambda e: first(i, e))
    lax.fori_loop(1, (reach + WIN - 1) // WIN, extra_pass, 0)
    o_ref[...] = _rms(h_ref[...] + f_acc[...], g_ref[...])


def _combine(lo, h, slot, aff, g_final, ye, n_rows):
    cap_pad = ye.shape[1]
    assert cap_pad % 16 == 0 and WIN % 16 == 0 and cap_pad >= WIN
    tile3 = pl.BlockSpec((None, N_EXPERTS, ROW_TILE), lambda i, lo: (i, 0, 0))
    return pl.pallas_call(
        functools.partial(_combine_kernel, cap_pad=cap_pad),
        grid_spec=pltpu.PrefetchScalarGridSpec(
            num_scalar_prefetch=1,
            grid=(n_rows // ROW_TILE,),
            in_specs=[pl.BlockSpec((ROW_TILE, D_MODEL), lambda i, lo: (i, 0)), tile3, tile3,
                      pl.BlockSpec(g_final.shape, lambda i, lo: (0, 0)), pl.BlockSpec(memory_space=pl.ANY)],
            out_specs=pl.BlockSpec((ROW_TILE, D_MODEL), lambda i, lo: (i, 0)),
            scratch_shapes=[pltpu.VMEM((2, N_EXPERTS * WIN, D_MODEL), _BF), pltpu.VMEM((N_EXPERTS * WIN, D_MODEL), _BF),
                            pltpu.VMEM((ROW_TILE, D_MODEL), _F32),
                            pltpu.SemaphoreType.DMA((2,)), pltpu.SemaphoreType.DMA(())]),
        out_shape=jax.ShapeDtypeStruct((n_rows, D_MODEL), _F32),
        compiler_params=pltpu.CompilerParams(dimension_semantics=("arbitrary",), vmem_limit_bytes=VMEM_LIMIT),
        name="combine_norm",
    )(lo, h, slot, aff, g_final, ye)


def _rotate_half_cols(w):
    half = QK_ROPE // 2
    return jnp.concatenate([-w[..., half:], w[..., :half]], axis=-1)


def _prepare_weights(w_in, g_attn, g_q, w_uq, g_kv, w_ukv, w_o, g_ffn, w_router, w_gate, w_up, w_down, na_rpb):
    na_scale = 1.0 / math.sqrt(NA_HEAD_DIM)
    c0, c1, c2 = 3 * NA_WIDTH, 3 * NA_WIDTH + Q_LORA, 3 * NA_WIDTH + Q_LORA + KV_LORA
    w_kr = w_in[:, c2:]
    zpad = lambda n: jnp.zeros((D_MODEL, n), _F32)
    slot = lambda w: jnp.concatenate([zpad(QK_NOPE), w, zpad(LANES - QK_NOPE - QK_ROPE)], axis=1)
    w1 = jnp.concatenate([w_in[:, NA_WIDTH:2 * NA_WIDTH], w_in[:, c0:c2], slot(w_kr),
                          slot(_rotate_half_cols(w_kr))], axis=1).astype(_BF)
    q_scale = LOG2E / math.sqrt(QK_NOPE + QK_ROPE)
    wq = (w_uq * q_scale).reshape(Q_LORA, MLA_HEADS, QK_NOPE + QK_ROPE)
    pad = LANES - QK_NOPE - QK_ROPE
    plain = jnp.pad(wq, ((0, 0), (0, 0), (0, pad)))
    rot = jnp.pad(_rotate_half_cols(wq[..., QK_NOPE:]), ((0, 0), (0, 0), (QK_NOPE, pad)))
    w2t = jnp.concatenate([plain.reshape(Q_LORA, -1), rot.reshape(Q_LORA, -1)], axis=1).T.astype(_BF)
    wkv = w_ukv.reshape(KV_LORA, MLA_HEADS, QK_NOPE + V_HEAD)
    w3k = jnp.pad(wkv[..., :QK_NOPE], ((0, 0), (0, 0), (0, LANES - QK_NOPE))).reshape(KV_LORA, -1).astype(_BF)
    w3vt = wkv[..., QK_NOPE:].reshape(KV_LORA, -1).T.astype(_BF)
    wr = w_router.T
    wr_hi = wr.astype(_BF)
    wr_lo = (wr - wr_hi.astype(_F32)).astype(_BF)
    return dict(
        w1=w1, w2t=w2t, w3k=w3k, w3vt=w3vt,
        g_attn=g_attn[None], g_q=g_q[None], g_kv=g_kv[None], g_ffn=g_ffn[None],
        wo_na=w_o[:NA_WIDTH].astype(_BF), wo_mla=w_o[NA_WIDTH:].astype(_BF),
        wr_hi=wr_hi, wr_lo=wr_lo,
        wg=w_gate.astype(_BF), wu=w_up.astype(_BF), wd=w_down.astype(_BF),
        wqvt=jnp.concatenate([w_in[:, :NA_WIDTH] * (na_scale * LOG2E), w_in[:, 2 * NA_WIDTH:3 * NA_WIDTH]],
                             axis=1).T.astype(_BF),
        na_bias_t=_na_bias_table(na_rpb),
    )


def _na_bias_table(rpb):
    ql = np.arange(NA_BLK)
    kl = np.arange(3 * NA_BLK)
    ri, qc = ql // GRID_W, ql % GRID_W
    kj, kc = kl // GRID_W, kl % GRID_W
    qcs = np.clip(qc - WIN_C // 2, 0, GRID_W - WIN_C)
    col_ok = (kc[None, :] >= qcs[:, None]) & (kc[None, :] < qcs[:, None] + WIN_C)
    lo = np.stack([np.full(NA_BLK, NA_ROWS), ri, np.zeros(NA_BLK, np.int64)])
    row_ok = (kj[None, None, :] >= lo[:, :, None]) & (kj[None, None, :] < lo[:, :, None] + WIN_R)
    ok = row_ok & col_ok[None]
    rows_q, rows_k, cols = np.arange(NA_ROWS), np.arange(3 * NA_ROWS), np.arange(GRID_W)
    dr = np.clip(rows_k[None, :] - rows_q[:, None] + (WIN_R - 1) - NA_ROWS, 0, 2 * WIN_R - 2)
    dc = np.clip(cols[None, :] - cols[:, None] + (WIN_C - 1), 0, 2 * WIN_C - 2)
    pick_r = (dr[..., None] == np.arange(2 * WIN_R - 1)).astype(np.float32)
    pick_c = (dc[..., None] == np.arange(2 * WIN_C - 1)).astype(np.float32)
    vals = jnp.einsum("hrc,ijr,qkc->hiqjk", rpb.astype(_F32), pick_r, pick_c, precision=lax.Precision.HIGHEST)
    vals = vals.reshape(NA_HEADS, NA_BLK, 3 * NA_BLK)
    return jnp.where(jnp.asarray(ok)[:, None], vals[None] * LOG2E, NEG).transpose(0, 1, 3, 2)


def _rope_tables(batch, seq, np_rows):
    half = QK_ROPE // 2
    inv_freq = ROPE_THETA ** (-jnp.arange(half, dtype=_F32) * 2.0 / QK_ROPE)
    pos_grid = jnp.tile(jnp.arange(seq, dtype=_F32) + N_META, batch)
    pos_meta = jnp.tile(jnp.arange(N_META, dtype=_F32), batch)
    pos = jnp.concatenate([pos_grid, pos_meta, jnp.zeros(np_rows - batch * (seq + N_META), _F32)])
    ang = pos[:, None] * inv_freq[None, :]
    cos, sin = jnp.cos(ang), jnp.sin(ang)
    pad = LANES - QK_NOPE - QK_ROPE
    cos_t = jnp.concatenate([jnp.ones((np_rows, QK_NOPE), _F32), cos, cos, jnp.zeros((np_rows, pad), _F32)], axis=1)
    sin_t = jnp.concatenate([jnp.zeros((np_rows, QK_NOPE), _F32), sin, sin, jnp.zeros((np_rows, pad), _F32)], axis=1)
    return cos_t, sin_t


def _expert_tile(cap):
    if cap <= 1024:
        return -(-cap // 16) * 16
    return min(range(512, 1025, 16), key=lambda t: (-(-cap // t) * t - cap, -t))


def _run_group(x, meta_tokens, prm, g_final, tq, tk):
    batch, seq, _ = x.shape
    n_grid = batch * seq
    n_valid = n_grid + batch * N_META
    np_rows = n_grid + ROW_TILE
    assert seq % NA_BLK == 0 and seq // GRID_W >= WIN_R and batch * N_META <= ROW_TILE
    meta = jnp.broadcast_to(meta_tokens[None], (batch, N_META, D_MODEL)).reshape(batch * N_META, D_MODEL)
    xg = x.reshape(n_grid, D_MODEL)
    xm = jnp.pad(meta, ((0, ROW_TILE - batch * N_META), (0, 0)))
    cos_r, sin_r = _rope_tables(batch, seq, np_rows)
    kna, qt, k3, vt, qnat, vnat = _proj(xg, xm, cos_r, sin_r, cos_r.T, sin_r.T, prm["g_attn"], prm["g_q"], prm["g_kv"],
                                        prm["w1"], prm["w2t"], prm["w3k"], prm["w3vt"], prm["wqvt"])
    n_meta = batch * N_META
    meta_cols = slice(n_grid, n_grid + n_meta)
    vmt_na = vnat[:, :, meta_cols].reshape(NA_HEADS, V_EXT, batch, N_META).transpose(2, 0, 1, 3)
    o_na_g = _na_attention(qnat, kna, vnat, vmt_na, prm["na_bias_t"], batch, seq).reshape(NA_WIDTH, n_grid)
    q_own = jnp.stack([qnat[h, (h % 2) * NA_HEAD_DIM:(h % 2 + 1) * NA_HEAD_DIM, meta_cols] for h in range(NA_HEADS)])
    rows_of = lambda t: t.transpose(2, 0, 1).reshape(n_meta, NA_WIDTH)
    o_na_m = _na_meta_attention(rows_of(q_own), kna[meta_cols], rows_of(vnat[:, :NA_HEAD_DIM, meta_cols]), batch)
    km = k3[:, n_grid:n_grid + n_meta].reshape(MLA_HEADS, batch, N_META, LANES).transpose(1, 0, 2, 3)
    vmt = vt[:, :, n_grid:n_grid + n_meta].reshape(MLA_HEADS, V_EXT, batch, N_META).transpose(2, 0, 1, 3)
    ot_g = _mla_attention(qt, k3, vt, km, vmt, batch, seq, seq, 0, seq // tq, tq, tk)
    ot_m = _mla_attention(qt, k3, vt, km, vmt, batch, seq, ROW_TILE, n_grid // ROW_TILE, 0, ROW_TILE, tk)
    ot_m = ot_m.reshape(MLA_WIDTH, batch, ROW_TILE)
    ot_m = jnp.concatenate([ot_m[:, b, b * N_META:(b + 1) * N_META] for b in range(batch)], axis=1)
    ot_m = jnp.pad(ot_m, ((0, 0), (0, ROW_TILE - n_meta)))
    o_na_m = jnp.pad(o_na_m.T, ((0, 0), (0, ROW_TILE - n_meta)))
    h, hn, aff = _outproj(xg, xm, o_na_g, o_na_m, ot_g.reshape(MLA_WIDTH, n_grid), ot_m, prm["wo_na"], prm["wo_mla"],
                          prm["g_ffn"], prm["wr_hi"], prm["wr_lo"], n_valid)
    cap = EC_CAPACITY * n_valid // N_EXPERTS
    tile = _expert_tile(cap)
    cap_pad = -(-cap // tile) * tile
    slot, lo3 = _route(aff, cap)
    lo = jnp.concatenate([lo3[:, :, 0], jnp.full((1, N_EXPERTS), cap, jnp.int32)]).reshape(-1)
    xe = _dispatch(lo, hn, slot, cap, cap_pad)
    ye = _experts(xe, prm["wg"], prm["wu"], prm["wd"], cap_pad, tile)
    return _combine(lo, h, slot, aff, g_final[None], ye, n_grid).reshape(batch, seq, D_MODEL)


def kernel(x_prompt, x_sample, meta_tokens, g_attn, w_in, na_rpb, g_q, w_uq, g_kv, w_ukv, w_o, g_ffn, w_router,
           w_gate, w_up, w_down, g_final):
    prm = _prepare_weights(w_in[0], g_attn[0], g_q[0], w_uq[0], g_kv[0], w_ukv[0], w_o[0], g_ffn[0], w_router[0],
                           w_gate[0], w_up[0], w_down[0], na_rpb[0])
    y_prompt = _run_group(x_prompt, meta_tokens, prm, g_final, MLA_TQ, MLA_TK)
    y_sample = _run_group(x_sample, meta_tokens, prm, g_final, MLA_TQ, MLA_TK)
    return (y_prompt, y_sample)
```

```python
import functools
import math

import jax
import jax.numpy as jnp
import numpy as np
from jax import lax
from jax.experimental import pallas as pl
from jax.experimental.pallas import tpu as pltpu

D_MODEL = 1024
GRID_W = 64
N_META = 16
WIN_R = 8
WIN_C = 16
NA_HEADS = 8
NA_HEAD_DIM = 64
MLA_HEADS = 8
QK_NOPE = 64
QK_ROPE = 32
V_HEAD = 64
Q_LORA = 768
KV_LORA = 256
ROPE_THETA = 10000.0
N_EXPERTS = 16
EC_CAPACITY = 2
D_EXPERT = 2048
NORM_EPS = 1e-6
NA_WIDTH = NA_HEADS * NA_HEAD_DIM
MLA_WIDTH = MLA_HEADS * V_HEAD

LANES = 128
ROW_TILE = 256
OUT_SUB = 128
NA_ROWS = 4
NA_BLK = NA_ROWS * GRID_W
NA_LOOKAHEAD = 4
NEG = -1e30
LOG2E = 1.4426950408889634
VMEM_LIMIT = 56 * 1024 * 1024

_BF = jnp.bfloat16
_F32 = jnp.float32


def _dot(a, b):
    return jnp.dot(a, b, preferred_element_type=_F32)


def _dot_nt(a, b):
    return lax.dot_general(a, b, (((1,), (1,)), ((), ())), preferred_element_type=_F32)


def _rms(x, g):
    return x * lax.rsqrt(jnp.mean(x * x, axis=-1, keepdims=True) + NORM_EPS) * g


def _proj_kernel(x_ref, xm_ref, cos_ref, sin_ref, cost_ref, sint_ref, g_attn_ref, g_q_ref, g_kv_ref, w1_ref, w2t_ref,
                 w3k_ref, w3vt_ref, wqvt_ref, kna_ref, qt_ref, k_ref, vt_ref, qnat_ref, vnat_ref):
    x = jnp.where(pl.program_id(0) == pl.num_programs(0) - 1, xm_ref[...], x_ref[...])
    a = _rms(x, g_attn_ref[...]).astype(_BF)
    p = _dot(a, w1_ref[...])
    kna_ref[...] = p[:, :NA_WIDTH].astype(_BF)
    o = NA_WIDTH
    cqn = _rms(p[:, o:o + Q_LORA], g_q_ref[...]).astype(_BF)
    o += Q_LORA
    ckvn = _rms(p[:, o:o + KV_LORA], g_kv_ref[...]).astype(_BF)
    o += KV_LORA
    kr = p[:, o:o + LANES] * cos_ref[...] + p[:, o + LANES:o + 2 * LANES] * sin_ref[...]
    k3 = _dot(ckvn, w3k_ref[...])
    q2t = _dot_nt(w2t_ref[...], cqn)
    vt = _dot_nt(w3vt_ref[...], ckvn)
    cos_t = cost_ref[...]
    sin_t = sint_ref[...]
    hw = MLA_HEADS * LANES
    ones_row = jnp.where(lax.broadcasted_iota(jnp.int32, (V_EXT - V_HEAD, x_ref.shape[0]), 0) == 0, 1.0, 0.0)
    for h in range(MLA_HEADS):
        sl = slice(h * LANES, (h + 1) * LANES)
        qt_ref[h] = (q2t[sl] * cos_t + q2t[hw + h * LANES:hw + (h + 1) * LANES] * sin_t).astype(_BF)
        k_ref[h] = (k3[:, sl] + kr).astype(_BF)
        vt_ref[h, :V_HEAD, :] = vt[h * V_HEAD:(h + 1) * V_HEAD].astype(_BF)
        vt_ref[h, V_HEAD:, :] = ones_row.astype(_BF)
    qvt = _dot_nt(wqvt_ref[...], a)
    zero_half = jnp.zeros((NA_HEAD_DIM, x_ref.shape[0]), _BF)
    for h in range(NA_HEADS):
        own = slice((h % 2) * NA_HEAD_DIM, (h % 2 + 1) * NA_HEAD_DIM)
        other = slice((1 - h % 2) * NA_HEAD_DIM, (2 - h % 2) * NA_HEAD_DIM)
        qnat_ref[h, own, :] = qvt[h * NA_HEAD_DIM:(h + 1) * NA_HEAD_DIM].astype(_BF)
        qnat_ref[h, other, :] = zero_half
        vnat_ref[h, :NA_HEAD_DIM, :] = qvt[NA_WIDTH + h * NA_HEAD_DIM:NA_WIDTH + (h + 1) * NA_HEAD_DIM].astype(_BF)
        vnat_ref[h, NA_HEAD_DIM:, :] = ones_row.astype(_BF)


def _proj(x, x_meta, cos, sin, cos_t, sin_t, g_attn, g_q, g_kv, w1, w2t, w3k, w3vt, wqvt):
    n_grid_tiles = x.shape[0] // ROW_TILE
    np_rows = x.shape[0] + ROW_TILE
    row = lambda i: (i, 0)
    col = lambda i: (0, i)
    full = lambda i: (0, 0)
    w = lambda a: pl.BlockSpec(a.shape, full)
    return pl.pallas_call(
        _proj_kernel,
        grid=(np_rows // ROW_TILE,),
        in_specs=[pl.BlockSpec((ROW_TILE, D_MODEL), lambda i: (jnp.minimum(i, n_grid_tiles - 1), 0)), w(x_meta),
                  pl.BlockSpec((ROW_TILE, LANES), row),
                  pl.BlockSpec((ROW_TILE, LANES), row), pl.BlockSpec((LANES, ROW_TILE), col),
                  pl.BlockSpec((LANES, ROW_TILE), col), w(g_attn), w(g_q), w(g_kv), w(w1), w(w2t), w(w3k), w(w3vt),
                  w(wqvt)],
        out_specs=[pl.BlockSpec((ROW_TILE, NA_WIDTH), row),
                   pl.BlockSpec((MLA_HEADS, LANES, ROW_TILE), lambda i: (0, 0, i)),
                   pl.BlockSpec((MLA_HEADS, ROW_TILE, LANES), lambda i: (0, i, 0)),
                   pl.BlockSpec((MLA_HEADS, V_EXT, ROW_TILE), lambda i: (0, 0, i)),
                   pl.BlockSpec((NA_HEADS, LANES, ROW_TILE), lambda i: (0, 0, i)),
                   pl.BlockSpec((NA_HEADS, V_EXT, ROW_TILE), lambda i: (0, 0, i))],
        out_shape=[jax.ShapeDtypeStruct((np_rows, NA_WIDTH), _BF),
                   jax.ShapeDtypeStruct((MLA_HEADS, LANES, np_rows), _BF),
                   jax.ShapeDtypeStruct((MLA_HEADS, np_rows, LANES), _BF),
                   jax.ShapeDtypeStruct((MLA_HEADS, V_EXT, np_rows), _BF),
                   jax.ShapeDtypeStruct((NA_HEADS, LANES, np_rows), _BF),
                   jax.ShapeDtypeStruct((NA_HEADS, V_EXT, np_rows), _BF)],
        compiler_params=pltpu.CompilerParams(dimension_semantics=("parallel",), vmem_limit_bytes=VMEM_LIMIT),
        name="proj",
    )(x, x_meta, cos, sin, cos_t, sin_t, g_attn, g_q, g_kv, w1, w2t, w3k, w3vt, wqvt)


def _na_kernel(qt_ref, kp_ref, kc_ref, kn_ref, vtp_ref, vtc_ref, vtn_ref, km_ref, vmt_ref, bias_ref, ot_ref):
    ks = (kp_ref, kc_ref, kn_ref)
    vts = (vtp_ref, vtc_ref, vtn_ref)
    steps = [(h, c) for h in range(NA_HEADS) for c in range(3)]
    lanes = lambda h: slice((h // 2) * LANES, (h // 2 + 1) * LANES)
    scores = lambda h, c: _dot(ks[c][:, lanes(h)], qt_ref[h])
    meta_s = [_dot(km_ref[:, lanes(h)], qt_ref[h]) for h in range(NA_HEADS)]
    meta_m = [jnp.max(sm, axis=0, keepdims=True) for sm in meta_s]
    meta_acc = [_dot(vmt_ref[h], jnp.exp2(meta_s[h] - meta_m[h]).astype(_BF)) for h in range(NA_HEADS)]
    pending = [scores(*st) for st in steps[:NA_LOOKAHEAD]]
    for n, (h, c) in enumerate(steps):
        if n + NA_LOOKAHEAD < len(steps):
            pending.append(scores(*steps[n + NA_LOOKAHEAD]))
        s = bias_ref[h, c * NA_BLK:(c + 1) * NA_BLK, :] + pending.pop(0)
        if c == 0:
            m, acc = meta_m[h], meta_acc[h]
        m_new = jnp.maximum(m, jnp.max(s, axis=0, keepdims=True))
        p = jnp.exp2(s - m_new).astype(_BF)
        acc = acc * jnp.exp2(m - m_new) + _dot(vts[c][h], p)
        m = m_new
        if c == 2:
            ot_ref[h] = (acc[:NA_HEAD_DIM] / acc[NA_HEAD_DIM:NA_HEAD_DIM + 1]).astype(_BF)


def _na_attention(qt, k, vt, vmt, bias_t, batch, seq):
    nblk = seq // NA_BLK
    meta_blk0 = batch * seq // N_META
    near = lambda b, j, off: b * nblk + jnp.clip(j + off, 0, nblk - 1)
    k_spec = lambda off: pl.BlockSpec((NA_BLK, NA_WIDTH), lambda b, j: (near(b, j, off), 0))
    vt_spec = lambda off: pl.BlockSpec((NA_HEADS, V_EXT, NA_BLK), lambda b, j: (0, 0, near(b, j, off)))
    cls = lambda b, j: (jnp.where(j == 0, 0, jnp.where(j == nblk - 1, 2, 1)), 0, 0, 0)
    return pl.pallas_call(
        _na_kernel,
        grid=(batch, nblk),
        in_specs=[pl.BlockSpec((NA_HEADS, LANES, NA_BLK), lambda b, j: (0, 0, b * nblk + j)),
                  k_spec(-1), k_spec(0), k_spec(1), vt_spec(-1), vt_spec(0), vt_spec(1),
                  pl.BlockSpec((N_META, NA_WIDTH), lambda b, j: (meta_blk0 + b, 0)),
                  pl.BlockSpec((None, NA_HEADS, V_EXT, N_META), lambda b, j: (b, 0, 0, 0)),
                  pl.BlockSpec((None, NA_HEADS, 3 * NA_BLK, NA_BLK), cls)],
        out_specs=pl.BlockSpec((NA_HEADS, NA_HEAD_DIM, NA_BLK), lambda b, j: (0, 0, b * nblk + j)),
        out_shape=jax.ShapeDtypeStruct((NA_HEADS, NA_HEAD_DIM, batch * seq), _BF),
        compiler_params=pltpu.CompilerParams(dimension_semantics=("parallel", "arbitrary"),
                                             vmem_limit_bytes=VMEM_LIMIT),
        name="na_attn",
    )(qt, k, k, k, vt, vt, vt, k, vmt, bias_t)


def _na_meta_kernel(q_ref, k_ref, v_ref, o_ref):
    for h in range(NA_HEADS):
        sl = slice(h * NA_HEAD_DIM, (h + 1) * NA_HEAD_DIM)
        s = _dot_nt(q_ref[:, sl], k_ref[:, sl])
        p = jnp.exp2(s - jnp.max(s, axis=1, keepdims=True))
        l = jnp.sum(p, axis=1, keepdims=True)
        o_ref[:, sl] = (_dot(p.astype(_BF), v_ref[:, sl]) / l).astype(_BF)


def _na_meta_attention(q, k, v, batch):
    spec = pl.BlockSpec((N_META, NA_WIDTH), lambda b: (b, 0))
    return pl.pallas_call(
        _na_meta_kernel,
        grid=(batch,),
        in_specs=[spec, spec, spec],
        out_specs=spec,
        out_shape=jax.ShapeDtypeStruct((batch * N_META, NA_WIDTH), _BF),
        compiler_params=pltpu.CompilerParams(dimension_semantics=("parallel",)),
        name="na_meta_attn",
    )(q, k, v)


V_EXT = V_HEAD + 16
MLA_CHUNK = 256
MLA_LOOKAHEAD = 6
MLA_QSUB = 256
MLA_TQ = 512
MLA_TK = 4096


def _mla_kernel(qt_ref, k_ref, vt_ref, km_ref, vmt_ref, ot_ref, m_sc, acc_sc, *, chunk):
    j = pl.program_id(2)
    tq = qt_ref.shape[2]
    nchunks = k_ref.shape[1] // chunk

    @pl.when(j == 0)
    def _():
        def init_head(h, carry):
            s = _dot(km_ref[h], qt_ref[h])
            m = jnp.max(s, axis=0, keepdims=True)
            p = jnp.exp2(s - m).astype(_BF)
            acc_sc[h] = _dot(vmt_ref[h], p)
            m_sc[h] = jnp.broadcast_to(m, (8, tq))
            return carry
        lax.fori_loop(0, MLA_HEADS, init_head, 0)

    qsub = min(MLA_QSUB, tq)
    steps = [(qs, h, c) for qs in range(tq // qsub) for h in range(MLA_HEADS) for c in range(nchunks)]
    lanes = lambda qs: slice(qs * qsub, (qs + 1) * qsub)
    scores = lambda qs, h, c: _dot(k_ref[h, c * chunk:(c + 1) * chunk, :], qt_ref[h, :, lanes(qs)])
    pending = [scores(*st) for st in steps[:MLA_LOOKAHEAD]]
    for n, (qs, h, c) in enumerate(steps):
        if n + MLA_LOOKAHEAD < len(steps):
            pending.append(scores(*steps[n + MLA_LOOKAHEAD]))
        s = pending.pop(0)
        if c == 0:
            m = m_sc[h, :1, lanes(qs)]
            acc = acc_sc[h, :, lanes(qs)]
        m_new = jnp.maximum(m, jnp.max(s, axis=0, keepdims=True))
        p = jnp.exp2(s - m_new).astype(_BF)
        acc = acc * jnp.exp2(m - m_new) + _dot(vt_ref[h, :, c * chunk:(c + 1) * chunk], p)
        m = m_new
        if c == nchunks - 1:
            m_sc[h, :, lanes(qs)] = jnp.broadcast_to(m, (8, qsub))
            acc_sc[h, :, lanes(qs)] = acc

    @pl.when(j == pl.num_programs(2) - 1)
    def _():
        for h in range(MLA_HEADS):
            acc = acc_sc[h]
            ot_ref[h] = (acc[:V_HEAD] / acc[V_HEAD:V_HEAD + 1]).astype(_BF)


def _mla_attention(qt, k, vt, km, vmt, batch, seq, q_rows, q_blk0, q_stride, tq, tk):
    assert q_rows % tq == 0 and seq % tk == 0 and tk % MLA_CHUNK == 0
    nq = q_rows // tq
    nk = seq // tk
    return pl.pallas_call(
        functools.partial(_mla_kernel, chunk=min(MLA_CHUNK, tk)),
        grid=(batch, nq, nk),
        in_specs=[pl.BlockSpec((MLA_HEADS, LANES, tq), lambda b, i, j: (0, 0, q_blk0 + b * q_stride + i)),
                  pl.BlockSpec((MLA_HEADS, tk, LANES), lambda b, i, j: (0, b * nk + j, 0)),
                  pl.BlockSpec((MLA_HEADS, V_EXT, tk), lambda b, i, j: (0, 0, b * nk + j)),
                  pl.BlockSpec((None, MLA_HEADS, N_META, LANES), lambda b, i, j: (b, 0, 0, 0)),
                  pl.BlockSpec((None, MLA_HEADS, V_EXT, N_META), lambda b, i, j: (b, 0, 0, 0))],
        out_specs=pl.BlockSpec((MLA_HEADS, V_HEAD, tq), lambda b, i, j: (0, 0, b * nq + i)),
        out_shape=jax.ShapeDtypeStruct((MLA_HEADS, V_HEAD, batch * q_rows), _BF),
        scratch_shapes=[pltpu.VMEM((MLA_HEADS, 8, tq), _F32), pltpu.VMEM((MLA_HEADS, V_EXT, tq), _F32)],
        compiler_params=pltpu.CompilerParams(dimension_semantics=("parallel", "parallel", "arbitrary"),
                                             vmem_limit_bytes=VMEM_LIMIT),
        name="mla_attn",
    )(qt, k, vt, km, vmt)


def _outproj_kernel(x_ref, xm_ref, ong_ref, onm_ref, omg_ref, omm_ref, wo_na_ref, wo_mla_ref, g_ref, wr_hi_ref,
                    wr_lo_ref, h_ref, hn_ref, aff_ref, *, n_grid_tiles, n_valid):
    i = pl.program_id(0)
    is_meta = i == n_grid_tiles
    tn = lambda a_t, b: lax.dot_general(a_t, b, (((0,), (0,)), ((), ())), preferred_element_type=_F32)
    wr_hi = wr_hi_ref[...]
    subs = [slice(r0, r0 + OUT_SUB) for r0 in range(0, ROW_TILE, OUT_SUB)]
    hs = []
    for rows in subs:
        o_na_t = jnp.where(is_meta, onm_ref[:, rows], ong_ref[:, rows])
        o_mla_t = jnp.where(is_meta, omm_ref[:, rows], omg_ref[:, rows])
        x = jnp.where(is_meta, xm_ref[rows, :], x_ref[rows, :])
        hs.append(x + tn(o_na_t, wo_na_ref[...]) + tn(o_mla_t, wo_mla_ref[...]))
    for rows, h in zip(subs, hs):
        r0 = rows.start
        h_ref[rows, :] = h
        hn = _rms(h, g_ref[...])
        hn_hi = hn.astype(_BF)
        hn_ref[rows, :] = hn_hi
        hn_lo = (hn - hn_hi.astype(_F32)).astype(_BF)
        logits = _dot_nt(wr_hi, hn_hi) + _dot_nt(wr_hi, hn_lo) + _dot_nt(wr_lo_ref[...], hn_hi)
        e = jnp.exp(logits - jnp.max(logits, axis=0, keepdims=True))
        aff = e / jnp.sum(e, axis=0, keepdims=True)
        row = i * ROW_TILE + r0 + lax.broadcasted_iota(jnp.int32, aff.shape, 1)
        aff_ref[:, rows] = jnp.where(row < n_valid, aff, -1.0)


def _outproj(x, x_meta, o_na_g, o_na_m, o_mla_g, o_mla_m, wo_na, wo_mla, g_ffn, wr_hi, wr_lo, n_valid):
    n_grid_tiles = x.shape[0] // ROW_TILE
    n_tiles = n_grid_tiles + 1
    np_rows = n_tiles * ROW_TILE
    row = lambda i: (i, 0)
    grid_col = lambda i: (0, jnp.minimum(i, n_grid_tiles - 1))
    full = lambda i: (0, 0)
    w = lambda a: pl.BlockSpec(a.shape, full)
    return pl.pallas_call(
        functools.partial(_outproj_kernel, n_grid_tiles=n_grid_tiles, n_valid=n_valid),
        grid=(n_tiles,),
        in_specs=[pl.BlockSpec((ROW_TILE, D_MODEL), lambda i: (jnp.minimum(i, n_grid_tiles - 1), 0)), w(x_meta),
                  pl.BlockSpec((NA_WIDTH, ROW_TILE), grid_col), w(o_na_m),
                  pl.BlockSpec((MLA_WIDTH, ROW_TILE), grid_col), w(o_mla_m),
                  w(wo_na), w(wo_mla), w(g_ffn), w(wr_hi), w(wr_lo)],
        out_specs=[pl.BlockSpec((ROW_TILE, D_MODEL), row), pl.BlockSpec((ROW_TILE, D_MODEL), row),
                   pl.BlockSpec((None, N_EXPERTS, ROW_TILE), lambda i: (i, 0, 0))],
        out_shape=[jax.ShapeDtypeStruct((np_rows, D_MODEL), _F32), jax.ShapeDtypeStruct((np_rows, D_MODEL), _BF),
                   jax.ShapeDtypeStruct((n_tiles, N_EXPERTS, ROW_TILE), _F32)],
        compiler_params=pltpu.CompilerParams(dimension_semantics=("parallel",), vmem_limit_bytes=VMEM_LIMIT),
        name="outproj_router",
    )(x, x_meta, o_na_g, o_na_m, o_mla_g, o_mla_m, wo_na, wo_mla, g_ffn, wr_hi, wr_lo)


XE_ALIGN = 16
SLOT_W = 64
WIN = 64


def _route_kernel(aff_ref, slot_ref, lo_ref, *, cap):
    n_tiles = aff_ref.shape[0]
    bits = lambda: pltpu.bitcast(aff_ref[...], jnp.int32)

    def bit_step(i, cur):
        cand = cur | jnp.left_shift(jnp.int32(1), 30 - i)
        cnt = jnp.sum((bits() >= cand).astype(jnp.int32), axis=(0, 2), keepdims=True)
        return jnp.where(cnt >= cap, cand, cur)
    thr = lax.fori_loop(0, 31, bit_step, jnp.zeros((1, N_EXPERTS, 1), jnp.int32))
    n_gt = jnp.sum((bits() > thr).astype(jnp.int32), axis=(0, 2), keepdims=True)
    need = (cap - n_gt)[0].astype(_F32)
    thr = thr[0]
    r = lax.broadcasted_iota(jnp.int32, (ROW_TILE, ROW_TILE), 0)
    c = lax.broadcasted_iota(jnp.int32, (ROW_TILE, ROW_TILE), 1)
    tri = jnp.where(r <= c, 1.0, 0.0).astype(_BF)
    as_bf = lambda m: jnp.where(m, 1.0, 0.0).astype(_BF)

    def tile_step(t, carry):
        tie_before, kept_before = carry
        b = pltpu.bitcast(aff_ref[t], jnp.int32)
        eq = b == thr
        tie_rank = _dot(as_bf(eq), tri) + tie_before
        keep = (b > thr) | (eq & (tie_rank <= need))
        kept_incl = _dot(as_bf(keep), tri) + kept_before
        slot_ref[t] = jnp.where(keep, kept_incl - 1.0, -1.0).astype(jnp.int32)
        lo_ref[t] = jnp.broadcast_to(kept_before, (N_EXPERTS, LANES)).astype(jnp.int32)
        return tie_rank[:, ROW_TILE - 1:], kept_incl[:, ROW_TILE - 1:]
    zero = jnp.zeros((N_EXPERTS, 1), _F32)
    lax.fori_loop(0, n_tiles, tile_step, (zero, zero))


def _route(aff, cap):
    n_tiles = aff.shape[0]
    full = lambda shape: pl.BlockSpec(shape, lambda i: (0, 0, 0))
    return pl.pallas_call(
        functools.partial(_route_kernel, cap=cap),
        grid=(1,),
        in_specs=[full(aff.shape)],
        out_specs=[full(aff.shape), full((n_tiles, N_EXPERTS, LANES))],
        out_shape=[jax.ShapeDtypeStruct(aff.shape, jnp.int32),
                   jax.ShapeDtypeStruct((n_tiles, N_EXPERTS, LANES), jnp.int32)],
        compiler_params=pltpu.CompilerParams(dimension_semantics=("arbitrary",), vmem_limit_bytes=VMEM_LIMIT),
        name="route",
    )(aff)


def _tile_max(lo_ref, i, start_of):
    m = lo_ref[(i + 1) * N_EXPERTS] - start_of(0)
    for e in range(1, N_EXPERTS):
        m = jnp.maximum(m, lo_ref[(i + 1) * N_EXPERTS + e] - start_of(e))
    return m


def _dispatch_kernel(lo_ref, hn_ref, slot_ref, xe_ref, stage, stage_extra, carry, sem, sem_extra, *, zero_from,
                     zero_rows):
    i = pl.program_id(0)
    s = i % 2
    base = lambda e: (lo_ref[i * N_EXPERTS + e] // XE_ALIGN) * XE_ALIGN
    next_base = lambda e: (lo_ref[(i + 1) * N_EXPERTS + e] // XE_ALIGN) * XE_ALIGN

    def compact(k, dst):
        w = lax.broadcasted_iota(jnp.int32, (SLOT_W, ROW_TILE), 0) + k * SLOT_W
        onehot = jnp.concatenate(
            [jnp.where(slot_ref[e:e + 1, :] - base(e) == w, 1.0, 0.0).astype(_BF) for e in range(N_EXPERTS)], axis=0)
        dst[...] = _dot(onehot, hn_ref[...]).astype(_BF)

    def copy(e, src, k, sm):
        dst_row = pl.multiple_of(base(e) + k * SLOT_W, XE_ALIGN)
        return pltpu.make_async_copy(src.at[pl.ds(e * SLOT_W, SLOT_W)], xe_ref.at[e, pl.ds(dst_row, SLOT_W)], sm)

    def save_carry(k, src):
        for e in range(N_EXPERTS):
            g = next_base(e) - base(e) - k * SLOT_W

            @pl.when((g >= 0) & (g < SLOT_W))
            def _():
                carry[e] = src[pl.ds(pl.multiple_of(e * SLOT_W + g, XE_ALIGN), XE_ALIGN), :]

    @pl.when(i == 0)
    def _():
        carry[...] = jnp.zeros(carry.shape, _BF)
        stage_extra[...] = jnp.zeros(stage_extra.shape, _BF)
        zero_copies = [pltpu.make_async_copy(stage_extra.at[pl.ds(0, zero_rows)],
                                             xe_ref.at[e, pl.ds(zero_from, zero_rows)], sem_extra)
                       for e in range(N_EXPERTS)]
        for cp in zero_copies:
            cp.start()
        for cp in zero_copies:
            cp.wait()

    compact(0, stage.at[s])
    for e in range(N_EXPERTS):
        stage[s, e * SLOT_W:e * SLOT_W + XE_ALIGN, :] += carry[e]
    save_carry(0, stage.at[s])

    @pl.when(i > 0)
    def _():
        for e in range(N_EXPERTS):
            copy(e, stage.at[1 - s], 0, sem.at[1 - s]).wait()

    for e in range(N_EXPERTS):
        copy(e, stage.at[s], 0, sem.at[s]).start()

    def extra_pass(k, c):
        compact(k, stage_extra)
        save_carry(k, stage_extra)
        for wait in (False, True):
            for e in range(N_EXPERTS):
                @pl.when(lo_ref[(i + 1) * N_EXPERTS + e] - base(e) >= k * SLOT_W)
                def _():
                    cp = copy(e, stage_extra, k, sem_extra)
                    cp.wait() if wait else cp.start()
        return c
    lax.fori_loop(1, _tile_max(lo_ref, i, base) // SLOT_W + 1, extra_pass, 0)

    @pl.when(i == pl.num_programs(0) - 1)
    def _():
        for e in range(N_EXPERTS):
            copy(e, stage.at[s], 0, sem.at[s]).wait()


def _dispatch(lo, hn, slot, cap, cap_pad):
    n_tiles = slot.shape[0]
    zero_from = cap // XE_ALIGN * XE_ALIGN
    zero_rows = cap_pad + SLOT_W - zero_from
    assert zero_rows <= N_EXPERTS * SLOT_W and SLOT_W % XE_ALIGN == 0 and cap_pad % XE_ALIGN == 0
    return pl.pallas_call(
        functools.partial(_dispatch_kernel, zero_from=zero_from, zero_rows=zero_rows),
        grid_spec=pltpu.PrefetchScalarGridSpec(
            num_scalar_prefetch=1,
            grid=(n_tiles,),
            in_specs=[pl.BlockSpec((ROW_TILE, D_MODEL), lambda i, lo: (i, 0)),
                      pl.BlockSpec((None, N_EXPERTS, ROW_TILE), lambda i, lo: (i, 0, 0))],
            out_specs=pl.BlockSpec(memory_space=pl.ANY),
            scratch_shapes=[pltpu.VMEM((2, N_EXPERTS * SLOT_W, D_MODEL), _BF),
                            pltpu.VMEM((N_EXPERTS * SLOT_W, D_MODEL), _BF),
                            pltpu.VMEM((N_EXPERTS, XE_ALIGN, D_MODEL), _BF),
                            pltpu.SemaphoreType.DMA((2,)), pltpu.SemaphoreType.DMA(())]),
        out_shape=jax.ShapeDtypeStruct((N_EXPERTS, cap_pad + SLOT_W, D_MODEL), _BF),
        compiler_params=pltpu.CompilerParams(dimension_semantics=("arbitrary",), vmem_limit_bytes=VMEM_LIMIT),
        name="dispatch",
    )(lo, hn, slot)


def _expert_kernel(x_ref, wg_ref, wu_ref, wd_ref, y_ref):
    x = x_ref[...]
    g = _dot(x, wg_ref[...])
    u = _dot(x, wu_ref[...])
    hid = (g * jax.nn.sigmoid(g) * u).astype(_BF)
    y_ref[...] = _dot(hid, wd_ref[...]).astype(_BF)


def _experts(xe, wg, wu, wd, cap_pad, tile):
    n_e = xe.shape[0]
    wspec = lambda a: pl.BlockSpec((None,) + a.shape[1:], lambda e, t: (e, 0, 0))
    return pl.pallas_call(
        _expert_kernel,
        grid=(n_e, cap_pad // tile),
        in_specs=[pl.BlockSpec((None, tile, D_MODEL), lambda e, t: (e, t, 0)), wspec(wg), wspec(wu), wspec(wd)],
        out_specs=pl.BlockSpec((None, tile, D_MODEL), lambda e, t: (e, t, 0)),
        out_shape=jax.ShapeDtypeStruct((n_e, cap_pad, D_MODEL), _BF),
        compiler_params=pltpu.CompilerParams(dimension_semantics=("parallel", "arbitrary"),
                                             vmem_limit_bytes=VMEM_LIMIT),
        name="experts",
    )(xe, wg, wu, wd)


def _combine_kernel(lo_ref, h_ref, slot_ref, aff_ref, g_ref, ye_ref, o_ref, win, win_extra, f_acc, sem, sem_extra, *,
                    cap_pad):
    i = pl.program_id(0)
    s = i % 2
    first = lambda t, e: (lo_ref[t * N_EXPERTS + e] // 16) * 16

    def start_row(t, e, k):
        return pl.multiple_of(jnp.minimum(first(t, e) + k * WIN, cap_pad - WIN), 16)

    def window_copies(t, k, buf, sm):
        return [pltpu.make_async_copy(ye_ref.at[e, pl.ds(start_row(t, e, k), WIN)], buf.at[pl.ds(e * WIN, WIN)], sm)
                for e in range(N_EXPERTS)]

    def weights(k):
        w = lax.broadcasted_iota(jnp.int32, (WIN, ROW_TILE), 0)
        parts = []
        for e in range(N_EXPERTS):
            slot = slot_ref[e:e + 1, :]
            hit = (slot == w + start_row(i, e, k)) & (slot >= first(i, e) + k * WIN)
            parts.append(jnp.where(hit, aff_ref[e:e + 1, :], 0.0).astype(_BF))
        return jnp.concatenate(parts, axis=0)

    scatter = lambda wts, rows: lax.dot_general(wts, rows, (((0,), (0,)), ((), ())), preferred_element_type=_F32)

    @pl.when(i == 0)
    def _():
        for cp in window_copies(0, 0, win.at[0], sem.at[0]):
            cp.start()

    @pl.when(i + 1 < pl.num_programs(0))
    def _():
        for cp in window_copies(i + 1, 0, win.at[1 - s], sem.at[1 - s]):
            cp.start()

    wts = weights(0)
    for cp in window_copies(i, 0, win.at[s], sem.at[s]):
        cp.wait()
    f_acc[...] = scatter(wts, win[s])

    def extra_pass(k, c):
        copies = window_copies(i, k, win_extra, sem_extra)
        for cp in copies:
            cp.start()
        wts_k = weights(k)
        for cp in copies:
            cp.wait()
        f_acc[...] += scatter(wts_k, win_extra[...])
        return c
    reach = _tile_max(lo_ref, i, lambda e: first(i, e))
    lax.fori_loop(1, (reach + WIN - 1) // WIN, extra_pass, 0)
    o_ref[...] = _rms(h_ref[...] + f_acc[...], g_ref[...])


def _combine(lo, h, slot, aff, g_final, ye, n_rows):
    cap_pad = ye.shape[1]
    assert cap_pad % 16 == 0 and WIN % 16 == 0 and cap_pad >= WIN
    tile3 = pl.BlockSpec((None, N_EXPERTS, ROW_TILE), lambda i, lo: (i, 0, 0))
    return pl.pallas_call(
        functools.partial(_combine_kernel, cap_pad=cap_pad),
        grid_spec=pltpu.PrefetchScalarGridSpec(
            num_scalar_prefetch=1,
            grid=(n_rows // ROW_TILE,),
            in_specs=[pl.BlockSpec((ROW_TILE, D_MODEL), lambda i, lo: (i, 0)), tile3, tile3,
                      pl.BlockSpec(g_final.shape, lambda i, lo: (0, 0)), pl.BlockSpec(memory_space=pl.ANY)],
            out_specs=pl.BlockSpec((ROW_TILE, D_MODEL), lambda i, lo: (i, 0)),
            scratch_shapes=[pltpu.VMEM((2, N_EXPERTS * WIN, D_MODEL), _BF), pltpu.VMEM((N_EXPERTS * WIN, D_MODEL), _BF),
                            pltpu.VMEM((ROW_TILE, D_MODEL), _F32),
                            pltpu.SemaphoreType.DMA((2,)), pltpu.SemaphoreType.DMA(())]),
        out_shape=jax.ShapeDtypeStruct((n_rows, D_MODEL), _F32),
        compiler_params=pltpu.CompilerParams(dimension_semantics=("arbitrary",), vmem_limit_bytes=VMEM_LIMIT),
        name="combine_norm",
    )(lo, h, slot, aff, g_final, ye)


def _rotate_half_cols(w):
    half = QK_ROPE // 2
    return jnp.concatenate([-w[..., half:], w[..., :half]], axis=-1)


def _prepare_weights(w_in, g_attn, g_q, w_uq, g_kv, w_ukv, w_o, g_ffn, w_router, w_gate, w_up, w_down, na_rpb):
    na_scale = 1.0 / math.sqrt(NA_HEAD_DIM)
    c0, c1, c2 = 3 * NA_WIDTH, 3 * NA_WIDTH + Q_LORA, 3 * NA_WIDTH + Q_LORA + KV_LORA
    w_kr = w_in[:, c2:]
    zpad = lambda n: jnp.zeros((D_MODEL, n), _F32)
    slot = lambda w: jnp.concatenate([zpad(QK_NOPE), w, zpad(LANES - QK_NOPE - QK_ROPE)], axis=1)
    w1 = jnp.concatenate([w_in[:, NA_WIDTH:2 * NA_WIDTH], w_in[:, c0:c2], slot(w_kr),
                          slot(_rotate_half_cols(w_kr))], axis=1).astype(_BF)
    q_scale = LOG2E / math.sqrt(QK_NOPE + QK_ROPE)
    wq = (w_uq * q_scale).reshape(Q_LORA, MLA_HEADS, QK_NOPE + QK_ROPE)
    pad = LANES - QK_NOPE - QK_ROPE
    plain = jnp.pad(wq, ((0, 0), (0, 0), (0, pad)))
    rot = jnp.pad(_rotate_half_cols(wq[..., QK_NOPE:]), ((0, 0), (0, 0), (QK_NOPE, pad)))
    w2t = jnp.concatenate([plain.reshape(Q_LORA, -1), rot.reshape(Q_LORA, -1)], axis=1).T.astype(_BF)
    wkv = w_ukv.reshape(KV_LORA, MLA_HEADS, QK_NOPE + V_HEAD)
    w3k = jnp.pad(wkv[..., :QK_NOPE], ((0, 0), (0, 0), (0, LANES - QK_NOPE))).reshape(KV_LORA, -1).astype(_BF)
    w3vt = wkv[..., QK_NOPE:].reshape(KV_LORA, -1).T.astype(_BF)
    wr = w_router.T
    wr_hi = wr.astype(_BF)
    wr_lo = (wr - wr_hi.astype(_F32)).astype(_BF)
    return dict(
        w1=w1, w2t=w2t, w3k=w3k, w3vt=w3vt,
        g_attn=g_attn[None], g_q=g_q[None], g_kv=g_kv[None], g_ffn=g_ffn[None],
        wo_na=w_o[:NA_WIDTH].astype(_BF), wo_mla=w_o[NA_WIDTH:].astype(_BF),
        wr_hi=wr_hi, wr_lo=wr_lo,
        wg=w_gate.astype(_BF), wu=w_up.astype(_BF), wd=w_down.astype(_BF),
        wqvt=jnp.concatenate([w_in[:, :NA_WIDTH] * (na_scale * LOG2E), w_in[:, 2 * NA_WIDTH:3 * NA_WIDTH]],
                             axis=1).T.astype(_BF),
        na_bias_t=_na_bias_table(na_rpb),
    )


def _na_bias_table(rpb):
    ql = np.arange(NA_BLK)
    kl = np.arange(3 * NA_BLK)
    ri, qc = ql // GRID_W, ql % GRID_W
    kj, kc = kl // GRID_W, kl % GRID_W
    qcs = np.clip(qc - WIN_C // 2, 0, GRID_W - WIN_C)
    col_ok = (kc[None, :] >= qcs[:, None]) & (kc[None, :] < qcs[:, None] + WIN_C)
    lo = np.stack([np.full(NA_BLK, NA_ROWS), ri, np.zeros(NA_BLK, np.int64)])
    row_ok = (kj[None, None, :] >= lo[:, :, None]) & (kj[None, None, :] < lo[:, :, None] + WIN_R)
    ok = row_ok & col_ok[None]
    rows_q, rows_k, cols = np.arange(NA_ROWS), np.arange(3 * NA_ROWS), np.arange(GRID_W)
    dr = np.clip(rows_k[None, :] - rows_q[:, None] + (WIN_R - 1) - NA_ROWS, 0, 2 * WIN_R - 2)
    dc = np.clip(cols[None, :] - cols[:, None] + (WIN_C - 1), 0, 2 * WIN_C - 2)
    pick_r = (dr[..., None] == np.arange(2 * WIN_R - 1)).astype(np.float32)
    pick_c = (dc[..., None] == np.arange(2 * WIN_C - 1)).astype(np.float32)
    vals = jnp.einsum("hrc,ijr,qkc->hiqjk", rpb.astype(_F32), pick_r, pick_c, precision=lax.Precision.HIGHEST)
    vals = vals.reshape(NA_HEADS, NA_BLK, 3 * NA_BLK)
    return jnp.where(jnp.asarray(ok)[:, None], vals[None] * LOG2E, NEG).transpose(0, 1, 3, 2)


def _rope_tables(batch, seq, np_rows):
    half = QK_ROPE // 2
    inv_freq = ROPE_THETA ** (-jnp.arange(half, dtype=_F32) * 2.0 / QK_ROPE)
    pos_grid = jnp.tile(jnp.arange(seq, dtype=_F32) + N_META, batch)
    pos_meta = jnp.tile(jnp.arange(N_META, dtype=_F32), batch)
    pos = jnp.concatenate([pos_grid, pos_meta, jnp.zeros(np_rows - batch * (seq + N_META), _F32)])
    ang = pos[:, None] * inv_freq[None, :]
    cos, sin = jnp.cos(ang), jnp.sin(ang)
    pad = LANES - QK_NOPE - QK_ROPE
    cos_t = jnp.concatenate([jnp.ones((np_rows, QK_NOPE), _F32), cos, cos, jnp.zeros((np_rows, pad), _F32)], axis=1)
    sin_t = jnp.concatenate([jnp.zeros((np_rows, QK_NOPE), _F32), sin, sin, jnp.zeros((np_rows, pad), _F32)], axis=1)
    return cos_t, sin_t


def _expert_tile(cap):
    if cap <= 1024:
        return -(-cap // 16) * 16
    return min(range(512, 1025, 16), key=lambda t: (-(-cap // t) * t - cap, -t))


def _run_group(x, meta_tokens, prm, g_final, tq, tk):
    batch, seq, _ = x.shape
    n_grid = batch * seq
    n_valid = n_grid + batch * N_META
    np_rows = n_grid + ROW_TILE
    assert seq % NA_BLK == 0 and seq // GRID_W >= WIN_R and batch * N_META <= ROW_TILE
    meta = jnp.broadcast_to(meta_tokens[None], (batch, N_META, D_MODEL)).reshape(batch * N_META, D_MODEL)
    xg = x.reshape(n_grid, D_MODEL)
    xm = jnp.pad(meta, ((0, ROW_TILE - batch * N_META), (0, 0)))
    cos_r, sin_r = _rope_tables(batch, seq, np_rows)
    kna, qt, k3, vt, qnat, vnat = _proj(xg, xm, cos_r, sin_r, cos_r.T, sin_r.T, prm["g_attn"], prm["g_q"], prm["g_kv"],
                                        prm["w1"], prm["w2t"], prm["w3k"], prm["w3vt"], prm["wqvt"])
    n_meta = batch * N_META
    meta_cols = slice(n_grid, n_grid + n_meta)
    vmt_na = vnat[:, :, meta_cols].reshape(NA_HEADS, V_EXT, batch, N_META).transpose(2, 0, 1, 3)
    o_na_g = _na_attention(qnat, kna, vnat, vmt_na, prm["na_bias_t"], batch, seq).reshape(NA_WIDTH, n_grid)
    q_own = jnp.stack([qnat[h, (h % 2) * NA_HEAD_DIM:(h % 2 + 1) * NA_HEAD_DIM, meta_cols] for h in range(NA_HEADS)])
    rows_of = lambda t: t.transpose(2, 0, 1).reshape(n_meta, NA_WIDTH)
    o_na_m = _na_meta_attention(rows_of(q_own), kna[meta_cols], rows_of(vnat[:, :NA_HEAD_DIM, meta_cols]), batch)
    km = k3[:, n_grid:n_grid + n_meta].reshape(MLA_HEADS, batch, N_META, LANES).transpose(1, 0, 2, 3)
    vmt = vt[:, :, n_grid:n_grid + n_meta].reshape(MLA_HEADS, V_EXT, batch, N_META).transpose(2, 0, 1, 3)
    tk = min(tk, seq)
    ot_g = _mla_attention(qt, k3, vt, km, vmt, batch, seq, seq, 0, seq // tq, tq, tk)
    ot_m = _mla_attention(qt, k3, vt, km, vmt, batch, seq, ROW_TILE, n_grid // ROW_TILE, 0, ROW_TILE, tk)
    ot_m = ot_m.reshape(MLA_WIDTH, batch, ROW_TILE)
    ot_m = jnp.concatenate([ot_m[:, b, b * N_META:(b + 1) * N_META] for b in range(batch)], axis=1)
    ot_m = jnp.pad(ot_m, ((0, 0), (0, ROW_TILE - n_meta)))
    o_na_m = jnp.pad(o_na_m.T, ((0, 0), (0, ROW_TILE - n_meta)))
    h, hn, aff = _outproj(xg, xm, o_na_g, o_na_m, ot_g.reshape(MLA_WIDTH, n_grid), ot_m, prm["wo_na"], prm["wo_mla"],
                          prm["g_ffn"], prm["wr_hi"], prm["wr_lo"], n_valid)
    cap = EC_CAPACITY * n_valid // N_EXPERTS
    tile = _expert_tile(cap)
    cap_pad = -(-cap // tile) * tile
    slot, lo3 = _route(aff, cap)
    lo = jnp.concatenate([lo3[:, :, 0], jnp.full((1, N_EXPERTS), cap, jnp.int32)]).reshape(-1)
    xe = _dispatch(lo, hn, slot, cap, cap_pad)
    ye = _experts(xe, prm["wg"], prm["wu"], prm["wd"], cap_pad, tile)
    return _combine(lo, h, slot, aff, g_final[None], ye, n_grid).reshape(batch, seq, D_MODEL)


def kernel(x_prompt, x_sample, meta_tokens, g_attn, w_in, na_rpb, g_q, w_uq, g_kv, w_ukv, w_o, g_ffn, w_router,
           w_gate, w_up, w_down, g_final):
    prm = _prepare_weights(w_in[0], g_attn[0], g_q[0], w_uq[0], g_kv[0], w_ukv[0], w_o[0], g_ffn[0], w_router[0],
                           w_gate[0], w_up[0], w_down[0], na_rpb[0])
    y_prompt = _run_group(x_prompt, meta_tokens, prm, g_final, MLA_TQ, MLA_TK)
    y_sample = _run_group(x_sample, meta_tokens, prm, g_final, MLA_TQ, MLA_TK)
    return (y_prompt, y_sample)
```

```python
import functools
import math

import jax
import jax.numpy as jnp
import numpy as np
from jax import lax
from jax.experimental import pallas as pl
from jax.experimental.pallas import tpu as pltpu

D_MODEL = 1024
GRID_W = 64
N_META = 16
WIN_R = 8
WIN_C = 16
NA_HEADS = 8
NA_HEAD_DIM = 64
MLA_HEADS = 8
QK_NOPE = 64
QK_ROPE = 32
V_HEAD = 64
Q_LORA = 768
KV_LORA = 256
ROPE_THETA = 10000.0
N_EXPERTS = 16
EC_CAPACITY = 2
D_EXPERT = 2048
NORM_EPS = 1e-6
NA_WIDTH = NA_HEADS * NA_HEAD_DIM
MLA_WIDTH = MLA_HEADS * V_HEAD

LANES = 128
ROW_TILE = 256
OUT_SUB = 128
NA_ROWS = 4
NA_BLK = NA_ROWS * GRID_W
NA_LOOKAHEAD = 4
NEG = -1e30
LOG2E = 1.4426950408889634
VMEM_LIMIT = 56 * 1024 * 1024

_BF = jnp.bfloat16
_F32 = jnp.float32


def _dot(a, b):
    return jnp.dot(a, b, preferred_element_type=_F32)


def _dot_nt(a, b):
    return lax.dot_general(a, b, (((1,), (1,)), ((), ())), preferred_element_type=_F32)


def _rms(x, g):
    return x * lax.rsqrt(jnp.mean(x * x, axis=-1, keepdims=True) + NORM_EPS) * g


def _proj_kernel(x_ref, xm_ref, cos_ref, sin_ref, cost_ref, sint_ref, g_attn_ref, g_q_ref, g_kv_ref, w1_ref, w2t_ref,
                 w3k_ref, w3vt_ref, wqvt_ref, kna_ref, qt_ref, k_ref, vt_ref, qnat_ref, vnat_ref):
    x = jnp.where(pl.program_id(0) == pl.num_programs(0) - 1, xm_ref[...], x_ref[...])
    a = _rms(x, g_attn_ref[...]).astype(_BF)
    p = _dot(a, w1_ref[...])
    kna_ref[...] = p[:, :NA_WIDTH].astype(_BF)
    o = NA_WIDTH
    cqn = _rms(p[:, o:o + Q_LORA], g_q_ref[...]).astype(_BF)
    o += Q_LORA
    ckvn = _rms(p[:, o:o + KV_LORA], g_kv_ref[...]).astype(_BF)
    o += KV_LORA
    kr = p[:, o:o + LANES] * cos_ref[...] + p[:, o + LANES:o + 2 * LANES] * sin_ref[...]
    k3 = _dot(ckvn, w3k_ref[...])
    q2t = _dot_nt(w2t_ref[...], cqn)
    vt = _dot_nt(w3vt_ref[...], ckvn)
    cos_t = cost_ref[...]
    sin_t = sint_ref[...]
    hw = MLA_HEADS * LANES
    ones_row = jnp.where(lax.broadcasted_iota(jnp.int32, (V_EXT - V_HEAD, x_ref.shape[0]), 0) == 0, 1.0, 0.0)
    for h in range(MLA_HEADS):
        sl = slice(h * LANES, (h + 1) * LANES)
        qt_ref[h] = (q2t[sl] * cos_t + q2t[hw + h * LANES:hw + (h + 1) * LANES] * sin_t).astype(_BF)
        k_ref[h] = (k3[:, sl] + kr).astype(_BF)
        vt_ref[h, :V_HEAD, :] = vt[h * V_HEAD:(h + 1) * V_HEAD].astype(_BF)
        vt_ref[h, V_HEAD:, :] = ones_row.astype(_BF)
    qvt = _dot_nt(wqvt_ref[...], a)
    zero_half = jnp.zeros((NA_HEAD_DIM, x_ref.shape[0]), _BF)
    for h in range(NA_HEADS):
        own = slice((h % 2) * NA_HEAD_DIM, (h % 2 + 1) * NA_HEAD_DIM)
        other = slice((1 - h % 2) * NA_HEAD_DIM, (2 - h % 2) * NA_HEAD_DIM)
        qnat_ref[h, own, :] = qvt[h * NA_HEAD_DIM:(h + 1) * NA_HEAD_DIM].astype(_BF)
        qnat_ref[h, other, :] = zero_half
        vnat_ref[h, :NA_HEAD_DIM, :] = qvt[NA_WIDTH + h * NA_HEAD_DIM:NA_WIDTH + (h + 1) * NA_HEAD_DIM].astype(_BF)
        vnat_ref[h, NA_HEAD_DIM:, :] = ones_row.astype(_BF)


def _proj(x, x_meta, seq, cos, sin, cos_t, sin_t, g_attn, g_q, g_kv, w1, w2t, w3k, w3vt, wqvt):
    n_grid_tiles = x.shape[0] // ROW_TILE
    np_rows = x.shape[0] + ROW_TILE
    tiles_per_seq = seq // ROW_TILE
    table_meta_tile = cos.shape[0] // ROW_TILE - 1
    table_tile = lambda i: jnp.where(i < n_grid_tiles, i % tiles_per_seq, table_meta_tile)
    row = lambda i: (table_tile(i), 0)
    col = lambda i: (0, table_tile(i))
    full = lambda i: (0, 0)
    w = lambda a: pl.BlockSpec(a.shape, full)
    return pl.pallas_call(
        _proj_kernel,
        grid=(np_rows // ROW_TILE,),
        in_specs=[pl.BlockSpec((ROW_TILE, D_MODEL), lambda i: (jnp.minimum(i, n_grid_tiles - 1), 0)), w(x_meta),
                  pl.BlockSpec((ROW_TILE, LANES), row),
                  pl.BlockSpec((ROW_TILE, LANES), row), pl.BlockSpec((LANES, ROW_TILE), col),
                  pl.BlockSpec((LANES, ROW_TILE), col), w(g_attn), w(g_q), w(g_kv), w(w1), w(w2t), w(w3k), w(w3vt),
                  w(wqvt)],
        out_specs=[pl.BlockSpec((ROW_TILE, NA_WIDTH), lambda i: (i, 0)),
                   pl.BlockSpec((MLA_HEADS, LANES, ROW_TILE), lambda i: (0, 0, i)),
                   pl.BlockSpec((MLA_HEADS, ROW_TILE, LANES), lambda i: (0, i, 0)),
                   pl.BlockSpec((MLA_HEADS, V_EXT, ROW_TILE), lambda i: (0, 0, i)),
                   pl.BlockSpec((NA_HEADS, LANES, ROW_TILE), lambda i: (0, 0, i)),
                   pl.BlockSpec((NA_HEADS, V_EXT, ROW_TILE), lambda i: (0, 0, i))],
        out_shape=[jax.ShapeDtypeStruct((np_rows, NA_WIDTH), _BF),
                   jax.ShapeDtypeStruct((MLA_HEADS, LANES, np_rows), _BF),
                   jax.ShapeDtypeStruct((MLA_HEADS, np_rows, LANES), _BF),
                   jax.ShapeDtypeStruct((MLA_HEADS, V_EXT, np_rows), _BF),
                   jax.ShapeDtypeStruct((NA_HEADS, LANES, np_rows), _BF),
                   jax.ShapeDtypeStruct((NA_HEADS, V_EXT, np_rows), _BF)],
        compiler_params=pltpu.CompilerParams(dimension_semantics=("parallel",), vmem_limit_bytes=VMEM_LIMIT),
        name="proj",
    )(x, x_meta, cos, sin, cos_t, sin_t, g_attn, g_q, g_kv, w1, w2t, w3k, w3vt, wqvt)


def _na_kernel(qt_ref, kp_ref, kc_ref, kn_ref, vtp_ref, vtc_ref, vtn_ref, km_ref, vmt_ref, bias_ref, ot_ref):
    ks = (kp_ref, kc_ref, kn_ref)
    vts = (vtp_ref, vtc_ref, vtn_ref)
    steps = [(h, c) for h in range(NA_HEADS) for c in range(3)]
    lanes = lambda h: slice((h // 2) * LANES, (h // 2 + 1) * LANES)
    scores = lambda h, c: _dot(ks[c][:, lanes(h)], qt_ref[h])
    meta_s = [_dot(km_ref[:, lanes(h)], qt_ref[h]) for h in range(NA_HEADS)]
    meta_m = [jnp.max(sm, axis=0, keepdims=True) for sm in meta_s]
    meta_acc = [_dot(vmt_ref[h], jnp.exp2(meta_s[h] - meta_m[h]).astype(_BF)) for h in range(NA_HEADS)]
    pending = [scores(*st) for st in steps[:NA_LOOKAHEAD]]
    for n, (h, c) in enumerate(steps):
        if n + NA_LOOKAHEAD < len(steps):
            pending.append(scores(*steps[n + NA_LOOKAHEAD]))
        s = bias_ref[h, c * NA_BLK:(c + 1) * NA_BLK, :] + pending.pop(0)
        if c == 0:
            m, acc = meta_m[h], meta_acc[h]
        m_new = jnp.maximum(m, jnp.max(s, axis=0, keepdims=True))
        p = jnp.exp2(s - m_new).astype(_BF)
        acc = acc * jnp.exp2(m - m_new) + _dot(vts[c][h], p)
        m = m_new
        if c == 2:
            ot_ref[h] = (acc[:NA_HEAD_DIM] / acc[NA_HEAD_DIM:NA_HEAD_DIM + 1]).astype(_BF)


def _na_attention(qt, k, vt, vmt, bias_t, batch, seq):
    nblk = seq // NA_BLK
    meta_blk0 = batch * seq // N_META
    near = lambda b, j, off: b * nblk + jnp.clip(j + off, 0, nblk - 1)
    k_spec = lambda off: pl.BlockSpec((NA_BLK, NA_WIDTH), lambda b, j: (near(b, j, off), 0))
    vt_spec = lambda off: pl.BlockSpec((NA_HEADS, V_EXT, NA_BLK), lambda b, j: (0, 0, near(b, j, off)))
    cls = lambda b, j: (jnp.where(j == 0, 0, jnp.where(j == nblk - 1, 2, 1)), 0, 0, 0)
    return pl.pallas_call(
        _na_kernel,
        grid=(batch, nblk),
        in_specs=[pl.BlockSpec((NA_HEADS, LANES, NA_BLK), lambda b, j: (0, 0, b * nblk + j)),
                  k_spec(-1), k_spec(0), k_spec(1), vt_spec(-1), vt_spec(0), vt_spec(1),
                  pl.BlockSpec((N_META, NA_WIDTH), lambda b, j: (meta_blk0 + b, 0)),
                  pl.BlockSpec((None, NA_HEADS, V_EXT, N_META), lambda b, j: (b, 0, 0, 0)),
                  pl.BlockSpec((None, NA_HEADS, 3 * NA_BLK, NA_BLK), cls)],
        out_specs=pl.BlockSpec((NA_HEADS, NA_HEAD_DIM, NA_BLK), lambda b, j: (0, 0, b * nblk + j)),
        out_shape=jax.ShapeDtypeStruct((NA_HEADS, NA_HEAD_DIM, batch * seq), _BF),
        compiler_params=pltpu.CompilerParams(dimension_semantics=("parallel", "arbitrary"),
                                             vmem_limit_bytes=VMEM_LIMIT),
        name="na_attn",
    )(qt, k, k, k, vt, vt, vt, k, vmt, bias_t)


def _na_meta_kernel(q_ref, k_ref, v_ref, o_ref):
    for h in range(NA_HEADS):
        sl = slice(h * NA_HEAD_DIM, (h + 1) * NA_HEAD_DIM)
        s = _dot_nt(q_ref[:, sl], k_ref[:, sl])
        p = jnp.exp2(s - jnp.max(s, axis=1, keepdims=True))
        l = jnp.sum(p, axis=1, keepdims=True)
        o_ref[:, sl] = (_dot(p.astype(_BF), v_ref[:, sl]) / l).astype(_BF)


def _na_meta_attention(q, k, v, batch):
    spec = pl.BlockSpec((N_META, NA_WIDTH), lambda b: (b, 0))
    return pl.pallas_call(
        _na_meta_kernel,
        grid=(batch,),
        in_specs=[spec, spec, spec],
        out_specs=spec,
        out_shape=jax.ShapeDtypeStruct((batch * N_META, NA_WIDTH), _BF),
        compiler_params=pltpu.CompilerParams(dimension_semantics=("parallel",)),
        name="na_meta_attn",
    )(q, k, v)


V_EXT = V_HEAD + 16
MLA_CHUNK = 256
MLA_LOOKAHEAD = 6
MLA_QSUB = 256
MLA_TQ = 512
MLA_TK = 4096


def _mla_kernel(qt_ref, k_ref, vt_ref, km_ref, vmt_ref, ot_ref, m_sc, acc_sc, *, chunk):
    j = pl.program_id(2)
    tq = qt_ref.shape[2]
    nchunks = k_ref.shape[1] // chunk

    @pl.when(j == 0)
    def _():
        def init_head(h, carry):
            s = _dot(km_ref[h], qt_ref[h])
            m = jnp.max(s, axis=0, keepdims=True)
            p = jnp.exp2(s - m).astype(_BF)
            acc_sc[h] = _dot(vmt_ref[h], p)
            m_sc[h] = jnp.broadcast_to(m, (8, tq))
            return carry
        lax.fori_loop(0, MLA_HEADS, init_head, 0)

    qsub = min(MLA_QSUB, tq)
    steps = [(qs, h, c) for qs in range(tq // qsub) for h in range(MLA_HEADS) for c in range(nchunks)]
    lanes = lambda qs: slice(qs * qsub, (qs + 1) * qsub)
    scores = lambda qs, h, c: _dot(k_ref[h, c * chunk:(c + 1) * chunk, :], qt_ref[h, :, lanes(qs)])
    pending = [scores(*st) for st in steps[:MLA_LOOKAHEAD]]
    for n, (qs, h, c) in enumerate(steps):
        if n + MLA_LOOKAHEAD < len(steps):
            pending.append(scores(*steps[n + MLA_LOOKAHEAD]))
        s = pending.pop(0)
        if c == 0:
            m = m_sc[h, :1, lanes(qs)]
            acc = acc_sc[h, :, lanes(qs)]
        m_new = jnp.maximum(m, jnp.max(s, axis=0, keepdims=True))
        p = jnp.exp2(s - m_new).astype(_BF)
        acc = acc * jnp.exp2(m - m_new) + _dot(vt_ref[h, :, c * chunk:(c + 1) * chunk], p)
        m = m_new
        if c == nchunks - 1:
            m_sc[h, :, lanes(qs)] = jnp.broadcast_to(m, (8, qsub))
            acc_sc[h, :, lanes(qs)] = acc

    @pl.when(j == pl.num_programs(2) - 1)
    def _():
        for h in range(MLA_HEADS):
            acc = acc_sc[h]
            ot_ref[h] = (acc[:V_HEAD] / acc[V_HEAD:V_HEAD + 1]).astype(_BF)


def _mla_attention(qt, k, vt, km, vmt, batch, seq, q_rows, q_blk0, q_stride, tq, tk):
    assert q_rows % tq == 0 and seq % tk == 0 and tk % MLA_CHUNK == 0
    nq = q_rows // tq
    nk = seq // tk
    return pl.pallas_call(
        functools.partial(_mla_kernel, chunk=min(MLA_CHUNK, tk)),
        grid=(batch, nq, nk),
        in_specs=[pl.BlockSpec((MLA_HEADS, LANES, tq), lambda b, i, j: (0, 0, q_blk0 + b * q_stride + i)),
                  pl.BlockSpec((MLA_HEADS, tk, LANES), lambda b, i, j: (0, b * nk + j, 0)),
                  pl.BlockSpec((MLA_HEADS, V_EXT, tk), lambda b, i, j: (0, 0, b * nk + j)),
                  pl.BlockSpec((None, MLA_HEADS, N_META, LANES), lambda b, i, j: (b, 0, 0, 0)),
                  pl.BlockSpec((None, MLA_HEADS, V_EXT, N_META), lambda b, i, j: (b, 0, 0, 0))],
        out_specs=pl.BlockSpec((MLA_HEADS, V_HEAD, tq), lambda b, i, j: (0, 0, b * nq + i)),
        out_shape=jax.ShapeDtypeStruct((MLA_HEADS, V_HEAD, batch * q_rows), _BF),
        scratch_shapes=[pltpu.VMEM((MLA_HEADS, 8, tq), _F32), pltpu.VMEM((MLA_HEADS, V_EXT, tq), _F32)],
        compiler_params=pltpu.CompilerParams(dimension_semantics=("parallel", "parallel", "arbitrary"),
                                             vmem_limit_bytes=VMEM_LIMIT),
        name="mla_attn",
    )(qt, k, vt, km, vmt)


def _outproj_kernel(x_ref, xm_ref, ong_ref, onm_ref, omg_ref, omm_ref, wo_na_ref, wo_mla_ref, g_ref, wr_hi_ref,
                    wr_lo_ref, h_ref, hn_ref, aff_ref, *, n_grid_tiles, n_valid):
    i = pl.program_id(0)
    is_meta = i == n_grid_tiles
    tn = lambda a_t, b: lax.dot_general(a_t, b, (((0,), (0,)), ((), ())), preferred_element_type=_F32)
    wr_hi = wr_hi_ref[...]
    subs = [slice(r0, r0 + OUT_SUB) for r0 in range(0, ROW_TILE, OUT_SUB)]
    hs = []
    for rows in subs:
        o_na_t = jnp.where(is_meta, onm_ref[:, rows], ong_ref[:, rows])
        o_mla_t = jnp.where(is_meta, omm_ref[:, rows], omg_ref[:, rows])
        x = jnp.where(is_meta, xm_ref[rows, :], x_ref[rows, :])
        hs.append(x + tn(o_na_t, wo_na_ref[...]) + tn(o_mla_t, wo_mla_ref[...]))
    split = []
    for rows, h in zip(subs, hs):
        h_ref[rows, :] = h
        hn = _rms(h, g_ref[...])
        hn_hi = hn.astype(_BF)
        hn_ref[rows, :] = hn_hi
        split.append((hn_hi, (hn - hn_hi.astype(_F32)).astype(_BF)))
    for rows, (hn_hi, hn_lo) in zip(subs, split):
        r0 = rows.start
        logits = _dot_nt(wr_hi, hn_hi) + _dot_nt(wr_hi, hn_lo) + _dot_nt(wr_lo_ref[...], hn_hi)
        e = jnp.exp(logits - jnp.max(logits, axis=0, keepdims=True))
        aff = e / jnp.sum(e, axis=0, keepdims=True)
        row = i * ROW_TILE + r0 + lax.broadcasted_iota(jnp.int32, aff.shape, 1)
        aff_ref[:, rows] = jnp.where(row < n_valid, aff, -1.0)


def _outproj(x, x_meta, o_na_g, o_na_m, o_mla_g, o_mla_m, wo_na, wo_mla, g_ffn, wr_hi, wr_lo, n_valid):
    n_grid_tiles = x.shape[0] // ROW_TILE
    n_tiles = n_grid_tiles + 1
    np_rows = n_tiles * ROW_TILE
    row = lambda i: (i, 0)
    grid_col = lambda i: (0, jnp.minimum(i, n_grid_tiles - 1))
    full = lambda i: (0, 0)
    w = lambda a: pl.BlockSpec(a.shape, full)
    return pl.pallas_call(
        functools.partial(_outproj_kernel, n_grid_tiles=n_grid_tiles, n_valid=n_valid),
        grid=(n_tiles,),
        in_specs=[pl.BlockSpec((ROW_TILE, D_MODEL), lambda i: (jnp.minimum(i, n_grid_tiles - 1), 0)), w(x_meta),
                  pl.BlockSpec((NA_WIDTH, ROW_TILE), grid_col), w(o_na_m),
                  pl.BlockSpec((MLA_WIDTH, ROW_TILE), grid_col), w(o_mla_m),
                  w(wo_na), w(wo_mla), w(g_ffn), w(wr_hi), w(wr_lo)],
        out_specs=[pl.BlockSpec((ROW_TILE, D_MODEL), row), pl.BlockSpec((ROW_TILE, D_MODEL), row),
                   pl.BlockSpec((None, N_EXPERTS, ROW_TILE), lambda i: (i, 0, 0))],
        out_shape=[jax.ShapeDtypeStruct((np_rows, D_MODEL), _F32), jax.ShapeDtypeStruct((np_rows, D_MODEL), _BF),
                   jax.ShapeDtypeStruct((n_tiles, N_EXPERTS, ROW_TILE), _F32)],
        compiler_params=pltpu.CompilerParams(dimension_semantics=("parallel",), vmem_limit_bytes=VMEM_LIMIT),
        name="outproj_router",
    )(x, x_meta, o_na_g, o_na_m, o_mla_g, o_mla_m, wo_na, wo_mla, g_ffn, wr_hi, wr_lo)


XE_ALIGN = 16
SLOT_W = 64
WIN = 64


def _route_kernel(aff_ref, slot_ref, lo_ref, *, cap):
    n_tiles = aff_ref.shape[0]
    bits = lambda: pltpu.bitcast(aff_ref[...], jnp.int32)

    def bit_step(i, cur):
        cand = cur | jnp.left_shift(jnp.int32(1), 30 - i)
        cnt = jnp.sum((bits() >= cand).astype(jnp.int32), axis=(0, 2), keepdims=True)
        return jnp.where(cnt >= cap, cand, cur)
    thr = lax.fori_loop(0, 31, bit_step, jnp.zeros((1, N_EXPERTS, 1), jnp.int32))
    n_gt = jnp.sum((bits() > thr).astype(jnp.int32), axis=(0, 2), keepdims=True)
    need = (cap - n_gt)[0].astype(_F32)
    thr = thr[0]
    r = lax.broadcasted_iota(jnp.int32, (ROW_TILE, ROW_TILE), 0)
    c = lax.broadcasted_iota(jnp.int32, (ROW_TILE, ROW_TILE), 1)
    tri = jnp.where(r <= c, 1.0, 0.0).astype(_BF)
    as_bf = lambda m: jnp.where(m, 1.0, 0.0).astype(_BF)

    def tile_step(t, carry):
        tie_before, kept_before = carry
        b = pltpu.bitcast(aff_ref[t], jnp.int32)
        eq = b == thr
        tie_rank = _dot(as_bf(eq), tri) + tie_before
        keep = (b > thr) | (eq & (tie_rank <= need))
        kept_incl = _dot(as_bf(keep), tri) + kept_before
        slot_ref[t] = jnp.where(keep, kept_incl - 1.0, -1.0).astype(jnp.int32)
        lo_ref[t] = jnp.broadcast_to(kept_before, (N_EXPERTS, LANES)).astype(jnp.int32)
        return tie_rank[:, ROW_TILE - 1:], kept_incl[:, ROW_TILE - 1:]
    zero = jnp.zeros((N_EXPERTS, 1), _F32)
    lax.fori_loop(0, n_tiles, tile_step, (zero, zero))


def _route(aff, cap):
    n_tiles = aff.shape[0]
    full = lambda shape: pl.BlockSpec(shape, lambda i: (0, 0, 0))
    return pl.pallas_call(
        functools.partial(_route_kernel, cap=cap),
        grid=(1,),
        in_specs=[full(aff.shape)],
        out_specs=[full(aff.shape), full((n_tiles, N_EXPERTS, LANES))],
        out_shape=[jax.ShapeDtypeStruct(aff.shape, jnp.int32),
                   jax.ShapeDtypeStruct((n_tiles, N_EXPERTS, LANES), jnp.int32)],
        compiler_params=pltpu.CompilerParams(dimension_semantics=("arbitrary",), vmem_limit_bytes=VMEM_LIMIT),
        name="route",
    )(aff)


def _tile_max(lo_ref, i, start_of):
    m = lo_ref[(i + 1) * N_EXPERTS] - start_of(0)
    for e in range(1, N_EXPERTS):
        m = jnp.maximum(m, lo_ref[(i + 1) * N_EXPERTS + e] - start_of(e))
    return m


def _dispatch_kernel(lo_ref, hn_ref, slot_ref, xe_ref, stage, stage_extra, carry, sem, sem_extra, *, zero_from,
                     zero_rows):
    i = pl.program_id(0)
    s = i % 2
    base = lambda e: (lo_ref[i * N_EXPERTS + e] // XE_ALIGN) * XE_ALIGN
    next_base = lambda e: (lo_ref[(i + 1) * N_EXPERTS + e] // XE_ALIGN) * XE_ALIGN

    def compact(k, dst):
        w = lax.broadcasted_iota(jnp.int32, (SLOT_W, ROW_TILE), 0) + k * SLOT_W
        onehot = jnp.concatenate(
            [jnp.where(slot_ref[e:e + 1, :] - base(e) == w, 1.0, 0.0).astype(_BF) for e in range(N_EXPERTS)], axis=0)
        dst[...] = _dot(onehot, hn_ref[...]).astype(_BF)

    def copy(e, src, k, sm):
        dst_row = pl.multiple_of(base(e) + k * SLOT_W, XE_ALIGN)
        return pltpu.make_async_copy(src.at[pl.ds(e * SLOT_W, SLOT_W)], xe_ref.at[e, pl.ds(dst_row, SLOT_W)], sm)

    def save_carry(k, src):
        for e in range(N_EXPERTS):
            g = next_base(e) - base(e) - k * SLOT_W

            @pl.when((g >= 0) & (g < SLOT_W))
            def _():
                carry[e] = src[pl.ds(pl.multiple_of(e * SLOT_W + g, XE_ALIGN), XE_ALIGN), :]

    @pl.when(i == 0)
    def _():
        carry[...] = jnp.zeros(carry.shape, _BF)
        stage_extra[...] = jnp.zeros(stage_extra.shape, _BF)
        zero_copies = [pltpu.make_async_copy(stage_extra.at[pl.ds(0, zero_rows)],
                                             xe_ref.at[e, pl.ds(zero_from, zero_rows)], sem_extra)
                       for e in range(N_EXPERTS)]
        for cp in zero_copies:
            cp.start()
        for cp in zero_copies:
            cp.wait()

    compact(0, stage.at[s])
    for e in range(N_EXPERTS):
        stage[s, e * SLOT_W:e * SLOT_W + XE_ALIGN, :] += carry[e]
    save_carry(0, stage.at[s])

    @pl.when(i > 0)
    def _():
        for e in range(N_EXPERTS):
            copy(e, stage.at[1 - s], 0, sem.at[1 - s]).wait()

    for e in range(N_EXPERTS):
        copy(e, stage.at[s], 0, sem.at[s]).start()

    def extra_pass(k, c):
        compact(k, stage_extra)
        save_carry(k, stage_extra)
        for wait in (False, True):
            for e in range(N_EXPERTS):
                @pl.when(lo_ref[(i + 1) * N_EXPERTS + e] - base(e) >= k * SLOT_W)
                def _():
                    cp = copy(e, stage_extra, k, sem_extra)
                    cp.wait() if wait else cp.start()
        return c
    lax.fori_loop(1, _tile_max(lo_ref, i, base) // SLOT_W + 1, extra_pass, 0)

    @pl.when(i == pl.num_programs(0) - 1)
    def _():
        for e in range(N_EXPERTS):
            copy(e, stage.at[s], 0, sem.at[s]).wait()


def _dispatch(lo, hn, slot, cap, cap_pad):
    n_tiles = slot.shape[0]
    zero_from = cap // XE_ALIGN * XE_ALIGN
    zero_rows = cap_pad + SLOT_W - zero_from
    assert zero_rows <= N_EXPERTS * SLOT_W and SLOT_W % XE_ALIGN == 0 and cap_pad % XE_ALIGN == 0
    return pl.pallas_call(
        functools.partial(_dispatch_kernel, zero_from=zero_from, zero_rows=zero_rows),
        grid_spec=pltpu.PrefetchScalarGridSpec(
            num_scalar_prefetch=1,
            grid=(n_tiles,),
            in_specs=[pl.BlockSpec((ROW_TILE, D_MODEL), lambda i, lo: (i, 0)),
                      pl.BlockSpec((None, N_EXPERTS, ROW_TILE), lambda i, lo: (i, 0, 0))],
            out_specs=pl.BlockSpec(memory_space=pl.ANY),
            scratch_shapes=[pltpu.VMEM((2, N_EXPERTS * SLOT_W, D_MODEL), _BF),
                            pltpu.VMEM((N_EXPERTS * SLOT_W, D_MODEL), _BF),
                            pltpu.VMEM((N_EXPERTS, XE_ALIGN, D_MODEL), _BF),
                            pltpu.SemaphoreType.DMA((2,)), pltpu.SemaphoreType.DMA(())]),
        out_shape=jax.ShapeDtypeStruct((N_EXPERTS, cap_pad + SLOT_W, D_MODEL), _BF),
        compiler_params=pltpu.CompilerParams(dimension_semantics=("arbitrary",), vmem_limit_bytes=VMEM_LIMIT),
        name="dispatch",
    )(lo, hn, slot)


def _expert_kernel(x_ref, wg_ref, wu_ref, wd_ref, y_ref):
    x = x_ref[...]
    g = _dot(x, wg_ref[...])
    u = _dot(x, wu_ref[...])
    hid = (g * jax.nn.sigmoid(g) * u).astype(_BF)
    y_ref[...] = _dot(hid, wd_ref[...]).astype(_BF)


def _experts(xe, wg, wu, wd, cap_pad, tile):
    n_e = xe.shape[0]
    wspec = lambda a: pl.BlockSpec((None,) + a.shape[1:], lambda e, t: (e, 0, 0))
    return pl.pallas_call(
        _expert_kernel,
        grid=(n_e, cap_pad // tile),
        in_specs=[pl.BlockSpec((None, tile, D_MODEL), lambda e, t: (e, t, 0)), wspec(wg), wspec(wu), wspec(wd)],
        out_specs=pl.BlockSpec((None, tile, D_MODEL), lambda e, t: (e, t, 0)),
        out_shape=jax.ShapeDtypeStruct((n_e, cap_pad, D_MODEL), _BF),
        compiler_params=pltpu.CompilerParams(dimension_semantics=("parallel", "arbitrary"),
                                             vmem_limit_bytes=VMEM_LIMIT),
        name="experts",
    )(xe, wg, wu, wd)


def _combine_kernel(lo_ref, h_ref, slot_ref, aff_ref, g_ref, ye_ref, o_ref, win, win_extra, f_acc, sem, sem_extra, *,
                    cap_pad):
    i = pl.program_id(0)
    s = i % 2
    first = lambda t, e: (lo_ref[t * N_EXPERTS + e] // 16) * 16

    def start_row(t, e, k):
        return pl.multiple_of(jnp.minimum(first(t, e) + k * WIN, cap_pad - WIN), 16)

    def window_copies(t, k, buf, sm):
        return [pltpu.make_async_copy(ye_ref.at[e, pl.ds(start_row(t, e, k), WIN)], buf.at[pl.ds(e * WIN, WIN)], sm)
                for e in range(N_EXPERTS)]

    def weights(k):
        w = lax.broadcasted_iota(jnp.int32, (WIN, ROW_TILE), 0)
        parts = []
        for e in range(N_EXPERTS):
            slot = slot_ref[e:e + 1, :]
            hit = (slot == w + start_row(i, e, k)) & (slot >= first(i, e) + k * WIN)
            parts.append(jnp.where(hit, aff_ref[e:e + 1, :], 0.0).astype(_BF))
        return jnp.concatenate(parts, axis=0)

    scatter = lambda wts, rows: lax.dot_general(wts, rows, (((0,), (0,)), ((), ())), preferred_element_type=_F32)

    @pl.when(i == 0)
    def _():
        for cp in window_copies(0, 0, win.at[0], sem.at[0]):
            cp.start()

    @pl.when(i + 1 < pl.num_programs(0))
    def _():
        for cp in window_copies(i + 1, 0, win.at[1 - s], sem.at[1 - s]):
            cp.start()

    wts = weights(0)
    for cp in window_copies(i, 0, win.at[s], sem.at[s]):
        cp.wait()
    f_acc[...] = scatter(wts, win[s])

    def extra_pass(k, c):
        copies = window_copies(i, k, win_extra, sem_extra)
        for cp in copies:
            cp.start()
        wts_k = weights(k)
        for cp in copies:
            cp.wait()
        f_acc[...] += scatter(wts_k, win_extra[...])
        return c
    reach = _tile_max(lo_ref, i, lambda e: first(i, e))
    lax.fori_loop(1, (reach + WIN - 1) // WIN, extra_pass, 0)
    o_ref[...] = _rms(h_ref[...] + f_acc[...], g_ref[...])


def _combine(lo, h, slot, aff, g_final, ye, n_rows):
    cap_pad = ye.shape[1]
    assert cap_pad % 16 == 0 and WIN % 16 == 0 and cap_pad >= WIN
    tile3 = pl.BlockSpec((None, N_EXPERTS, ROW_TILE), lambda i, lo: (i, 0, 0))
    return pl.pallas_call(
        functools.partial(_combine_kernel, cap_pad=cap_pad),
        grid_spec=pltpu.PrefetchScalarGridSpec(
            num_scalar_prefetch=1,
            grid=(n_rows // ROW_TILE,),
            in_specs=[pl.BlockSpec((ROW_TILE, D_MODEL), lambda i, lo: (i, 0)), tile3, tile3,
                      pl.BlockSpec(g_final.shape, lambda i, lo: (0, 0)), pl.BlockSpec(memory_space=pl.ANY)],
            out_specs=pl.BlockSpec((ROW_TILE, D_MODEL), lambda i, lo: (i, 0)),
            scratch_shapes=[pltpu.VMEM((2, N_EXPERTS * WIN, D_MODEL), _BF), pltpu.VMEM((N_EXPERTS * WIN, D_MODEL), _BF),
                            pltpu.VMEM((ROW_TILE, D_MODEL), _F32),
                            pltpu.SemaphoreType.DMA((2,)), pltpu.SemaphoreType.DMA(())]),
        out_shape=jax.ShapeDtypeStruct((n_rows, D_MODEL), _F32),
        compiler_params=pltpu.CompilerParams(dimension_semantics=("arbitrary",), vmem_limit_bytes=VMEM_LIMIT),
        name="combine_norm",
    )(lo, h, slot, aff, g_final, ye)


def _rotate_half_cols(w):
    half = QK_ROPE // 2
    return jnp.concatenate([-w[..., half:], w[..., :half]], axis=-1)


def _prepare_weights(w_in, g_attn, g_q, w_uq, g_kv, w_ukv, w_o, g_ffn, w_router, w_gate, w_up, w_down, na_rpb):
    na_scale = 1.0 / math.sqrt(NA_HEAD_DIM)
    c0, c1, c2 = 3 * NA_WIDTH, 3 * NA_WIDTH + Q_LORA, 3 * NA_WIDTH + Q_LORA + KV_LORA
    w_kr = w_in[:, c2:]
    zpad = lambda n: jnp.zeros((D_MODEL, n), _F32)
    slot = lambda w: jnp.concatenate([zpad(QK_NOPE), w, zpad(LANES - QK_NOPE - QK_ROPE)], axis=1)
    w1 = jnp.concatenate([w_in[:, NA_WIDTH:2 * NA_WIDTH], w_in[:, c0:c2], slot(w_kr),
                          slot(_rotate_half_cols(w_kr))], axis=1).astype(_BF)
    q_scale = LOG2E / math.sqrt(QK_NOPE + QK_ROPE)
    wq = (w_uq * q_scale).reshape(Q_LORA, MLA_HEADS, QK_NOPE + QK_ROPE)
    pad = LANES - QK_NOPE - QK_ROPE
    plain = jnp.pad(wq, ((0, 0), (0, 0), (0, pad)))
    rot = jnp.pad(_rotate_half_cols(wq[..., QK_NOPE:]), ((0, 0), (0, 0), (QK_NOPE, pad)))
    w2t = jnp.concatenate([plain.reshape(Q_LORA, -1), rot.reshape(Q_LORA, -1)], axis=1).T.astype(_BF)
    wkv = w_ukv.reshape(KV_LORA, MLA_HEADS, QK_NOPE + V_HEAD)
    w3k = jnp.pad(wkv[..., :QK_NOPE], ((0, 0), (0, 0), (0, LANES - QK_NOPE))).reshape(KV_LORA, -1).astype(_BF)
    w3vt = wkv[..., QK_NOPE:].reshape(KV_LORA, -1).T.astype(_BF)
    wr = w_router.T
    wr_hi = wr.astype(_BF)
    wr_lo = (wr - wr_hi.astype(_F32)).astype(_BF)
    return dict(
        w1=w1, w2t=w2t, w3k=w3k, w3vt=w3vt,
        g_attn=g_attn[None], g_q=g_q[None], g_kv=g_kv[None], g_ffn=g_ffn[None],
        wo_na=w_o[:NA_WIDTH].astype(_BF), wo_mla=w_o[NA_WIDTH:].astype(_BF),
        wr_hi=wr_hi, wr_lo=wr_lo,
        wg=w_gate.astype(_BF), wu=w_up.astype(_BF), wd=w_down.astype(_BF),
        wqvt=jnp.concatenate([w_in[:, :NA_WIDTH] * (na_scale * LOG2E), w_in[:, 2 * NA_WIDTH:3 * NA_WIDTH]],
                             axis=1).T.astype(_BF),
        na_bias_t=_na_bias_table(na_rpb),
    )


def _na_bias_table(rpb):
    ql = np.arange(NA_BLK)
    kl = np.arange(3 * NA_BLK)
    ri, qc = ql // GRID_W, ql % GRID_W
    kj, kc = kl // GRID_W, kl % GRID_W
    qcs = np.clip(qc - WIN_C // 2, 0, GRID_W - WIN_C)
    col_ok = (kc[None, :] >= qcs[:, None]) & (kc[None, :] < qcs[:, None] + WIN_C)
    lo = np.stack([np.full(NA_BLK, NA_ROWS), ri, np.zeros(NA_BLK, np.int64)])
    row_ok = (kj[None, None, :] >= lo[:, :, None]) & (kj[None, None, :] < lo[:, :, None] + WIN_R)
    ok = row_ok & col_ok[None]
    rows_q, rows_k, cols = np.arange(NA_ROWS), np.arange(3 * NA_ROWS), np.arange(GRID_W)
    dr = np.clip(rows_k[None, :] - rows_q[:, None] + (WIN_R - 1) - NA_ROWS, 0, 2 * WIN_R - 2)
    dc = np.clip(cols[None, :] - cols[:, None] + (WIN_C - 1), 0, 2 * WIN_C - 2)
    pick_r = (dr[..., None] == np.arange(2 * WIN_R - 1)).astype(np.float32)
    pick_c = (dc[..., None] == np.arange(2 * WIN_C - 1)).astype(np.float32)
    vals = jnp.einsum("hrc,ijr,qkc->hiqjk", rpb.astype(_F32), pick_r, pick_c, precision=lax.Precision.HIGHEST)
    vals = vals.reshape(NA_HEADS, NA_BLK, 3 * NA_BLK)
    return jnp.where(jnp.asarray(ok)[:, None], vals[None] * LOG2E, NEG).transpose(0, 1, 3, 2)


def _rope_tables(seq):
    half = QK_ROPE // 2
    inv_freq = ROPE_THETA ** (-jnp.arange(half, dtype=_F32) * 2.0 / QK_ROPE)
    pos = jnp.concatenate([jnp.arange(seq, dtype=_F32) + N_META,
                           jnp.tile(jnp.arange(N_META, dtype=_F32), ROW_TILE // N_META)])
    n = seq + ROW_TILE
    pad = LANES - QK_NOPE - QK_ROPE

    def lay(a, nope_fill, axis):
        shape = lambda k: (n, k) if axis == 1 else (k, n)
        return jnp.concatenate([jnp.full(shape(QK_NOPE), nope_fill, _F32), a, a, jnp.zeros(shape(pad), _F32)], axis=axis)
    ang = pos[:, None] * inv_freq[None, :]
    ang_t = inv_freq[:, None] * pos[None, :]
    return (lay(jnp.cos(ang), 1.0, 1), lay(jnp.sin(ang), 0.0, 1),
            lay(jnp.cos(ang_t), 1.0, 0), lay(jnp.sin(ang_t), 0.0, 0))


def _expert_tile(cap):
    if cap <= 1024:
        return -(-cap // 16) * 16
    return min(range(512, 1025, 16), key=lambda t: (-(-cap // t) * t - cap, -t))


def _run_group(x, meta_tokens, prm, rope, g_final, tq, tk):
    batch, seq, _ = x.shape
    n_grid = batch * seq
    n_valid = n_grid + batch * N_META
    np_rows = n_grid + ROW_TILE
    assert seq % NA_BLK == 0 and seq // GRID_W >= WIN_R and batch * N_META <= ROW_TILE
    meta = jnp.broadcast_to(meta_tokens[None], (batch, N_META, D_MODEL)).reshape(batch * N_META, D_MODEL)
    xg = x.reshape(n_grid, D_MODEL)
    xm = jnp.pad(meta, ((0, ROW_TILE - batch * N_META), (0, 0)))
    assert rope[0].shape[0] >= seq + ROW_TILE
    kna, qt, k3, vt, qnat, vnat = _proj(xg, xm, seq, *rope, prm["g_attn"], prm["g_q"], prm["g_kv"],
                                        prm["w1"], prm["w2t"], prm["w3k"], prm["w3vt"], prm["wqvt"])
    n_meta = batch * N_META
    meta_cols = slice(n_grid, n_grid + n_meta)
    vmt_na = vnat[:, :, meta_cols].reshape(NA_HEADS, V_EXT, batch, N_META).transpose(2, 0, 1, 3)
    o_na_g = _na_attention(qnat, kna, vnat, vmt_na, prm["na_bias_t"], batch, seq).reshape(NA_WIDTH, n_grid)
    q_own = jnp.stack([qnat[h, (h % 2) * NA_HEAD_DIM:(h % 2 + 1) * NA_HEAD_DIM, meta_cols] for h in range(NA_HEADS)])
    rows_of = lambda t: t.transpose(2, 0, 1).reshape(n_meta, NA_WIDTH)
    o_na_m = _na_meta_attention(rows_of(q_own), kna[meta_cols], rows_of(vnat[:, :NA_HEAD_DIM, meta_cols]), batch)
    km = k3[:, n_grid:n_grid + n_meta].reshape(MLA_HEADS, batch, N_META, LANES).transpose(1, 0, 2, 3)
    vmt = vt[:, :, n_grid:n_grid + n_meta].reshape(MLA_HEADS, V_EXT, batch, N_META).transpose(2, 0, 1, 3)
    tk = min(tk, seq)
    ot_g = _mla_attention(qt, k3, vt, km, vmt, batch, seq, seq, 0, seq // tq, tq, tk)
    ot_m = _mla_attention(qt, k3, vt, km, vmt, batch, seq, ROW_TILE, n_grid // ROW_TILE, 0, ROW_TILE, tk)
    ot_m = ot_m.reshape(MLA_WIDTH, batch, ROW_TILE)
    ot_m = jnp.concatenate([ot_m[:, b, b * N_META:(b + 1) * N_META] for b in range(batch)], axis=1)
    ot_m = jnp.pad(ot_m, ((0, 0), (0, ROW_TILE - n_meta)))
    o_na_m = jnp.pad(o_na_m.T, ((0, 0), (0, ROW_TILE - n_meta)))
    h, hn, aff = _outproj(xg, xm, o_na_g, o_na_m, ot_g.reshape(MLA_WIDTH, n_grid), ot_m, prm["wo_na"], prm["wo_mla"],
                          prm["g_ffn"], prm["wr_hi"], prm["wr_lo"], n_valid)
    cap = EC_CAPACITY * n_valid // N_EXPERTS
    tile = _expert_tile(cap)
    cap_pad = -(-cap // tile) * tile
    slot, lo3 = _route(aff, cap)
    lo = jnp.concatenate([lo3[:, :, 0], jnp.full((1, N_EXPERTS), cap, jnp.int32)]).reshape(-1)
    xe = _dispatch(lo, hn, slot, cap, cap_pad)
    ye = _experts(xe, prm["wg"], prm["wu"], prm["wd"], cap_pad, tile)
    return _combine(lo, h, slot, aff, g_final[None], ye, n_grid).reshape(batch, seq, D_MODEL)


def kernel(x_prompt, x_sample, meta_tokens, g_attn, w_in, na_rpb, g_q, w_uq, g_kv, w_ukv, w_o, g_ffn, w_router,
           w_gate, w_up, w_down, g_final):
    prm = _prepare_weights(w_in[0], g_attn[0], g_q[0], w_uq[0], g_kv[0], w_ukv[0], w_o[0], g_ffn[0], w_router[0],
                           w_gate[0], w_up[0], w_down[0], na_rpb[0])
    rope = _rope_tables(max(x_prompt.shape[1], x_sample.shape[1]))
    y_prompt = _run_group(x_prompt, meta_tokens, prm, rope, g_final, MLA_TQ, MLA_TK)
    y_sample = _run_group(x_sample, meta_tokens, prm, rope, g_final, MLA_TQ, MLA_TK)
    return (y_prompt, y_sample)
```

```python
import functools
import math

import jax
import jax.numpy as jnp
import numpy as np
from jax import lax
from jax.experimental import pallas as pl
from jax.experimental.pallas import tpu as pltpu

D_MODEL = 1024
GRID_W = 64
N_META = 16
WIN_R = 8
WIN_C = 16
NA_HEADS = 8
NA_HEAD_DIM = 64
MLA_HEADS = 8
QK_NOPE = 64
QK_ROPE = 32
V_HEAD = 64
Q_LORA = 768
KV_LORA = 256
ROPE_THETA = 10000.0
N_EXPERTS = 16
EC_CAPACITY = 2
D_EXPERT = 2048
NORM_EPS = 1e-6
NA_WIDTH = NA_HEADS * NA_HEAD_DIM
MLA_WIDTH = MLA_HEADS * V_HEAD

LANES = 128
ROW_TILE = 256
OUT_SUB = 128
NA_ROWS = 4
NA_BLK = NA_ROWS * GRID_W
NA_LOOKAHEAD = 5
NEG = -1e30
LOG2E = 1.4426950408889634
VMEM_LIMIT = 56 * 1024 * 1024

_BF = jnp.bfloat16
_F32 = jnp.float32


def _dot(a, b):
    return jnp.dot(a, b, preferred_element_type=_F32)


def _dot_nt(a, b):
    return lax.dot_general(a, b, (((1,), (1,)), ((), ())), preferred_element_type=_F32)


def _rms(x, g):
    return x * lax.rsqrt(jnp.mean(x * x, axis=-1, keepdims=True) + NORM_EPS) * g


def _proj_kernel(x_ref, xm_ref, cos_ref, sin_ref, cost_ref, sint_ref, g_attn_ref, g_q_ref, g_kv_ref, w1_ref, w2t_ref,
                 w3k_ref, w3vt_ref, wqvt_ref, kna_ref, qt_ref, k_ref, vt_ref, qnat_ref, vnat_ref):
    x = jnp.where(pl.program_id(0) == pl.num_programs(0) - 1, xm_ref[...], x_ref[...])
    a = _rms(x, g_attn_ref[...]).astype(_BF)
    p = _dot(a, w1_ref[...])
    kna_ref[...] = p[:, :NA_WIDTH].astype(_BF)
    o = NA_WIDTH
    cqn = _rms(p[:, o:o + Q_LORA], g_q_ref[...]).astype(_BF)
    o += Q_LORA
    ckvn = _rms(p[:, o:o + KV_LORA], g_kv_ref[...]).astype(_BF)
    o += KV_LORA
    kr = p[:, o:o + LANES] * cos_ref[...] + p[:, o + LANES:o + 2 * LANES] * sin_ref[...]
    k3 = _dot(ckvn, w3k_ref[...])
    q2t = _dot_nt(w2t_ref[...], cqn)
    vt = _dot_nt(w3vt_ref[...], ckvn)
    cos_t = cost_ref[...]
    sin_t = sint_ref[...]
    hw = MLA_HEADS * LANES
    ones_row = jnp.where(lax.broadcasted_iota(jnp.int32, (V_EXT - V_HEAD, x_ref.shape[0]), 0) == 0, 1.0, 0.0)
    for h in range(MLA_HEADS):
        sl = slice(h * LANES, (h + 1) * LANES)
        qt_ref[h] = (q2t[sl] * cos_t + q2t[hw + h * LANES:hw + (h + 1) * LANES] * sin_t).astype(_BF)
        k_ref[h] = (k3[:, sl] + kr).astype(_BF)
        vt_ref[h, :V_HEAD, :] = vt[h * V_HEAD:(h + 1) * V_HEAD].astype(_BF)
        vt_ref[h, V_HEAD:, :] = ones_row.astype(_BF)
    qvt = _dot_nt(wqvt_ref[...], a)
    zero_half = jnp.zeros((NA_HEAD_DIM, x_ref.shape[0]), _BF)
    for h in range(NA_HEADS):
        own = slice((h % 2) * NA_HEAD_DIM, (h % 2 + 1) * NA_HEAD_DIM)
        other = slice((1 - h % 2) * NA_HEAD_DIM, (2 - h % 2) * NA_HEAD_DIM)
        qnat_ref[h, own, :] = qvt[h * NA_HEAD_DIM:(h + 1) * NA_HEAD_DIM].astype(_BF)
        qnat_ref[h, other, :] = zero_half
        vnat_ref[h, :NA_HEAD_DIM, :] = qvt[NA_WIDTH + h * NA_HEAD_DIM:NA_WIDTH + (h + 1) * NA_HEAD_DIM].astype(_BF)
        vnat_ref[h, NA_HEAD_DIM:, :] = ones_row.astype(_BF)


def _proj(x, x_meta, seq, cos, sin, cos_t, sin_t, g_attn, g_q, g_kv, w1, w2t, w3k, w3vt, wqvt):
    n_grid_tiles = x.shape[0] // ROW_TILE
    np_rows = x.shape[0] + ROW_TILE
    tiles_per_seq = seq // ROW_TILE
    table_meta_tile = cos.shape[0] // ROW_TILE - 1
    table_tile = lambda i: jnp.where(i < n_grid_tiles, i % tiles_per_seq, table_meta_tile)
    row = lambda i: (table_tile(i), 0)
    col = lambda i: (0, table_tile(i))
    full = lambda i: (0, 0)
    w = lambda a: pl.BlockSpec(a.shape, full)
    return pl.pallas_call(
        _proj_kernel,
        grid=(np_rows // ROW_TILE,),
        in_specs=[pl.BlockSpec((ROW_TILE, D_MODEL), lambda i: (jnp.minimum(i, n_grid_tiles - 1), 0)), w(x_meta),
                  pl.BlockSpec((ROW_TILE, LANES), row),
                  pl.BlockSpec((ROW_TILE, LANES), row), pl.BlockSpec((LANES, ROW_TILE), col),
                  pl.BlockSpec((LANES, ROW_TILE), col), w(g_attn), w(g_q), w(g_kv), w(w1), w(w2t), w(w3k), w(w3vt),
                  w(wqvt)],
        out_specs=[pl.BlockSpec((ROW_TILE, NA_WIDTH), lambda i: (i, 0)),
                   pl.BlockSpec((MLA_HEADS, LANES, ROW_TILE), lambda i: (0, 0, i)),
                   pl.BlockSpec((MLA_HEADS, ROW_TILE, LANES), lambda i: (0, i, 0)),
                   pl.BlockSpec((MLA_HEADS, V_EXT, ROW_TILE), lambda i: (0, 0, i)),
                   pl.BlockSpec((NA_HEADS, LANES, ROW_TILE), lambda i: (0, 0, i)),
                   pl.BlockSpec((NA_HEADS, V_EXT, ROW_TILE), lambda i: (0, 0, i))],
        out_shape=[jax.ShapeDtypeStruct((np_rows, NA_WIDTH), _BF),
                   jax.ShapeDtypeStruct((MLA_HEADS, LANES, np_rows), _BF),
                   jax.ShapeDtypeStruct((MLA_HEADS, np_rows, LANES), _BF),
                   jax.ShapeDtypeStruct((MLA_HEADS, V_EXT, np_rows), _BF),
                   jax.ShapeDtypeStruct((NA_HEADS, LANES, np_rows), _BF),
                   jax.ShapeDtypeStruct((NA_HEADS, V_EXT, np_rows), _BF)],
        compiler_params=pltpu.CompilerParams(dimension_semantics=("parallel",), vmem_limit_bytes=VMEM_LIMIT),
        name="proj",
    )(x, x_meta, cos, sin, cos_t, sin_t, g_attn, g_q, g_kv, w1, w2t, w3k, w3vt, wqvt)


def _na_kernel(qt_ref, kp_ref, kc_ref, kn_ref, vtp_ref, vtc_ref, vtn_ref, km_ref, vmt_ref, bias_ref, ot_ref):
    ks = (kp_ref, kc_ref, kn_ref)
    vts = (vtp_ref, vtc_ref, vtn_ref)
    steps = [(h, c) for h in range(NA_HEADS) for c in range(3)]
    lanes = lambda h: slice((h // 2) * LANES, (h // 2 + 1) * LANES)
    scores = lambda h, c: _dot(ks[c][:, lanes(h)], qt_ref[h])
    meta_s = [_dot(km_ref[:, lanes(h)], qt_ref[h]) for h in range(NA_HEADS)]
    meta_m = [jnp.max(sm, axis=0, keepdims=True) for sm in meta_s]
    meta_acc = [_dot(vmt_ref[h], jnp.exp2(meta_s[h] - meta_m[h]).astype(_BF)) for h in range(NA_HEADS)]
    pending = [scores(*st) for st in steps[:NA_LOOKAHEAD]]
    for n, (h, c) in enumerate(steps):
        if n + NA_LOOKAHEAD < len(steps):
            pending.append(scores(*steps[n + NA_LOOKAHEAD]))
        s = bias_ref[h, c * NA_BLK:(c + 1) * NA_BLK, :] + pending.pop(0)
        if c == 0:
            m, acc = meta_m[h], meta_acc[h]
        m_new = jnp.maximum(m, jnp.max(s, axis=0, keepdims=True))
        p = jnp.exp2(s - m_new).astype(_BF)
        acc = acc * jnp.exp2(m - m_new) + _dot(vts[c][h], p)
        m = m_new
        if c == 2:
            ot_ref[h] = (acc[:NA_HEAD_DIM] / acc[NA_HEAD_DIM:NA_HEAD_DIM + 1]).astype(_BF)


def _na_attention(qt, k, vt, vmt, bias_t, batch, seq):
    nblk = seq // NA_BLK
    meta_blk0 = batch * seq // N_META
    near = lambda b, j, off: b * nblk + jnp.clip(j + off, 0, nblk - 1)
    k_spec = lambda off: pl.BlockSpec((NA_BLK, NA_WIDTH), lambda b, j: (near(b, j, off), 0))
    vt_spec = lambda off: pl.BlockSpec((NA_HEADS, V_EXT, NA_BLK), lambda b, j: (0, 0, near(b, j, off)))
    cls = lambda b, j: (jnp.where(j == 0, 0, jnp.where(j == nblk - 1, 2, 1)), 0, 0, 0)
    return pl.pallas_call(
        _na_kernel,
        grid=(batch, nblk),
        in_specs=[pl.BlockSpec((NA_HEADS, LANES, NA_BLK), lambda b, j: (0, 0, b * nblk + j)),
                  k_spec(-1), k_spec(0), k_spec(1), vt_spec(-1), vt_spec(0), vt_spec(1),
                  pl.BlockSpec((N_META, NA_WIDTH), lambda b, j: (meta_blk0 + b, 0)),
                  pl.BlockSpec((None, NA_HEADS, V_EXT, N_META), lambda b, j: (b, 0, 0, 0)),
                  pl.BlockSpec((None, NA_HEADS, 3 * NA_BLK, NA_BLK), cls)],
        out_specs=pl.BlockSpec((NA_HEADS, NA_HEAD_DIM, NA_BLK), lambda b, j: (0, 0, b * nblk + j)),
        out_shape=jax.ShapeDtypeStruct((NA_HEADS, NA_HEAD_DIM, batch * seq), _BF),
        compiler_params=pltpu.CompilerParams(dimension_semantics=("parallel", "arbitrary"),
                                             vmem_limit_bytes=VMEM_LIMIT),
        name="na_attn",
    )(qt, k, k, k, vt, vt, vt, k, vmt, bias_t)


def _na_meta_kernel(q_ref, k_ref, v_ref, o_ref):
    for h in range(NA_HEADS):
        sl = slice(h * NA_HEAD_DIM, (h + 1) * NA_HEAD_DIM)
        s = _dot_nt(q_ref[:, sl], k_ref[:, sl])
        p = jnp.exp2(s - jnp.max(s, axis=1, keepdims=True))
        l = jnp.sum(p, axis=1, keepdims=True)
        o_ref[:, sl] = (_dot(p.astype(_BF), v_ref[:, sl]) / l).astype(_BF)


def _na_meta_attention(q, k, v, batch):
    spec = pl.BlockSpec((N_META, NA_WIDTH), lambda b: (b, 0))
    return pl.pallas_call(
        _na_meta_kernel,
        grid=(batch,),
        in_specs=[spec, spec, spec],
        out_specs=spec,
        out_shape=jax.ShapeDtypeStruct((batch * N_META, NA_WIDTH), _BF),
        compiler_params=pltpu.CompilerParams(dimension_semantics=("parallel",)),
        name="na_meta_attn",
    )(q, k, v)


V_EXT = V_HEAD + 16
MLA_CHUNK = 256
MLA_LOOKAHEAD = 6
MLA_QSUB = 256
MLA_TQ = 512
MLA_TK = 4096


def _mla_kernel(qt_ref, k_ref, vt_ref, km_ref, vmt_ref, ot_ref, m_sc, acc_sc, *, chunk):
    j = pl.program_id(2)
    tq = qt_ref.shape[2]
    nchunks = k_ref.shape[1] // chunk

    @pl.when(j == 0)
    def _():
        def init_head(h, carry):
            s = _dot(km_ref[h], qt_ref[h])
            m = jnp.max(s, axis=0, keepdims=True)
            p = jnp.exp2(s - m).astype(_BF)
            acc_sc[h] = _dot(vmt_ref[h], p)
            m_sc[h] = jnp.broadcast_to(m, (8, tq))
            return carry
        lax.fori_loop(0, MLA_HEADS, init_head, 0)

    qsub = min(MLA_QSUB, tq)
    steps = [(qs, h, c) for qs in range(tq // qsub) for h in range(MLA_HEADS) for c in range(nchunks)]
    lanes = lambda qs: slice(qs * qsub, (qs + 1) * qsub)
    scores = lambda qs, h, c: _dot(k_ref[h, c * chunk:(c + 1) * chunk, :], qt_ref[h, :, lanes(qs)])
    pending = [scores(*st) for st in steps[:MLA_LOOKAHEAD]]
    for n, (qs, h, c) in enumerate(steps):
        if n + MLA_LOOKAHEAD < len(steps):
            pending.append(scores(*steps[n + MLA_LOOKAHEAD]))
        s = pending.pop(0)
        if c == 0:
            m = m_sc[h, :1, lanes(qs)]
            acc = acc_sc[h, :, lanes(qs)]
        m_new = jnp.maximum(m, jnp.max(s, axis=0, keepdims=True))
        p = jnp.exp2(s - m_new).astype(_BF)
        acc = acc * jnp.exp2(m - m_new) + _dot(vt_ref[h, :, c * chunk:(c + 1) * chunk], p)
        m = m_new
        if c == nchunks - 1:
            m_sc[h, :, lanes(qs)] = jnp.broadcast_to(m, (8, qsub))
            acc_sc[h, :, lanes(qs)] = acc

    @pl.when(j == pl.num_programs(2) - 1)
    def _():
        for h in range(MLA_HEADS):
            acc = acc_sc[h]
            ot_ref[h] = (acc[:V_HEAD] / acc[V_HEAD:V_HEAD + 1]).astype(_BF)


def _mla_attention(qt, k, vt, km, vmt, batch, seq, q_rows, q_blk0, q_stride, tq, tk):
    assert q_rows % tq == 0 and seq % tk == 0 and tk % MLA_CHUNK == 0
    nq = q_rows // tq
    nk = seq // tk
    return pl.pallas_call(
        functools.partial(_mla_kernel, chunk=min(MLA_CHUNK, tk)),
        grid=(batch, nq, nk),
        in_specs=[pl.BlockSpec((MLA_HEADS, LANES, tq), lambda b, i, j: (0, 0, q_blk0 + b * q_stride + i)),
                  pl.BlockSpec((MLA_HEADS, tk, LANES), lambda b, i, j: (0, b * nk + j, 0)),
                  pl.BlockSpec((MLA_HEADS, V_EXT, tk), lambda b, i, j: (0, 0, b * nk + j)),
                  pl.BlockSpec((None, MLA_HEADS, N_META, LANES), lambda b, i, j: (b, 0, 0, 0)),
                  pl.BlockSpec((None, MLA_HEADS, V_EXT, N_META), lambda b, i, j: (b, 0, 0, 0))],
        out_specs=pl.BlockSpec((MLA_HEADS, V_HEAD, tq), lambda b, i, j: (0, 0, b * nq + i)),
        out_shape=jax.ShapeDtypeStruct((MLA_HEADS, V_HEAD, batch * q_rows), _BF),
        scratch_shapes=[pltpu.VMEM((MLA_HEADS, 8, tq), _F32), pltpu.VMEM((MLA_HEADS, V_EXT, tq), _F32)],
        compiler_params=pltpu.CompilerParams(dimension_semantics=("parallel", "parallel", "arbitrary"),
                                             vmem_limit_bytes=VMEM_LIMIT),
        name="mla_attn",
    )(qt, k, vt, km, vmt)


def _outproj_kernel(x_ref, xm_ref, ong_ref, onm_ref, omg_ref, omm_ref, wo_na_ref, wo_mla_ref, g_ref, wr_hi_ref,
                    wr_lo_ref, h_ref, hn_ref, aff_ref, *, n_grid_tiles, n_valid):
    i = pl.program_id(0)
    is_meta = i == n_grid_tiles
    tn = lambda a_t, b: lax.dot_general(a_t, b, (((0,), (0,)), ((), ())), preferred_element_type=_F32)
    wr_hi = wr_hi_ref[...]
    subs = [slice(r0, r0 + OUT_SUB) for r0 in range(0, ROW_TILE, OUT_SUB)]
    hs = []
    for rows in subs:
        o_na_t = jnp.where(is_meta, onm_ref[:, rows], ong_ref[:, rows])
        o_mla_t = jnp.where(is_meta, omm_ref[:, rows], omg_ref[:, rows])
        x = jnp.where(is_meta, xm_ref[rows, :], x_ref[rows, :])
        hs.append(x + tn(o_na_t, wo_na_ref[...]) + tn(o_mla_t, wo_mla_ref[...]))
    split = []
    for rows, h in zip(subs, hs):
        h_ref[rows, :] = h
        hn = _rms(h, g_ref[...])
        hn_hi = hn.astype(_BF)
        hn_ref[rows, :] = hn_hi
        split.append((hn_hi, (hn - hn_hi.astype(_F32)).astype(_BF)))
    for rows, (hn_hi, hn_lo) in zip(subs, split):
        r0 = rows.start
        logits = _dot_nt(wr_hi, hn_hi) + _dot_nt(wr_hi, hn_lo) + _dot_nt(wr_lo_ref[...], hn_hi)
        e = jnp.exp(logits - jnp.max(logits, axis=0, keepdims=True))
        aff = e / jnp.sum(e, axis=0, keepdims=True)
        row = i * ROW_TILE + r0 + lax.broadcasted_iota(jnp.int32, aff.shape, 1)
        aff_ref[:, rows] = jnp.where(row < n_valid, aff, -1.0)


def _outproj(x, x_meta, o_na_g, o_na_m, o_mla_g, o_mla_m, wo_na, wo_mla, g_ffn, wr_hi, wr_lo, n_valid):
    n_grid_tiles = x.shape[0] // ROW_TILE
    n_tiles = n_grid_tiles + 1
    np_rows = n_tiles * ROW_TILE
    row = lambda i: (i, 0)
    grid_col = lambda i: (0, jnp.minimum(i, n_grid_tiles - 1))
    full = lambda i: (0, 0)
    w = lambda a: pl.BlockSpec(a.shape, full)
    return pl.pallas_call(
        functools.partial(_outproj_kernel, n_grid_tiles=n_grid_tiles, n_valid=n_valid),
        grid=(n_tiles,),
        in_specs=[pl.BlockSpec((ROW_TILE, D_MODEL), lambda i: (jnp.minimum(i, n_grid_tiles - 1), 0)), w(x_meta),
                  pl.BlockSpec((NA_WIDTH, ROW_TILE), grid_col), w(o_na_m),
                  pl.BlockSpec((MLA_WIDTH, ROW_TILE), grid_col), w(o_mla_m),
                  w(wo_na), w(wo_mla), w(g_ffn), w(wr_hi), w(wr_lo)],
        out_specs=[pl.BlockSpec((ROW_TILE, D_MODEL), row), pl.BlockSpec((ROW_TILE, D_MODEL), row),
                   pl.BlockSpec((None, N_EXPERTS, ROW_TILE), lambda i: (i, 0, 0))],
        out_shape=[jax.ShapeDtypeStruct((np_rows, D_MODEL), _F32), jax.ShapeDtypeStruct((np_rows, D_MODEL), _BF),
                   jax.ShapeDtypeStruct((n_tiles, N_EXPERTS, ROW_TILE), _F32)],
        compiler_params=pltpu.CompilerParams(dimension_semantics=("parallel",), vmem_limit_bytes=VMEM_LIMIT),
        name="outproj_router",
    )(x, x_meta, o_na_g, o_na_m, o_mla_g, o_mla_m, wo_na, wo_mla, g_ffn, wr_hi, wr_lo)


XE_ALIGN = 16
SLOT_W = 64
WIN = 64


def _route_kernel(aff_ref, slot_ref, lo_ref, *, cap):
    n_tiles = aff_ref.shape[0]
    bits = lambda: pltpu.bitcast(aff_ref[...], jnp.int32)

    def bit_step(i, cur):
        cand = cur | jnp.left_shift(jnp.int32(1), 30 - i)
        cnt = jnp.sum((bits() >= cand).astype(jnp.int32), axis=(0, 2), keepdims=True)
        return jnp.where(cnt >= cap, cand, cur)
    thr = lax.fori_loop(0, 31, bit_step, jnp.zeros((1, N_EXPERTS, 1), jnp.int32))
    n_gt = jnp.sum((bits() > thr).astype(jnp.int32), axis=(0, 2), keepdims=True)
    need = (cap - n_gt)[0].astype(_F32)
    thr = thr[0]
    r = lax.broadcasted_iota(jnp.int32, (ROW_TILE, ROW_TILE), 0)
    c = lax.broadcasted_iota(jnp.int32, (ROW_TILE, ROW_TILE), 1)
    tri = jnp.where(r <= c, 1.0, 0.0).astype(_BF)
    as_bf = lambda m: jnp.where(m, 1.0, 0.0).astype(_BF)

    def tile_step(t, carry):
        tie_before, kept_before = carry
        b = pltpu.bitcast(aff_ref[t], jnp.int32)
        eq = b == thr
        tie_rank = _dot(as_bf(eq), tri) + tie_before
        keep = (b > thr) | (eq & (tie_rank <= need))
        kept_incl = _dot(as_bf(keep), tri) + kept_before
        slot_ref[t] = jnp.where(keep, kept_incl - 1.0, -1.0).astype(jnp.int32)
        lo_ref[t] = jnp.broadcast_to(kept_before, (N_EXPERTS, LANES)).astype(jnp.int32)
        return tie_rank[:, ROW_TILE - 1:], kept_incl[:, ROW_TILE - 1:]
    zero = jnp.zeros((N_EXPERTS, 1), _F32)
    lax.fori_loop(0, n_tiles, tile_step, (zero, zero))


def _route(aff, cap):
    n_tiles = aff.shape[0]
    full = lambda shape: pl.BlockSpec(shape, lambda i: (0, 0, 0))
    return pl.pallas_call(
        functools.partial(_route_kernel, cap=cap),
        grid=(1,),
        in_specs=[full(aff.shape)],
        out_specs=[full(aff.shape), full((n_tiles, N_EXPERTS, LANES))],
        out_shape=[jax.ShapeDtypeStruct(aff.shape, jnp.int32),
                   jax.ShapeDtypeStruct((n_tiles, N_EXPERTS, LANES), jnp.int32)],
        compiler_params=pltpu.CompilerParams(dimension_semantics=("arbitrary",), vmem_limit_bytes=VMEM_LIMIT),
        name="route",
    )(aff)


def _tile_max(lo_ref, i, start_of):
    m = lo_ref[(i + 1) * N_EXPERTS] - start_of(0)
    for e in range(1, N_EXPERTS):
        m = jnp.maximum(m, lo_ref[(i + 1) * N_EXPERTS + e] - start_of(e))
    return m


def _dispatch_kernel(lo_ref, hn_ref, slot_ref, xe_ref, stage, stage_extra, carry, sem, sem_extra, *, zero_from,
                     zero_rows):
    i = pl.program_id(0)
    s = i % 2
    base = lambda e: (lo_ref[i * N_EXPERTS + e] // XE_ALIGN) * XE_ALIGN
    next_base = lambda e: (lo_ref[(i + 1) * N_EXPERTS + e] // XE_ALIGN) * XE_ALIGN

    def compact(k, dst):
        w = lax.broadcasted_iota(jnp.int32, (SLOT_W, ROW_TILE), 0) + k * SLOT_W
        onehot = jnp.concatenate(
            [jnp.where(slot_ref[e:e + 1, :] - base(e) == w, 1.0, 0.0).astype(_BF) for e in range(N_EXPERTS)], axis=0)
        dst[...] = _dot(onehot, hn_ref[...]).astype(_BF)

    def copy(e, src, k, sm):
        dst_row = pl.multiple_of(base(e) + k * SLOT_W, XE_ALIGN)
        return pltpu.make_async_copy(src.at[pl.ds(e * SLOT_W, SLOT_W)], xe_ref.at[e, pl.ds(dst_row, SLOT_W)], sm)

    def save_carry(k, src):
        for e in range(N_EXPERTS):
            g = next_base(e) - base(e) - k * SLOT_W

            @pl.when((g >= 0) & (g < SLOT_W))
            def _():
                carry[e] = src[pl.ds(pl.multiple_of(e * SLOT_W + g, XE_ALIGN), XE_ALIGN), :]

    @pl.when(i == 0)
    def _():
        carry[...] = jnp.zeros(carry.shape, _BF)
        stage_extra[...] = jnp.zeros(stage_extra.shape, _BF)
        zero_copies = [pltpu.make_async_copy(stage_extra.at[pl.ds(0, zero_rows)],
                                             xe_ref.at[e, pl.ds(zero_from, zero_rows)], sem_extra)
                       for e in range(N_EXPERTS)]
        for cp in zero_copies:
            cp.start()
        for cp in zero_copies:
            cp.wait()

    compact(0, stage.at[s])
    for e in range(N_EXPERTS):
        stage[s, e * SLOT_W:e * SLOT_W + XE_ALIGN, :] += carry[e]
    save_carry(0, stage.at[s])

    @pl.when(i > 0)
    def _():
        for e in range(N_EXPERTS):
            copy(e, stage.at[1 - s], 0, sem.at[1 - s]).wait()

    for e in range(N_EXPERTS):
        copy(e, stage.at[s], 0, sem.at[s]).start()

    def extra_pass(k, c):
        compact(k, stage_extra)
        save_carry(k, stage_extra)
        for wait in (False, True):
            for e in range(N_EXPERTS):
                @pl.when(lo_ref[(i + 1) * N_EXPERTS + e] - base(e) >= k * SLOT_W)
                def _():
                    cp = copy(e, stage_extra, k, sem_extra)
                    cp.wait() if wait else cp.start()
        return c
    lax.fori_loop(1, _tile_max(lo_ref, i, base) // SLOT_W + 1, extra_pass, 0)

    @pl.when(i == pl.num_programs(0) - 1)
    def _():
        for e in range(N_EXPERTS):
            copy(e, stage.at[s], 0, sem.at[s]).wait()


def _dispatch(lo, hn, slot, cap, cap_pad):
    n_tiles = slot.shape[0]
    zero_from = cap // XE_ALIGN * XE_ALIGN
    zero_rows = cap_pad + SLOT_W - zero_from
    assert zero_rows <= N_EXPERTS * SLOT_W and SLOT_W % XE_ALIGN == 0 and cap_pad % XE_ALIGN == 0
    return pl.pallas_call(
        functools.partial(_dispatch_kernel, zero_from=zero_from, zero_rows=zero_rows),
        grid_spec=pltpu.PrefetchScalarGridSpec(
            num_scalar_prefetch=1,
            grid=(n_tiles,),
            in_specs=[pl.BlockSpec((ROW_TILE, D_MODEL), lambda i, lo: (i, 0)),
                      pl.BlockSpec((None, N_EXPERTS, ROW_TILE), lambda i, lo: (i, 0, 0))],
            out_specs=pl.BlockSpec(memory_space=pl.ANY),
            scratch_shapes=[pltpu.VMEM((2, N_EXPERTS * SLOT_W, D_MODEL), _BF),
                            pltpu.VMEM((N_EXPERTS * SLOT_W, D_MODEL), _BF),
                            pltpu.VMEM((N_EXPERTS, XE_ALIGN, D_MODEL), _BF),
                            pltpu.SemaphoreType.DMA((2,)), pltpu.SemaphoreType.DMA(())]),
        out_shape=jax.ShapeDtypeStruct((N_EXPERTS, cap_pad + SLOT_W, D_MODEL), _BF),
        compiler_params=pltpu.CompilerParams(dimension_semantics=("arbitrary",), vmem_limit_bytes=VMEM_LIMIT),
        name="dispatch",
    )(lo, hn, slot)


def _expert_kernel(x_ref, wg_ref, wu_ref, wd_ref, y_ref):
    x = x_ref[...]
    g = _dot(x, wg_ref[...])
    u = _dot(x, wu_ref[...])
    hid = (g * jax.nn.sigmoid(g) * u).astype(_BF)
    y_ref[...] = _dot(hid, wd_ref[...]).astype(_BF)


def _experts(xe, wg, wu, wd, cap_pad, tile):
    n_e = xe.shape[0]
    wspec = lambda a: pl.BlockSpec((None,) + a.shape[1:], lambda e, t: (e, 0, 0))
    return pl.pallas_call(
        _expert_kernel,
        grid=(n_e, cap_pad // tile),
        in_specs=[pl.BlockSpec((None, tile, D_MODEL), lambda e, t: (e, t, 0)), wspec(wg), wspec(wu), wspec(wd)],
        out_specs=pl.BlockSpec((None, tile, D_MODEL), lambda e, t: (e, t, 0)),
        out_shape=jax.ShapeDtypeStruct((n_e, cap_pad, D_MODEL), _BF),
        compiler_params=pltpu.CompilerParams(dimension_semantics=("parallel", "arbitrary"),
                                             vmem_limit_bytes=VMEM_LIMIT),
        name="experts",
    )(xe, wg, wu, wd)


def _combine_kernel(lo_ref, h_ref, slot_ref, aff_ref, g_ref, ye_ref, o_ref, win, win_extra, f_acc, sem, sem_extra, *,
                    cap_pad):
    i = pl.program_id(0)
    s = i % 2
    first = lambda t, e: (lo_ref[t * N_EXPERTS + e] // 16) * 16

    def start_row(t, e, k):
        return pl.multiple_of(jnp.minimum(first(t, e) + k * WIN, cap_pad - WIN), 16)

    def window_copies(t, k, buf, sm):
        return [pltpu.make_async_copy(ye_ref.at[e, pl.ds(start_row(t, e, k), WIN)], buf.at[pl.ds(e * WIN, WIN)], sm)
                for e in range(N_EXPERTS)]

    def weights(k):
        w = lax.broadcasted_iota(jnp.int32, (WIN, ROW_TILE), 0)
        parts = []
        for e in range(N_EXPERTS):
            slot = slot_ref[e:e + 1, :]
            hit = (slot == w + start_row(i, e, k)) & (slot >= first(i, e) + k * WIN)
            parts.append(jnp.where(hit, aff_ref[e:e + 1, :], 0.0).astype(_BF))
        return jnp.concatenate(parts, axis=0)

    scatter = lambda wts, rows: lax.dot_general(wts, rows, (((0,), (0,)), ((), ())), preferred_element_type=_F32)

    @pl.when(i == 0)
    def _():
        for cp in window_copies(0, 0, win.at[0], sem.at[0]):
            cp.start()

    @pl.when(i + 1 < pl.num_programs(0))
    def _():
        for cp in window_copies(i + 1, 0, win.at[1 - s], sem.at[1 - s]):
            cp.start()

    wts = weights(0)
    for cp in window_copies(i, 0, win.at[s], sem.at[s]):
        cp.wait()
    f_acc[...] = scatter(wts, win[s])

    def extra_pass(k, c):
        copies = window_copies(i, k, win_extra, sem_extra)
        for cp in copies:
            cp.start()
        wts_k = weights(k)
        for cp in copies:
            cp.wait()
        f_acc[...] += scatter(wts_k, win_extra[...])
        return c
    reach = _tile_max(lo_ref, i, lambda e: first(i, e))
    lax.fori_loop(1, (reach + WIN - 1) // WIN, extra_pass, 0)
    o_ref[...] = _rms(h_ref[...] + f_acc[...], g_ref[...])


def _combine(lo, h, slot, aff, g_final, ye, n_rows):
    cap_pad = ye.shape[1]
    assert cap_pad % 16 == 0 and WIN % 16 == 0 and cap_pad >= WIN
    tile3 = pl.BlockSpec((None, N_EXPERTS, ROW_TILE), lambda i, lo: (i, 0, 0))
    return pl.pallas_call(
        functools.partial(_combine_kernel, cap_pad=cap_pad),
        grid_spec=pltpu.PrefetchScalarGridSpec(
            num_scalar_prefetch=1,
            grid=(n_rows // ROW_TILE,),
            in_specs=[pl.BlockSpec((ROW_TILE, D_MODEL), lambda i, lo: (i, 0)), tile3, tile3,
                      pl.BlockSpec(g_final.shape, lambda i, lo: (0, 0)), pl.BlockSpec(memory_space=pl.ANY)],
            out_specs=pl.BlockSpec((ROW_TILE, D_MODEL), lambda i, lo: (i, 0)),
            scratch_shapes=[pltpu.VMEM((2, N_EXPERTS * WIN, D_MODEL), _BF), pltpu.VMEM((N_EXPERTS * WIN, D_MODEL), _BF),
                            pltpu.VMEM((ROW_TILE, D_MODEL), _F32),
                            pltpu.SemaphoreType.DMA((2,)), pltpu.SemaphoreType.DMA(())]),
        out_shape=jax.ShapeDtypeStruct((n_rows, D_MODEL), _F32),
        compiler_params=pltpu.CompilerParams(dimension_semantics=("arbitrary",), vmem_limit_bytes=VMEM_LIMIT),
        name="combine_norm",
    )(lo, h, slot, aff, g_final, ye)


def _rotate_half_cols(w):
    half = QK_ROPE // 2
    return jnp.concatenate([-w[..., half:], w[..., :half]], axis=-1)


def _prepare_weights(w_in, g_attn, g_q, w_uq, g_kv, w_ukv, w_o, g_ffn, w_router, w_gate, w_up, w_down, na_rpb):
    na_scale = 1.0 / math.sqrt(NA_HEAD_DIM)
    c0, c2 = 3 * NA_WIDTH, 3 * NA_WIDTH + Q_LORA + KV_LORA
    w_kr = w_in[:, c2:]
    zpad = lambda n: jnp.zeros((D_MODEL, n), _F32)
    slot = lambda w: jnp.concatenate([zpad(QK_NOPE), w, zpad(LANES - QK_NOPE - QK_ROPE)], axis=1)
    w1 = jnp.concatenate([w_in[:, NA_WIDTH:2 * NA_WIDTH], w_in[:, c0:c2], slot(w_kr),
                          slot(_rotate_half_cols(w_kr))], axis=1).astype(_BF)
    q_scale = LOG2E / math.sqrt(QK_NOPE + QK_ROPE)
    wq = (w_uq * q_scale).reshape(Q_LORA, MLA_HEADS, QK_NOPE + QK_ROPE)
    pad = LANES - QK_NOPE - QK_ROPE
    plain = jnp.pad(wq, ((0, 0), (0, 0), (0, pad)))
    rot = jnp.pad(_rotate_half_cols(wq[..., QK_NOPE:]), ((0, 0), (0, 0), (QK_NOPE, pad)))
    w2t = jnp.concatenate([plain.reshape(Q_LORA, -1), rot.reshape(Q_LORA, -1)], axis=1).T.astype(_BF)
    wkv = w_ukv.reshape(KV_LORA, MLA_HEADS, QK_NOPE + V_HEAD)
    w3k = jnp.pad(wkv[..., :QK_NOPE], ((0, 0), (0, 0), (0, LANES - QK_NOPE))).reshape(KV_LORA, -1).astype(_BF)
    w3vt = wkv[..., QK_NOPE:].reshape(KV_LORA, -1).T.astype(_BF)
    wr = w_router.T
    wr_hi = wr.astype(_BF)
    wr_lo = (wr - wr_hi.astype(_F32)).astype(_BF)
    return dict(
        w1=w1, w2t=w2t, w3k=w3k, w3vt=w3vt,
        g_attn=g_attn[None], g_q=g_q[None], g_kv=g_kv[None], g_ffn=g_ffn[None],
        wo_na=w_o[:NA_WIDTH].astype(_BF), wo_mla=w_o[NA_WIDTH:].astype(_BF),
        wr_hi=wr_hi, wr_lo=wr_lo,
        wg=w_gate.astype(_BF), wu=w_up.astype(_BF), wd=w_down.astype(_BF),
        wqvt=jnp.concatenate([w_in[:, :NA_WIDTH] * (na_scale * LOG2E), w_in[:, 2 * NA_WIDTH:3 * NA_WIDTH]],
                             axis=1).T.astype(_BF),
        na_bias_t=_na_bias_table(na_rpb),
    )


def _na_bias_table(rpb):
    ql = np.arange(NA_BLK)
    kl = np.arange(3 * NA_BLK)
    ri, qc = ql // GRID_W, ql % GRID_W
    kj, kc = kl // GRID_W, kl % GRID_W
    qcs = np.clip(qc - WIN_C // 2, 0, GRID_W - WIN_C)
    col_ok = (kc[None, :] >= qcs[:, None]) & (kc[None, :] < qcs[:, None] + WIN_C)
    lo = np.stack([np.full(NA_BLK, NA_ROWS), ri, np.zeros(NA_BLK, np.int64)])
    row_ok = (kj[None, None, :] >= lo[:, :, None]) & (kj[None, None, :] < lo[:, :, None] + WIN_R)
    ok = row_ok & col_ok[None]
    rows_q, rows_k, cols = np.arange(NA_ROWS), np.arange(3 * NA_ROWS), np.arange(GRID_W)
    dr = np.clip(rows_k[None, :] - rows_q[:, None] + (WIN_R - 1) - NA_ROWS, 0, 2 * WIN_R - 2)
    dc = np.clip(cols[None, :] - cols[:, None] + (WIN_C - 1), 0, 2 * WIN_C - 2)
    pick_r = (dr[..., None] == np.arange(2 * WIN_R - 1)).astype(np.float32)
    pick_c = (dc[..., None] == np.arange(2 * WIN_C - 1)).astype(np.float32)
    vals = jnp.einsum("hrc,ijr,qkc->hiqjk", rpb.astype(_F32), pick_r, pick_c, precision=lax.Precision.HIGHEST)
    vals = vals.reshape(NA_HEADS, NA_BLK, 3 * NA_BLK)
    return jnp.where(jnp.asarray(ok)[:, None], vals[None] * LOG2E, NEG).transpose(0, 1, 3, 2)


def _rope_tables(seq):
    half = QK_ROPE // 2
    inv_freq = ROPE_THETA ** (-jnp.arange(half, dtype=_F32) * 2.0 / QK_ROPE)
    pos = jnp.concatenate([jnp.arange(seq, dtype=_F32) + N_META,
                           jnp.tile(jnp.arange(N_META, dtype=_F32), ROW_TILE // N_META)])
    n = seq + ROW_TILE
    pad = LANES - QK_NOPE - QK_ROPE

    def lay(a, nope_fill, axis):
        shape = lambda k: (n, k) if axis == 1 else (k, n)
        return jnp.concatenate([jnp.full(shape(QK_NOPE), nope_fill, _F32), a, a, jnp.zeros(shape(pad), _F32)], axis=axis)
    ang = pos[:, None] * inv_freq[None, :]
    cos, sin = jnp.cos(ang), jnp.sin(ang)
    return lay(cos, 1.0, 1), lay(sin, 0.0, 1), lay(cos.T, 1.0, 0), lay(sin.T, 0.0, 0)


def _expert_tile(cap):
    if cap <= 1024:
        return -(-cap // 16) * 16
    return min(range(512, 1025, 16), key=lambda t: (-(-cap // t) * t - cap, -t))


def _run_group(x, meta_tokens, prm, rope, g_final, tq, tk):
    batch, seq, _ = x.shape
    n_grid = batch * seq
    n_valid = n_grid + batch * N_META
    assert seq % NA_BLK == 0 and seq // GRID_W >= WIN_R and batch * N_META <= ROW_TILE
    meta = jnp.broadcast_to(meta_tokens[None], (batch, N_META, D_MODEL)).reshape(batch * N_META, D_MODEL)
    xg = x.reshape(n_grid, D_MODEL)
    xm = jnp.pad(meta, ((0, ROW_TILE - batch * N_META), (0, 0)))
    assert rope[0].shape[0] >= seq + ROW_TILE
    kna, qt, k3, vt, qnat, vnat = _proj(xg, xm, seq, *rope, prm["g_attn"], prm["g_q"], prm["g_kv"],
                                        prm["w1"], prm["w2t"], prm["w3k"], prm["w3vt"], prm["wqvt"])
    n_meta = batch * N_META
    meta_cols = slice(n_grid, n_grid + n_meta)
    vmt_na = vnat[:, :, meta_cols].reshape(NA_HEADS, V_EXT, batch, N_META).transpose(2, 0, 1, 3)
    o_na_g = _na_attention(qnat, kna, vnat, vmt_na, prm["na_bias_t"], batch, seq).reshape(NA_WIDTH, n_grid)
    q_own = jnp.stack([qnat[h, (h % 2) * NA_HEAD_DIM:(h % 2 + 1) * NA_HEAD_DIM, meta_cols] for h in range(NA_HEADS)])
    rows_of = lambda t: t.transpose(2, 0, 1).reshape(n_meta, NA_WIDTH)
    o_na_m = _na_meta_attention(rows_of(q_own), kna[meta_cols], rows_of(vnat[:, :NA_HEAD_DIM, meta_cols]), batch)
    km = k3[:, n_grid:n_grid + n_meta].reshape(MLA_HEADS, batch, N_META, LANES).transpose(1, 0, 2, 3)
    vmt = vt[:, :, n_grid:n_grid + n_meta].reshape(MLA_HEADS, V_EXT, batch, N_META).transpose(2, 0, 1, 3)
    tk = min(tk, seq)
    ot_g = _mla_attention(qt, k3, vt, km, vmt, batch, seq, seq, 0, seq // tq, tq, tk)
    ot_m = _mla_attention(qt, k3, vt, km, vmt, batch, seq, ROW_TILE, n_grid // ROW_TILE, 0, ROW_TILE, tk)
    ot_m = ot_m.reshape(MLA_WIDTH, batch, ROW_TILE)
    ot_m = jnp.concatenate([ot_m[:, b, b * N_META:(b + 1) * N_META] for b in range(batch)], axis=1)
    ot_m = jnp.pad(ot_m, ((0, 0), (0, ROW_TILE - n_meta)))
    o_na_m = jnp.pad(o_na_m.T, ((0, 0), (0, ROW_TILE - n_meta)))
    h, hn, aff = _outproj(xg, xm, o_na_g, o_na_m, ot_g.reshape(MLA_WIDTH, n_grid), ot_m, prm["wo_na"], prm["wo_mla"],
                          prm["g_ffn"], prm["wr_hi"], prm["wr_lo"], n_valid)
    cap = EC_CAPACITY * n_valid // N_EXPERTS
    tile = _expert_tile(cap)
    cap_pad = -(-cap // tile) * tile
    slot, lo3 = _route(aff, cap)
    lo = jnp.concatenate([lo3[:, :, 0], jnp.full((1, N_EXPERTS), cap, jnp.int32)]).reshape(-1)
    xe = _dispatch(lo, hn, slot, cap, cap_pad)
    ye = _experts(xe, prm["wg"], prm["wu"], prm["wd"], cap_pad, tile)
    return _combine(lo, h, slot, aff, g_final[None], ye, n_grid).reshape(batch, seq, D_MODEL)


def kernel(x_prompt, x_sample, meta_tokens, g_attn, w_in, na_rpb, g_q, w_uq, g_kv, w_ukv, w_o, g_ffn, w_router,
           w_gate, w_up, w_down, g_final):
    prm = _prepare_weights(w_in[0], g_attn[0], g_q[0], w_uq[0], g_kv[0], w_ukv[0], w_o[0], g_ffn[0], w_router[0],
                           w_gate[0], w_up[0], w_down[0], na_rpb[0])
    rope = _rope_tables(max(x_prompt.shape[1], x_sample.shape[1]))
    y_prompt = _run_group(x_prompt, meta_tokens, prm, rope, g_final, MLA_TQ, MLA_TK)
    y_sample = _run_group(x_sample, meta_tokens, prm, rope, g_final, MLA_TQ, MLA_TK)
    return (y_prompt, y_sample)
```

```python
import functools
import math

import jax
import jax.numpy as jnp
import numpy as np
from jax import lax
from jax.experimental import pallas as pl
from jax.experimental.pallas import tpu as pltpu

D_MODEL = 1024
GRID_W = 64
N_META = 16
WIN_R = 8
WIN_C = 16
NA_HEADS = 8
NA_HEAD_DIM = 64
MLA_HEADS = 8
QK_NOPE = 64
QK_ROPE = 32
V_HEAD = 64
Q_LORA = 768
KV_LORA = 256
ROPE_THETA = 10000.0
N_EXPERTS = 16
EC_CAPACITY = 2
D_EXPERT = 2048
NORM_EPS = 1e-6
NA_WIDTH = NA_HEADS * NA_HEAD_DIM
MLA_WIDTH = MLA_HEADS * V_HEAD

LANES = 128
ROW_TILE = 256
OUT_SUB = 128
NA_ROWS = 4
NA_BLK = NA_ROWS * GRID_W
NA_LOOKAHEAD = 5
NEG = -1e30
LOG2E = 1.4426950408889634
VMEM_LIMIT = 56 * 1024 * 1024

_BF = jnp.bfloat16
_F32 = jnp.float32


def _dot(a, b):
    return jnp.dot(a, b, preferred_element_type=_F32)


def _dot_nt(a, b):
    return lax.dot_general(a, b, (((1,), (1,)), ((), ())), preferred_element_type=_F32)


def _rms(x, g):
    return x * lax.rsqrt(jnp.mean(x * x, axis=-1, keepdims=True) + NORM_EPS) * g


def _proj_kernel(x_ref, xm_ref, cos_ref, sin_ref, cost_ref, sint_ref, g_attn_ref, g_q_ref, g_kv_ref, w1_ref, w2t_ref,
                 w3k_ref, w3vt_ref, wqvt_ref, kna_ref, qt_ref, k_ref, vt_ref, qnat_ref, vnat_ref):
    x = jnp.where(pl.program_id(0) == pl.num_programs(0) - 1, xm_ref[...], x_ref[...])
    a = _rms(x, g_attn_ref[...]).astype(_BF)
    p = _dot(a, w1_ref[...])
    kna_ref[...] = p[:, :NA_WIDTH].astype(_BF)
    o = NA_WIDTH
    cqn = _rms(p[:, o:o + Q_LORA], g_q_ref[...]).astype(_BF)
    o += Q_LORA
    ckvn = _rms(p[:, o:o + KV_LORA], g_kv_ref[...]).astype(_BF)
    o += KV_LORA
    kr = p[:, o:o + LANES] * cos_ref[...] + p[:, o + LANES:o + 2 * LANES] * sin_ref[...]
    k3 = _dot(ckvn, w3k_ref[...])
    q2t = _dot_nt(w2t_ref[...], cqn)
    vt = _dot_nt(w3vt_ref[...], ckvn)
    cos_t = cost_ref[...]
    sin_t = sint_ref[...]
    hw = MLA_HEADS * LANES
    ones_row = jnp.where(lax.broadcasted_iota(jnp.int32, (V_EXT - V_HEAD, x_ref.shape[0]), 0) == 0, 1.0, 0.0)
    for h in range(MLA_HEADS):
        sl = slice(h * LANES, (h + 1) * LANES)
        qt_ref[h] = (q2t[sl] * cos_t + q2t[hw + h * LANES:hw + (h + 1) * LANES] * sin_t).astype(_BF)
        k_ref[h] = (k3[:, sl] + kr).astype(_BF)
        vt_ref[h, :V_HEAD, :] = vt[h * V_HEAD:(h + 1) * V_HEAD].astype(_BF)
        vt_ref[h, V_HEAD:, :] = ones_row.astype(_BF)
    qvt = _dot_nt(wqvt_ref[...], a)
    zero_half = jnp.zeros((NA_HEAD_DIM, x_ref.shape[0]), _BF)
    for h in range(NA_HEADS):
        own = slice((h % 2) * NA_HEAD_DIM, (h % 2 + 1) * NA_HEAD_DIM)
        other = slice((1 - h % 2) * NA_HEAD_DIM, (2 - h % 2) * NA_HEAD_DIM)
        qnat_ref[h, own, :] = qvt[h * NA_HEAD_DIM:(h + 1) * NA_HEAD_DIM].astype(_BF)
        qnat_ref[h, other, :] = zero_half
        vnat_ref[h, :NA_HEAD_DIM, :] = qvt[NA_WIDTH + h * NA_HEAD_DIM:NA_WIDTH + (h + 1) * NA_HEAD_DIM].astype(_BF)
        vnat_ref[h, NA_HEAD_DIM:, :] = ones_row.astype(_BF)


def _proj(x, x_meta, seq, cos, sin, cos_t, sin_t, g_attn, g_q, g_kv, w1, w2t, w3k, w3vt, wqvt):
    n_grid_tiles = x.shape[0] // ROW_TILE
    np_rows = x.shape[0] + ROW_TILE
    tiles_per_seq = seq // ROW_TILE
    table_meta_tile = cos.shape[0] // ROW_TILE - 1
    table_tile = lambda i: jnp.where(i < n_grid_tiles, i % tiles_per_seq, table_meta_tile)
    row = lambda i: (table_tile(i), 0)
    col = lambda i: (0, table_tile(i))
    full = lambda i: (0, 0)
    w = lambda a: pl.BlockSpec(a.shape, full)
    return pl.pallas_call(
        _proj_kernel,
        grid=(np_rows // ROW_TILE,),
        in_specs=[pl.BlockSpec((ROW_TILE, D_MODEL), lambda i: (jnp.minimum(i, n_grid_tiles - 1), 0)), w(x_meta),
                  pl.BlockSpec((ROW_TILE, LANES), row),
                  pl.BlockSpec((ROW_TILE, LANES), row), pl.BlockSpec((LANES, ROW_TILE), col),
                  pl.BlockSpec((LANES, ROW_TILE), col), w(g_attn), w(g_q), w(g_kv), w(w1), w(w2t), w(w3k), w(w3vt),
                  w(wqvt)],
        out_specs=[pl.BlockSpec((ROW_TILE, NA_WIDTH), lambda i: (i, 0)),
                   pl.BlockSpec((MLA_HEADS, LANES, ROW_TILE), lambda i: (0, 0, i)),
                   pl.BlockSpec((MLA_HEADS, ROW_TILE, LANES), lambda i: (0, i, 0)),
                   pl.BlockSpec((MLA_HEADS, V_EXT, ROW_TILE), lambda i: (0, 0, i)),
                   pl.BlockSpec((NA_HEADS, LANES, ROW_TILE), lambda i: (0, 0, i)),
                   pl.BlockSpec((NA_HEADS, V_EXT, ROW_TILE), lambda i: (0, 0, i))],
        out_shape=[jax.ShapeDtypeStruct((np_rows, NA_WIDTH), _BF),
                   jax.ShapeDtypeStruct((MLA_HEADS, LANES, np_rows), _BF),
                   jax.ShapeDtypeStruct((MLA_HEADS, np_rows, LANES), _BF),
                   jax.ShapeDtypeStruct((MLA_HEADS, V_EXT, np_rows), _BF),
                   jax.ShapeDtypeStruct((NA_HEADS, LANES, np_rows), _BF),
                   jax.ShapeDtypeStruct((NA_HEADS, V_EXT, np_rows), _BF)],
        compiler_params=pltpu.CompilerParams(dimension_semantics=("parallel",), vmem_limit_bytes=VMEM_LIMIT),
        name="proj",
    )(x, x_meta, cos, sin, cos_t, sin_t, g_attn, g_q, g_kv, w1, w2t, w3k, w3vt, wqvt)


def _na_kernel(qt_ref, kp_ref, kc_ref, kn_ref, vtp_ref, vtc_ref, vtn_ref, km_ref, vmt_ref, bias_ref, ot_ref):
    ks = (kp_ref, kc_ref, kn_ref)
    vts = (vtp_ref, vtc_ref, vtn_ref)
    steps = [(h, c) for h in range(NA_HEADS) for c in range(3)]
    lanes = lambda h: slice((h // 2) * LANES, (h // 2 + 1) * LANES)
    scores = lambda h, c: _dot(ks[c][:, lanes(h)], qt_ref[h])
    meta_s = [_dot(km_ref[:, lanes(h)], qt_ref[h]) for h in range(NA_HEADS)]
    meta_m = [jnp.max(sm, axis=0, keepdims=True) for sm in meta_s]
    meta_acc = [_dot(vmt_ref[h], jnp.exp2(meta_s[h] - meta_m[h]).astype(_BF)) for h in range(NA_HEADS)]
    pending = [scores(*st) for st in steps[:NA_LOOKAHEAD]]
    for n, (h, c) in enumerate(steps):
        if n + NA_LOOKAHEAD < len(steps):
            pending.append(scores(*steps[n + NA_LOOKAHEAD]))
        s = bias_ref[h, c * NA_BLK:(c + 1) * NA_BLK, :] + pending.pop(0)
        if c == 0:
            m, acc = meta_m[h], meta_acc[h]
        m_new = jnp.maximum(m, jnp.max(s, axis=0, keepdims=True))
        p = jnp.exp2(s - m_new).astype(_BF)
        acc = acc * jnp.exp2(m - m_new) + _dot(vts[c][h], p)
        m = m_new
        if c == 2:
            ot_ref[h] = (acc[:NA_HEAD_DIM] / acc[NA_HEAD_DIM:NA_HEAD_DIM + 1]).astype(_BF)


def _na_attention(qt, k, vt, vmt, bias_t, batch, seq):
    nblk = seq // NA_BLK
    meta_blk0 = batch * seq // N_META
    near = lambda b, j, off: b * nblk + jnp.clip(j + off, 0, nblk - 1)
    k_spec = lambda off: pl.BlockSpec((NA_BLK, NA_WIDTH), lambda b, j: (near(b, j, off), 0))
    vt_spec = lambda off: pl.BlockSpec((NA_HEADS, V_EXT, NA_BLK), lambda b, j: (0, 0, near(b, j, off)))
    cls = lambda b, j: (jnp.where(j == 0, 0, jnp.where(j == nblk - 1, 2, 1)), 0, 0, 0)
    return pl.pallas_call(
        _na_kernel,
        grid=(batch, nblk),
        in_specs=[pl.BlockSpec((NA_HEADS, LANES, NA_BLK), lambda b, j: (0, 0, b * nblk + j)),
                  k_spec(-1), k_spec(0), k_spec(1), vt_spec(-1), vt_spec(0), vt_spec(1),
                  pl.BlockSpec((N_META, NA_WIDTH), lambda b, j: (meta_blk0 + b, 0)),
                  pl.BlockSpec((None, NA_HEADS, V_EXT, N_META), lambda b, j: (b, 0, 0, 0)),
                  pl.BlockSpec((None, NA_HEADS, 3 * NA_BLK, NA_BLK), cls)],
        out_specs=pl.BlockSpec((NA_HEADS, NA_HEAD_DIM, NA_BLK), lambda b, j: (0, 0, b * nblk + j)),
        out_shape=jax.ShapeDtypeStruct((NA_HEADS, NA_HEAD_DIM, batch * seq), _BF),
        compiler_params=pltpu.CompilerParams(dimension_semantics=("parallel", "arbitrary"),
                                             vmem_limit_bytes=VMEM_LIMIT),
        name="na_attn",
    )(qt, k, k, k, vt, vt, vt, k, vmt, bias_t)


def _na_meta_kernel(q_ref, k_ref, v_ref, o_ref):
    for h in range(NA_HEADS):
        sl = slice(h * NA_HEAD_DIM, (h + 1) * NA_HEAD_DIM)
        s = _dot_nt(q_ref[:, sl], k_ref[:, sl])
        p = jnp.exp2(s - jnp.max(s, axis=1, keepdims=True))
        l = jnp.sum(p, axis=1, keepdims=True)
        o_ref[:, sl] = (_dot(p.astype(_BF), v_ref[:, sl]) / l).astype(_BF)


def _na_meta_attention(q, k, v, batch):
    spec = pl.BlockSpec((N_META, NA_WIDTH), lambda b: (b, 0))
    return pl.pallas_call(
        _na_meta_kernel,
        grid=(batch,),
        in_specs=[spec, spec, spec],
        out_specs=spec,
        out_shape=jax.ShapeDtypeStruct((batch * N_META, NA_WIDTH), _BF),
        compiler_params=pltpu.CompilerParams(dimension_semantics=("parallel",)),
        name="na_meta_attn",
    )(q, k, v)


V_EXT = V_HEAD + 16
MLA_CHUNK = 256
MLA_LOOKAHEAD = 6
MLA_QSUB = 256
MLA_TQ = 512
MLA_TK = 4096


def _mla_kernel(qt_ref, k_ref, vt_ref, km_ref, vmt_ref, ot_ref, m_sc, acc_sc, *, chunk):
    j = pl.program_id(2)
    tq = qt_ref.shape[2]
    nchunks = k_ref.shape[1] // chunk

    @pl.when(j == 0)
    def _():
        def init_head(h, carry):
            s = _dot(km_ref[h], qt_ref[h])
            m = jnp.max(s, axis=0, keepdims=True)
            p = jnp.exp2(s - m).astype(_BF)
            acc_sc[h] = _dot(vmt_ref[h], p)
            m_sc[h] = jnp.broadcast_to(m, (8, tq))
            return carry
        lax.fori_loop(0, MLA_HEADS, init_head, 0)

    qsub = min(MLA_QSUB, tq)
    steps = [(qs, h, c) for qs in range(tq // qsub) for h in range(MLA_HEADS) for c in range(nchunks)]
    lanes = lambda qs: slice(qs * qsub, (qs + 1) * qsub)
    scores = lambda qs, h, c: _dot(k_ref[h, c * chunk:(c + 1) * chunk, :], qt_ref[h, :, lanes(qs)])
    pending = [scores(*st) for st in steps[:MLA_LOOKAHEAD]]
    for n, (qs, h, c) in enumerate(steps):
        if n + MLA_LOOKAHEAD < len(steps):
            pending.append(scores(*steps[n + MLA_LOOKAHEAD]))
        s = pending.pop(0)
        if c == 0:
            m = m_sc[h, :1, lanes(qs)]
            acc = acc_sc[h, :, lanes(qs)]
        m_new = jnp.maximum(m, jnp.max(s, axis=0, keepdims=True))
        p = jnp.exp2(s - m_new).astype(_BF)
        acc = acc * jnp.exp2(m - m_new) + _dot(vt_ref[h, :, c * chunk:(c + 1) * chunk], p)
        m = m_new
        if c == nchunks - 1:
            m_sc[h, :, lanes(qs)] = jnp.broadcast_to(m, (8, qsub))
            acc_sc[h, :, lanes(qs)] = acc

    @pl.when(j == pl.num_programs(2) - 1)
    def _():
        for h in range(MLA_HEADS):
            acc = acc_sc[h]
            ot_ref[h] = (acc[:V_HEAD] / acc[V_HEAD:V_HEAD + 1]).astype(_BF)


def _mla_attention(qt, k, vt, km, vmt, batch, seq, q_rows, q_blk0, q_stride, tq, tk):
    assert q_rows % tq == 0 and seq % tk == 0 and tk % MLA_CHUNK == 0
    nq = q_rows // tq
    nk = seq // tk
    return pl.pallas_call(
        functools.partial(_mla_kernel, chunk=min(MLA_CHUNK, tk)),
        grid=(batch, nq, nk),
        in_specs=[pl.BlockSpec((MLA_HEADS, LANES, tq), lambda b, i, j: (0, 0, q_blk0 + b * q_stride + i)),
                  pl.BlockSpec((MLA_HEADS, tk, LANES), lambda b, i, j: (0, b * nk + j, 0)),
                  pl.BlockSpec((MLA_HEADS, V_EXT, tk), lambda b, i, j: (0, 0, b * nk + j)),
                  pl.BlockSpec((None, MLA_HEADS, N_META, LANES), lambda b, i, j: (b, 0, 0, 0)),
                  pl.BlockSpec((None, MLA_HEADS, V_EXT, N_META), lambda b, i, j: (b, 0, 0, 0))],
        out_specs=pl.BlockSpec((MLA_HEADS, V_HEAD, tq), lambda b, i, j: (0, 0, b * nq + i)),
        out_shape=jax.ShapeDtypeStruct((MLA_HEADS, V_HEAD, batch * q_rows), _BF),
        scratch_shapes=[pltpu.VMEM((MLA_HEADS, 8, tq), _F32), pltpu.VMEM((MLA_HEADS, V_EXT, tq), _F32)],
        compiler_params=pltpu.CompilerParams(dimension_semantics=("parallel", "parallel", "arbitrary"),
                                             vmem_limit_bytes=VMEM_LIMIT),
        name="mla_attn",
    )(qt, k, vt, km, vmt)


def _outproj_kernel(x_ref, xm_ref, ong_ref, onm_ref, omg_ref, omm_ref, wo_na_ref, wo_mla_ref, g_ref, wr_hi_ref,
                    wr_lo_ref, h_ref, hn_ref, aff_ref, *, n_grid_tiles, n_valid):
    i = pl.program_id(0)
    is_meta = i == n_grid_tiles
    tn = lambda a_t, b: lax.dot_general(a_t, b, (((0,), (0,)), ((), ())), preferred_element_type=_F32)
    wr_hi = wr_hi_ref[...]
    subs = [slice(r0, r0 + OUT_SUB) for r0 in range(0, ROW_TILE, OUT_SUB)]
    hs = []
    for rows in subs:
        o_na_t = jnp.where(is_meta, onm_ref[:, rows], ong_ref[:, rows])
        o_mla_t = jnp.where(is_meta, omm_ref[:, rows], omg_ref[:, rows])
        x = jnp.where(is_meta, xm_ref[rows, :], x_ref[rows, :])
        hs.append(x + tn(o_na_t, wo_na_ref[...]) + tn(o_mla_t, wo_mla_ref[...]))
    split = []
    for rows, h in zip(subs, hs):
        h_ref[rows, :] = h
        hn = _rms(h, g_ref[...])
        hn_hi = hn.astype(_BF)
        hn_ref[rows, :] = hn_hi
        split.append((hn_hi, (hn - hn_hi.astype(_F32)).astype(_BF)))
    for rows, (hn_hi, hn_lo) in zip(subs, split):
        r0 = rows.start
        logits = _dot_nt(wr_hi, hn_hi) + _dot_nt(wr_hi, hn_lo) + _dot_nt(wr_lo_ref[...], hn_hi)
        e = jnp.exp(logits - jnp.max(logits, axis=0, keepdims=True))
        aff = e / jnp.sum(e, axis=0, keepdims=True)
        row = i * ROW_TILE + r0 + lax.broadcasted_iota(jnp.int32, aff.shape, 1)
        aff_ref[:, rows] = jnp.where(row < n_valid, aff, -1.0)


def _outproj(x, x_meta, o_na_g, o_na_m, o_mla_g, o_mla_m, wo_na, wo_mla, g_ffn, wr_hi, wr_lo, n_valid):
    n_grid_tiles = x.shape[0] // ROW_TILE
    n_tiles = n_grid_tiles + 1
    np_rows = n_tiles * ROW_TILE
    row = lambda i: (i, 0)
    grid_col = lambda i: (0, jnp.minimum(i, n_grid_tiles - 1))
    full = lambda i: (0, 0)
    w = lambda a: pl.BlockSpec(a.shape, full)
    return pl.pallas_call(
        functools.partial(_outproj_kernel, n_grid_tiles=n_grid_tiles, n_valid=n_valid),
        grid=(n_tiles,),
        in_specs=[pl.BlockSpec((ROW_TILE, D_MODEL), lambda i: (jnp.minimum(i, n_grid_tiles - 1), 0)), w(x_meta),
                  pl.BlockSpec((NA_WIDTH, ROW_TILE), grid_col), w(o_na_m),
                  pl.BlockSpec((MLA_WIDTH, ROW_TILE), grid_col), w(o_mla_m),
                  w(wo_na), w(wo_mla), w(g_ffn), w(wr_hi), w(wr_lo)],
        out_specs=[pl.BlockSpec((ROW_TILE, D_MODEL), row), pl.BlockSpec((ROW_TILE, D_MODEL), row),
                   pl.BlockSpec((None, N_EXPERTS, ROW_TILE), lambda i: (i, 0, 0))],
        out_shape=[jax.ShapeDtypeStruct((np_rows, D_MODEL), _F32), jax.ShapeDtypeStruct((np_rows, D_MODEL), _BF),
                   jax.ShapeDtypeStruct((n_tiles, N_EXPERTS, ROW_TILE), _F32)],
        compiler_params=pltpu.CompilerParams(dimension_semantics=("parallel",), vmem_limit_bytes=VMEM_LIMIT),
        name="outproj_router",
    )(x, x_meta, o_na_g, o_na_m, o_mla_g, o_mla_m, wo_na, wo_mla, g_ffn, wr_hi, wr_lo)


XE_ALIGN = 16
SLOT_W = 64
WIN = 64


def _route_kernel(aff_ref, slot_ref, lo_ref, *, cap):
    n_tiles = aff_ref.shape[0]
    bits = lambda: pltpu.bitcast(aff_ref[...], jnp.int32)

    def bit_step(i, cur):
        cand = cur | jnp.left_shift(jnp.int32(1), 30 - i)
        cnt = jnp.sum((bits() >= cand).astype(jnp.int32), axis=(0, 2), keepdims=True)
        return jnp.where(cnt >= cap, cand, cur)
    thr = lax.fori_loop(0, 31, bit_step, jnp.zeros((1, N_EXPERTS, 1), jnp.int32))
    n_gt = jnp.sum((bits() > thr).astype(jnp.int32), axis=(0, 2), keepdims=True)
    need = (cap - n_gt)[0].astype(_F32)
    thr = thr[0]
    r = lax.broadcasted_iota(jnp.int32, (ROW_TILE, ROW_TILE), 0)
    c = lax.broadcasted_iota(jnp.int32, (ROW_TILE, ROW_TILE), 1)
    tri = jnp.where(r <= c, 1.0, 0.0).astype(_BF)
    as_bf = lambda m: jnp.where(m, 1.0, 0.0).astype(_BF)

    def tile_step(t, carry):
        tie_before, kept_before = carry
        b = pltpu.bitcast(aff_ref[t], jnp.int32)
        eq = b == thr
        tie_rank = _dot(as_bf(eq), tri) + tie_before
        keep = (b > thr) | (eq & (tie_rank <= need))
        kept_incl = _dot(as_bf(keep), tri) + kept_before
        slot_ref[t] = jnp.where(keep, kept_incl - 1.0, -1.0).astype(jnp.int32)
        lo_ref[t] = jnp.broadcast_to(kept_before, (N_EXPERTS, LANES)).astype(jnp.int32)
        return tie_rank[:, ROW_TILE - 1:], kept_incl[:, ROW_TILE - 1:]
    zero = jnp.zeros((N_EXPERTS, 1), _F32)
    lax.fori_loop(0, n_tiles, tile_step, (zero, zero))


def _route(aff, cap):
    n_tiles = aff.shape[0]
    full = lambda shape: pl.BlockSpec(shape, lambda i: (0, 0, 0))
    return pl.pallas_call(
        functools.partial(_route_kernel, cap=cap),
        grid=(1,),
        in_specs=[full(aff.shape)],
        out_specs=[full(aff.shape), full((n_tiles, N_EXPERTS, LANES))],
        out_shape=[jax.ShapeDtypeStruct(aff.shape, jnp.int32),
                   jax.ShapeDtypeStruct((n_tiles, N_EXPERTS, LANES), jnp.int32)],
        compiler_params=pltpu.CompilerParams(dimension_semantics=("arbitrary",), vmem_limit_bytes=VMEM_LIMIT),
        name="route",
    )(aff)


def _tile_max(lo_ref, i, start_of):
    m = lo_ref[(i + 1) * N_EXPERTS] - start_of(0)
    for e in range(1, N_EXPERTS):
        m = jnp.maximum(m, lo_ref[(i + 1) * N_EXPERTS + e] - start_of(e))
    return m


def _dispatch_kernel(lo_ref, hn_ref, slot_ref, xe_ref, stage, stage_extra, carry, sem, sem_extra, *, zero_from,
                     zero_rows):
    i = pl.program_id(0)
    s = i % 2
    base = lambda e: (lo_ref[i * N_EXPERTS + e] // XE_ALIGN) * XE_ALIGN
    next_base = lambda e: (lo_ref[(i + 1) * N_EXPERTS + e] // XE_ALIGN) * XE_ALIGN

    def compact(k, dst):
        w = lax.broadcasted_iota(jnp.int32, (SLOT_W, ROW_TILE), 0) + k * SLOT_W
        onehot = jnp.concatenate(
            [jnp.where(slot_ref[e:e + 1, :] - base(e) == w, 1.0, 0.0).astype(_BF) for e in range(N_EXPERTS)], axis=0)
        dst[...] = _dot(onehot, hn_ref[...]).astype(_BF)

    def copy(e, src, k, sm):
        dst_row = pl.multiple_of(base(e) + k * SLOT_W, XE_ALIGN)
        return pltpu.make_async_copy(src.at[pl.ds(e * SLOT_W, SLOT_W)], xe_ref.at[e, pl.ds(dst_row, SLOT_W)], sm)

    def save_carry(k, src):
        for e in range(N_EXPERTS):
            g = next_base(e) - base(e) - k * SLOT_W

            @pl.when((g >= 0) & (g < SLOT_W))
            def _():
                carry[e] = src[pl.ds(pl.multiple_of(e * SLOT_W + g, XE_ALIGN), XE_ALIGN), :]

    @pl.when(i == 0)
    def _():
        carry[...] = jnp.zeros(carry.shape, _BF)
        stage_extra[...] = jnp.zeros(stage_extra.shape, _BF)
        zero_copies = [pltpu.make_async_copy(stage_extra.at[pl.ds(0, zero_rows)],
                                             xe_ref.at[e, pl.ds(zero_from, zero_rows)], sem_extra)
                       for e in range(N_EXPERTS)]
        for cp in zero_copies:
            cp.start()
        for cp in zero_copies:
            cp.wait()

    compact(0, stage.at[s])
    for e in range(N_EXPERTS):
        stage[s, e * SLOT_W:e * SLOT_W + XE_ALIGN, :] += carry[e]
    save_carry(0, stage.at[s])

    @pl.when(i > 0)
    def _():
        for e in range(N_EXPERTS):
            copy(e, stage.at[1 - s], 0, sem.at[1 - s]).wait()

    for e in range(N_EXPERTS):
        copy(e, stage.at[s], 0, sem.at[s]).start()

    def extra_pass(k, c):
        compact(k, stage_extra)
        save_carry(k, stage_extra)
        for wait in (False, True):
            for e in range(N_EXPERTS):
                @pl.when(lo_ref[(i + 1) * N_EXPERTS + e] - base(e) >= k * SLOT_W)
                def _():
                    cp = copy(e, stage_extra, k, sem_extra)
                    cp.wait() if wait else cp.start()
        return c
    lax.fori_loop(1, _tile_max(lo_ref, i, base) // SLOT_W + 1, extra_pass, 0)

    @pl.when(i == pl.num_programs(0) - 1)
    def _():
        for e in range(N_EXPERTS):
            copy(e, stage.at[s], 0, sem.at[s]).wait()


def _dispatch(lo, hn, slot, cap, cap_pad):
    n_tiles = slot.shape[0]
    zero_from = cap // XE_ALIGN * XE_ALIGN
    zero_rows = cap_pad + SLOT_W - zero_from
    assert zero_rows <= N_EXPERTS * SLOT_W and SLOT_W % XE_ALIGN == 0 and cap_pad % XE_ALIGN == 0
    return pl.pallas_call(
        functools.partial(_dispatch_kernel, zero_from=zero_from, zero_rows=zero_rows),
        grid_spec=pltpu.PrefetchScalarGridSpec(
            num_scalar_prefetch=1,
            grid=(n_tiles,),
            in_specs=[pl.BlockSpec((ROW_TILE, D_MODEL), lambda i, lo: (i, 0)),
                      pl.BlockSpec((None, N_EXPERTS, ROW_TILE), lambda i, lo: (i, 0, 0))],
            out_specs=pl.BlockSpec(memory_space=pl.ANY),
            scratch_shapes=[pltpu.VMEM((2, N_EXPERTS * SLOT_W, D_MODEL), _BF),
                            pltpu.VMEM((N_EXPERTS * SLOT_W, D_MODEL), _BF),
                            pltpu.VMEM((N_EXPERTS, XE_ALIGN, D_MODEL), _BF),
                            pltpu.SemaphoreType.DMA((2,)), pltpu.SemaphoreType.DMA(())]),
        out_shape=jax.ShapeDtypeStruct((N_EXPERTS, cap_pad + SLOT_W, D_MODEL), _BF),
        compiler_params=pltpu.CompilerParams(dimension_semantics=("arbitrary",), vmem_limit_bytes=VMEM_LIMIT),
        name="dispatch",
    )(lo, hn, slot)


def _expert_kernel(x_ref, wg_ref, wu_ref, wd_ref, y_ref):
    x = x_ref[...]
    g = _dot(x, wg_ref[...])
    u = _dot(x, wu_ref[...])
    hid = (g * jax.nn.sigmoid(g) * u).astype(_BF)
    y_ref[...] = _dot(hid, wd_ref[...].astype(_BF)).astype(_BF)


def _experts(xe, wg, wu, wd, cap_pad, tile):
    n_e = xe.shape[0]
    wspec = lambda a: pl.BlockSpec((None,) + a.shape[1:], lambda e, t: (e, 0, 0))
    return pl.pallas_call(
        _expert_kernel,
        grid=(n_e, cap_pad // tile),
        in_specs=[pl.BlockSpec((None, tile, D_MODEL), lambda e, t: (e, t, 0)), wspec(wg), wspec(wu), wspec(wd)],
        out_specs=pl.BlockSpec((None, tile, D_MODEL), lambda e, t: (e, t, 0)),
        out_shape=jax.ShapeDtypeStruct((n_e, cap_pad, D_MODEL), _BF),
        compiler_params=pltpu.CompilerParams(dimension_semantics=("parallel", "arbitrary"),
                                             vmem_limit_bytes=VMEM_LIMIT),
        name="experts",
    )(xe, wg, wu, wd)


def _combine_kernel(lo_ref, h_ref, slot_ref, aff_ref, g_ref, ye_ref, o_ref, win, win_extra, f_acc, sem, sem_extra, *,
                    cap_pad):
    i = pl.program_id(0)
    s = i % 2
    first = lambda t, e: (lo_ref[t * N_EXPERTS + e] // 16) * 16

    def start_row(t, e, k):
        return pl.multiple_of(jnp.minimum(first(t, e) + k * WIN, cap_pad - WIN), 16)

    def window_copies(t, k, buf, sm):
        return [pltpu.make_async_copy(ye_ref.at[e, pl.ds(start_row(t, e, k), WIN)], buf.at[pl.ds(e * WIN, WIN)], sm)
                for e in range(N_EXPERTS)]

    def weights(k):
        w = lax.broadcasted_iota(jnp.int32, (WIN, ROW_TILE), 0)
        parts = []
        for e in range(N_EXPERTS):
            slot = slot_ref[e:e + 1, :]
            hit = (slot == w + start_row(i, e, k)) & (slot >= first(i, e) + k * WIN)
            parts.append(jnp.where(hit, aff_ref[e:e + 1, :], 0.0).astype(_BF))
        return jnp.concatenate(parts, axis=0)

    scatter = lambda wts, rows: lax.dot_general(wts, rows, (((0,), (0,)), ((), ())), preferred_element_type=_F32)

    @pl.when(i == 0)
    def _():
        for cp in window_copies(0, 0, win.at[0], sem.at[0]):
            cp.start()

    @pl.when(i + 1 < pl.num_programs(0))
    def _():
        for cp in window_copies(i + 1, 0, win.at[1 - s], sem.at[1 - s]):
            cp.start()

    wts = weights(0)
    for cp in window_copies(i, 0, win.at[s], sem.at[s]):
        cp.wait()
    f_acc[...] = scatter(wts, win[s])

    def extra_pass(k, c):
        copies = window_copies(i, k, win_extra, sem_extra)
        for cp in copies:
            cp.start()
        wts_k = weights(k)
        for cp in copies:
            cp.wait()
        f_acc[...] += scatter(wts_k, win_extra[...])
        return c
    reach = _tile_max(lo_ref, i, lambda e: first(i, e))
    lax.fori_loop(1, (reach + WIN - 1) // WIN, extra_pass, 0)
    o_ref[...] = _rms(h_ref[...] + f_acc[...], g_ref[...])


def _combine(lo, h, slot, aff, g_final, ye, n_rows):
    cap_pad = ye.shape[1]
    assert cap_pad % 16 == 0 and WIN % 16 == 0 and cap_pad >= WIN
    tile3 = pl.BlockSpec((None, N_EXPERTS, ROW_TILE), lambda i, lo: (i, 0, 0))
    return pl.pallas_call(
        functools.partial(_combine_kernel, cap_pad=cap_pad),
        grid_spec=pltpu.PrefetchScalarGridSpec(
            num_scalar_prefetch=1,
            grid=(n_rows // ROW_TILE,),
            in_specs=[pl.BlockSpec((ROW_TILE, D_MODEL), lambda i, lo: (i, 0)), tile3, tile3,
                      pl.BlockSpec(g_final.shape, lambda i, lo: (0, 0)), pl.BlockSpec(memory_space=pl.ANY)],
            out_specs=pl.BlockSpec((ROW_TILE, D_MODEL), lambda i, lo: (i, 0)),
            scratch_shapes=[pltpu.VMEM((2, N_EXPERTS * WIN, D_MODEL), _BF), pltpu.VMEM((N_EXPERTS * WIN, D_MODEL), _BF),
                            pltpu.VMEM((ROW_TILE, D_MODEL), _F32),
                            pltpu.SemaphoreType.DMA((2,)), pltpu.SemaphoreType.DMA(())]),
        out_shape=jax.ShapeDtypeStruct((n_rows, D_MODEL), _F32),
        compiler_params=pltpu.CompilerParams(dimension_semantics=("arbitrary",), vmem_limit_bytes=VMEM_LIMIT),
        name="combine_norm",
    )(lo, h, slot, aff, g_final, ye)


def _rotate_half_cols(w):
    half = QK_ROPE // 2
    return jnp.concatenate([-w[..., half:], w[..., :half]], axis=-1)


def _prepare_weights(w_in, g_attn, g_q, w_uq, g_kv, w_ukv, w_o, g_ffn, w_router, w_gate, w_up, w_down, na_rpb):
    na_scale = 1.0 / math.sqrt(NA_HEAD_DIM)
    c0, c2 = 3 * NA_WIDTH, 3 * NA_WIDTH + Q_LORA + KV_LORA
    w_kr = w_in[:, c2:]
    zpad = lambda n: jnp.zeros((D_MODEL, n), _F32)
    slot = lambda w: jnp.concatenate([zpad(QK_NOPE), w, zpad(LANES - QK_NOPE - QK_ROPE)], axis=1)
    w1 = jnp.concatenate([w_in[:, NA_WIDTH:2 * NA_WIDTH], w_in[:, c0:c2], slot(w_kr),
                          slot(_rotate_half_cols(w_kr))], axis=1).astype(_BF)
    q_scale = LOG2E / math.sqrt(QK_NOPE + QK_ROPE)
    wq = (w_uq * q_scale).reshape(Q_LORA, MLA_HEADS, QK_NOPE + QK_ROPE)
    pad = LANES - QK_NOPE - QK_ROPE
    plain = jnp.pad(wq, ((0, 0), (0, 0), (0, pad)))
    rot = jnp.pad(_rotate_half_cols(wq[..., QK_NOPE:]), ((0, 0), (0, 0), (QK_NOPE, pad)))
    w2t = jnp.concatenate([plain.reshape(Q_LORA, -1), rot.reshape(Q_LORA, -1)], axis=1).T.astype(_BF)
    wkv = w_ukv.reshape(KV_LORA, MLA_HEADS, QK_NOPE + V_HEAD)
    w3k = jnp.pad(wkv[..., :QK_NOPE], ((0, 0), (0, 0), (0, LANES - QK_NOPE))).reshape(KV_LORA, -1).astype(_BF)
    w3vt = wkv[..., QK_NOPE:].reshape(KV_LORA, -1).T.astype(_BF)
    wr = w_router.T
    wr_hi = wr.astype(_BF)
    wr_lo = (wr - wr_hi.astype(_F32)).astype(_BF)
    return dict(
        w1=w1, w2t=w2t, w3k=w3k, w3vt=w3vt,
        g_attn=g_attn[None], g_q=g_q[None], g_kv=g_kv[None], g_ffn=g_ffn[None],
        wo_na=w_o[:NA_WIDTH].astype(_BF), wo_mla=w_o[NA_WIDTH:].astype(_BF),
        wr_hi=wr_hi, wr_lo=wr_lo,
        wg=w_gate.astype(_BF), wu=w_up.astype(_BF), wd=w_down,
        wqvt=jnp.concatenate([w_in[:, :NA_WIDTH] * (na_scale * LOG2E), w_in[:, 2 * NA_WIDTH:3 * NA_WIDTH]],
                             axis=1).T.astype(_BF),
        na_bias_t=_na_bias_table(na_rpb),
    )


def _na_bias_table(rpb):
    ql = np.arange(NA_BLK)
    kl = np.arange(3 * NA_BLK)
    ri, qc = ql // GRID_W, ql % GRID_W
    kj, kc = kl // GRID_W, kl % GRID_W
    qcs = np.clip(qc - WIN_C // 2, 0, GRID_W - WIN_C)
    col_ok = (kc[None, :] >= qcs[:, None]) & (kc[None, :] < qcs[:, None] + WIN_C)
    lo = np.stack([np.full(NA_BLK, NA_ROWS), ri, np.zeros(NA_BLK, np.int64)])
    row_ok = (kj[None, None, :] >= lo[:, :, None]) & (kj[None, None, :] < lo[:, :, None] + WIN_R)
    ok = row_ok & col_ok[None]
    rows_q, rows_k, cols = np.arange(NA_ROWS), np.arange(3 * NA_ROWS), np.arange(GRID_W)
    dr = np.clip(rows_k[None, :] - rows_q[:, None] + (WIN_R - 1) - NA_ROWS, 0, 2 * WIN_R - 2)
    dc = np.clip(cols[None, :] - cols[:, None] + (WIN_C - 1), 0, 2 * WIN_C - 2)
    pick_r = (dr[..., None] == np.arange(2 * WIN_R - 1)).astype(np.float32)
    pick_c = (dc[..., None] == np.arange(2 * WIN_C - 1)).astype(np.float32)
    vals = jnp.einsum("hrc,ijr,qkc->hiqjk", rpb.astype(_F32), pick_r, pick_c, precision=lax.Precision.HIGHEST)
    vals = vals.reshape(NA_HEADS, NA_BLK, 3 * NA_BLK)
    return jnp.where(jnp.asarray(ok)[:, None], vals[None] * LOG2E, NEG).transpose(0, 1, 3, 2)


def _rope_tables(seq):
    half = QK_ROPE // 2
    inv_freq = ROPE_THETA ** (-jnp.arange(half, dtype=_F32) * 2.0 / QK_ROPE)
    pos = jnp.concatenate([jnp.arange(seq, dtype=_F32) + N_META,
                           jnp.tile(jnp.arange(N_META, dtype=_F32), ROW_TILE // N_META)])
    n = seq + ROW_TILE
    pad = LANES - QK_NOPE - QK_ROPE

    def lay(a, nope_fill, axis):
        shape = lambda k: (n, k) if axis == 1 else (k, n)
        return jnp.concatenate([jnp.full(shape(QK_NOPE), nope_fill, _F32), a, a, jnp.zeros(shape(pad), _F32)], axis=axis)
    ang = pos[:, None] * inv_freq[None, :]
    dense = ang.reshape(-1, LANES)
    cos, sin = jnp.cos(dense).reshape(n, half), jnp.sin(dense).reshape(n, half)
    return lay(cos, 1.0, 1), lay(sin, 0.0, 1), lay(cos.T, 1.0, 0), lay(sin.T, 0.0, 0)


def _expert_tile(cap):
    if cap <= 1024:
        return -(-cap // 16) * 16
    return min(range(512, 1025, 16), key=lambda t: (-(-cap // t) * t - cap, -t))


def _run_group(x, meta_tokens, prm, rope, g_final, tq, tk):
    batch, seq, _ = x.shape
    n_grid = batch * seq
    n_valid = n_grid + batch * N_META
    assert seq % NA_BLK == 0 and seq // GRID_W >= WIN_R and batch * N_META <= ROW_TILE
    meta = jnp.broadcast_to(meta_tokens[None], (batch, N_META, D_MODEL)).reshape(batch * N_META, D_MODEL)
    xg = x.reshape(n_grid, D_MODEL)
    xm = jnp.pad(meta, ((0, ROW_TILE - batch * N_META), (0, 0)))
    assert rope[0].shape[0] >= seq + ROW_TILE
    kna, qt, k3, vt, qnat, vnat = _proj(xg, xm, seq, *rope, prm["g_attn"], prm["g_q"], prm["g_kv"],
                                        prm["w1"], prm["w2t"], prm["w3k"], prm["w3vt"], prm["wqvt"])
    n_meta = batch * N_META
    meta_cols = slice(n_grid, n_grid + n_meta)
    vmt_na = vnat[:, :, meta_cols].reshape(NA_HEADS, V_EXT, batch, N_META).transpose(2, 0, 1, 3)
    o_na_g = _na_attention(qnat, kna, vnat, vmt_na, prm["na_bias_t"], batch, seq).reshape(NA_WIDTH, n_grid)
    q_own = jnp.stack([qnat[h, (h % 2) * NA_HEAD_DIM:(h % 2 + 1) * NA_HEAD_DIM, meta_cols] for h in range(NA_HEADS)])
    rows_of = lambda t: t.transpose(2, 0, 1).reshape(n_meta, NA_WIDTH)
    o_na_m = _na_meta_attention(rows_of(q_own), kna[meta_cols], rows_of(vnat[:, :NA_HEAD_DIM, meta_cols]), batch)
    km = k3[:, n_grid:n_grid + n_meta].reshape(MLA_HEADS, batch, N_META, LANES).transpose(1, 0, 2, 3)
    vmt = vt[:, :, n_grid:n_grid + n_meta].reshape(MLA_HEADS, V_EXT, batch, N_META).transpose(2, 0, 1, 3)
    tk = min(tk, seq)
    ot_g = _mla_attention(qt, k3, vt, km, vmt, batch, seq, seq, 0, seq // tq, tq, tk)
    ot_m = _mla_attention(qt, k3, vt, km, vmt, batch, seq, ROW_TILE, n_grid // ROW_TILE, 0, ROW_TILE, tk)
    ot_m = ot_m.reshape(MLA_WIDTH, batch, ROW_TILE)
    ot_m = jnp.concatenate([ot_m[:, b, b * N_META:(b + 1) * N_META] for b in range(batch)], axis=1)
    ot_m = jnp.pad(ot_m, ((0, 0), (0, ROW_TILE - n_meta)))
    o_na_m = jnp.pad(o_na_m.T, ((0, 0), (0, ROW_TILE - n_meta)))
    h, hn, aff = _outproj(xg, xm, o_na_g, o_na_m, ot_g.reshape(MLA_WIDTH, n_grid), ot_m, prm["wo_na"], prm["wo_mla"],
                          prm["g_ffn"], prm["wr_hi"], prm["wr_lo"], n_valid)
    cap = EC_CAPACITY * n_valid // N_EXPERTS
    tile = _expert_tile(cap)
    cap_pad = -(-cap // tile) * tile
    slot, lo3 = _route(aff, cap)
    lo = jnp.concatenate([lo3[:, :, 0], jnp.full((1, N_EXPERTS), cap, jnp.int32)]).reshape(-1)
    xe = _dispatch(lo, hn, slot, cap, cap_pad)
    ye = _experts(xe, prm["wg"], prm["wu"], prm["wd"], cap_pad, tile)
    return _combine(lo, h, slot, aff, g_final[None], ye, n_grid).reshape(batch, seq, D_MODEL)


def kernel(x_prompt, x_sample, meta_tokens, g_attn, w_in, na_rpb, g_q, w_uq, g_kv, w_ukv, w_o, g_ffn, w_router,
           w_gate, w_up, w_down, g_final):
    prm = _prepare_weights(w_in[0], g_attn[0], g_q[0], w_uq[0], g_kv[0], w_ukv[0], w_o[0], g_ffn[0], w_router[0],
                           w_gate[0], w_up[0], w_down[0], na_rpb[0])
    rope = _rope_tables(max(x_prompt.shape[1], x_sample.shape[1]))
    y_prompt = _run_group(x_prompt, meta_tokens, prm, rope, g_final, MLA_TQ, MLA_TK)
    y_sample = _run_group(x_sample, meta_tokens, prm, rope, g_final, MLA_TQ, MLA_TK)
    return (y_prompt, y_sample)
```

```python
import functools
import math

import jax
import jax.numpy as jnp
import numpy as np
from jax import lax
from jax.experimental import pallas as pl
from jax.experimental.pallas import tpu as pltpu

D_MODEL = 1024
GRID_W = 64
N_META = 16
WIN_R = 8
WIN_C = 16
NA_HEADS = 8
NA_HEAD_DIM = 64
MLA_HEADS = 8
QK_NOPE = 64
QK_ROPE = 32
V_HEAD = 64
Q_LORA = 768
KV_LORA = 256
ROPE_THETA = 10000.0
N_EXPERTS = 16
EC_CAPACITY = 2
D_EXPERT = 2048
NORM_EPS = 1e-6
NA_WIDTH = NA_HEADS * NA_HEAD_DIM
MLA_WIDTH = MLA_HEADS * V_HEAD

LANES = 128
ROW_TILE = 256
OUT_SUB = 128
NA_ROWS = 4
NA_BLK = NA_ROWS * GRID_W
NA_LOOKAHEAD = 5
NEG = -1e30
LOG2E = 1.4426950408889634
VMEM_LIMIT = 56 * 1024 * 1024

_BF = jnp.bfloat16
_F32 = jnp.float32


def _dot(a, b):
    return jnp.dot(a, b, preferred_element_type=_F32)


def _dot_nt(a, b):
    return lax.dot_general(a, b, (((1,), (1,)), ((), ())), preferred_element_type=_F32)


def _rms(x, g):
    return x * lax.rsqrt(jnp.mean(x * x, axis=-1, keepdims=True) + NORM_EPS) * g


def _proj_kernel(x_ref, xm_ref, cos_ref, sin_ref, cost_ref, sint_ref, g_attn_ref, g_q_ref, g_kv_ref, w1_ref, w2t_ref,
                 w3k_ref, w3vt_ref, wqvt_ref, kna_ref, qt_ref, k_ref, vt_ref, qnat_ref, vnat_ref):
    x = jnp.where(pl.program_id(0) == pl.num_programs(0) - 1, xm_ref[...], x_ref[...])
    a = _rms(x, g_attn_ref[...]).astype(_BF)
    p = _dot(a, w1_ref[...])
    kna_ref[...] = p[:, :NA_WIDTH].astype(_BF)
    o = NA_WIDTH
    cqn = _rms(p[:, o:o + Q_LORA], g_q_ref[...]).astype(_BF)
    o += Q_LORA
    ckvn = _rms(p[:, o:o + KV_LORA], g_kv_ref[...]).astype(_BF)
    o += KV_LORA
    kr = p[:, o:o + LANES] * cos_ref[...] + p[:, o + LANES:o + 2 * LANES] * sin_ref[...]
    k3 = _dot(ckvn, w3k_ref[...])
    q2t = _dot_nt(w2t_ref[...], cqn)
    vt = _dot_nt(w3vt_ref[...], ckvn)
    cos_t = cost_ref[...]
    sin_t = sint_ref[...]
    hw = MLA_HEADS * LANES
    ones_row = jnp.where(lax.broadcasted_iota(jnp.int32, (V_EXT - V_HEAD, x_ref.shape[0]), 0) == 0, 1.0, 0.0)
    for h in range(MLA_HEADS):
        sl = slice(h * LANES, (h + 1) * LANES)
        qt_ref[h] = (q2t[sl] * cos_t + q2t[hw + h * LANES:hw + (h + 1) * LANES] * sin_t).astype(_BF)
        k_ref[h] = (k3[:, sl] + kr).astype(_BF)
        vt_ref[h, :V_HEAD, :] = vt[h * V_HEAD:(h + 1) * V_HEAD].astype(_BF)
        vt_ref[h, V_HEAD:, :] = ones_row.astype(_BF)
    qvt = _dot_nt(wqvt_ref[...], a)
    zero_half = jnp.zeros((NA_HEAD_DIM, x_ref.shape[0]), _BF)
    for h in range(NA_HEADS):
        own = slice((h % 2) * NA_HEAD_DIM, (h % 2 + 1) * NA_HEAD_DIM)
        other = slice((1 - h % 2) * NA_HEAD_DIM, (2 - h % 2) * NA_HEAD_DIM)
        qnat_ref[h, own, :] = qvt[h * NA_HEAD_DIM:(h + 1) * NA_HEAD_DIM].astype(_BF)
        qnat_ref[h, other, :] = zero_half
        vnat_ref[h, :NA_HEAD_DIM, :] = qvt[NA_WIDTH + h * NA_HEAD_DIM:NA_WIDTH + (h + 1) * NA_HEAD_DIM].astype(_BF)
        vnat_ref[h, NA_HEAD_DIM:, :] = ones_row.astype(_BF)


def _proj(x, x_meta, seq, cos, sin, cos_t, sin_t, g_attn, g_q, g_kv, w1, w2t, w3k, w3vt, wqvt):
    n_grid_tiles = x.shape[0] // ROW_TILE
    np_rows = x.shape[0] + ROW_TILE
    tiles_per_seq = seq // ROW_TILE
    table_meta_tile = cos.shape[0] // ROW_TILE - 1
    table_tile = lambda i: jnp.where(i < n_grid_tiles, i % tiles_per_seq, table_meta_tile)
    row = lambda i: (table_tile(i), 0)
    col = lambda i: (0, table_tile(i))
    full = lambda i: (0, 0)
    w = lambda a: pl.BlockSpec(a.shape, full)
    return pl.pallas_call(
        _proj_kernel,
        grid=(np_rows // ROW_TILE,),
        in_specs=[pl.BlockSpec((ROW_TILE, D_MODEL), lambda i: (jnp.minimum(i, n_grid_tiles - 1), 0)), w(x_meta),
                  pl.BlockSpec((ROW_TILE, LANES), row),
                  pl.BlockSpec((ROW_TILE, LANES), row), pl.BlockSpec((LANES, ROW_TILE), col),
                  pl.BlockSpec((LANES, ROW_TILE), col), w(g_attn), w(g_q), w(g_kv), w(w1), w(w2t), w(w3k), w(w3vt),
                  w(wqvt)],
        out_specs=[pl.BlockSpec((ROW_TILE, NA_WIDTH), lambda i: (i, 0)),
                   pl.BlockSpec((MLA_HEADS, LANES, ROW_TILE), lambda i: (0, 0, i)),
                   pl.BlockSpec((MLA_HEADS, ROW_TILE, LANES), lambda i: (0, i, 0)),
                   pl.BlockSpec((MLA_HEADS, V_EXT, ROW_TILE), lambda i: (0, 0, i)),
                   pl.BlockSpec((NA_HEADS, LANES, ROW_TILE), lambda i: (0, 0, i)),
                   pl.BlockSpec((NA_HEADS, V_EXT, ROW_TILE), lambda i: (0, 0, i))],
        out_shape=[jax.ShapeDtypeStruct((np_rows, NA_WIDTH), _BF),
                   jax.ShapeDtypeStruct((MLA_HEADS, LANES, np_rows), _BF),
                   jax.ShapeDtypeStruct((MLA_HEADS, np_rows, LANES), _BF),
                   jax.ShapeDtypeStruct((MLA_HEADS, V_EXT, np_rows), _BF),
                   jax.ShapeDtypeStruct((NA_HEADS, LANES, np_rows), _BF),
                   jax.ShapeDtypeStruct((NA_HEADS, V_EXT, np_rows), _BF)],
        compiler_params=pltpu.CompilerParams(dimension_semantics=("parallel",), vmem_limit_bytes=VMEM_LIMIT),
        name="proj",
    )(x, x_meta, cos, sin, cos_t, sin_t, g_attn, g_q, g_kv, w1, w2t, w3k, w3vt, wqvt)


def _na_kernel(qt_ref, kp_ref, kc_ref, kn_ref, vtp_ref, vtc_ref, vtn_ref, km_ref, vmt_ref, bias_ref, ot_ref):
    ks = (kp_ref, kc_ref, kn_ref)
    vts = (vtp_ref, vtc_ref, vtn_ref)
    steps = [(h, c) for h in range(NA_HEADS) for c in range(3)]
    lanes = lambda h: slice((h // 2) * LANES, (h // 2 + 1) * LANES)
    scores = lambda h, c: _dot(ks[c][:, lanes(h)], qt_ref[h])
    meta_s = [_dot(km_ref[:, lanes(h)], qt_ref[h]) for h in range(NA_HEADS)]
    meta_m = [jnp.max(sm, axis=0, keepdims=True) for sm in meta_s]
    meta_acc = [_dot(vmt_ref[h], jnp.exp2(meta_s[h] - meta_m[h]).astype(_BF)) for h in range(NA_HEADS)]
    pending = [scores(*st) for st in steps[:NA_LOOKAHEAD]]
    for n, (h, c) in enumerate(steps):
        if n + NA_LOOKAHEAD < len(steps):
            pending.append(scores(*steps[n + NA_LOOKAHEAD]))
        s = bias_ref[h, c * NA_BLK:(c + 1) * NA_BLK, :] + pending.pop(0)
        if c == 0:
            m, acc = meta_m[h], meta_acc[h]
        m_new = jnp.maximum(m, jnp.max(s, axis=0, keepdims=True))
        p = jnp.exp2(s - m_new).astype(_BF)
        acc = acc * jnp.exp2(m - m_new) + _dot(vts[c][h], p)
        m = m_new
        if c == 2:
            ot_ref[h] = (acc[:NA_HEAD_DIM] / acc[NA_HEAD_DIM:NA_HEAD_DIM + 1]).astype(_BF)


def _na_attention(qt, k, vt, vmt, bias_t, batch, seq):
    nblk = seq // NA_BLK
    meta_blk0 = batch * seq // N_META
    near = lambda b, j, off: b * nblk + jnp.clip(j + off, 0, nblk - 1)
    k_spec = lambda off: pl.BlockSpec((NA_BLK, NA_WIDTH), lambda b, j: (near(b, j, off), 0))
    vt_spec = lambda off: pl.BlockSpec((NA_HEADS, V_EXT, NA_BLK), lambda b, j: (0, 0, near(b, j, off)))
    cls = lambda b, j: (jnp.where(j == 0, 0, jnp.where(j == nblk - 1, 2, 1)), 0, 0, 0)
    return pl.pallas_call(
        _na_kernel,
        grid=(batch, nblk),
        in_specs=[pl.BlockSpec((NA_HEADS, LANES, NA_BLK), lambda b, j: (0, 0, b * nblk + j)),
                  k_spec(-1), k_spec(0), k_spec(1), vt_spec(-1), vt_spec(0), vt_spec(1),
                  pl.BlockSpec((N_META, NA_WIDTH), lambda b, j: (meta_blk0 + b, 0)),
                  pl.BlockSpec((None, NA_HEADS, V_EXT, N_META), lambda b, j: (b, 0, 0, 0)),
                  pl.BlockSpec((None, NA_HEADS, 3 * NA_BLK, NA_BLK), cls)],
        out_specs=pl.BlockSpec((NA_HEADS, NA_HEAD_DIM, NA_BLK), lambda b, j: (0, 0, b * nblk + j)),
        out_shape=jax.ShapeDtypeStruct((NA_HEADS, NA_HEAD_DIM, batch * seq), _BF),
        compiler_params=pltpu.CompilerParams(dimension_semantics=("parallel", "arbitrary"),
                                             vmem_limit_bytes=VMEM_LIMIT),
        name="na_attn",
    )(qt, k, k, k, vt, vt, vt, k, vmt, bias_t)


def _na_meta_kernel(q_ref, k_ref, v_ref, o_ref):
    for h in range(NA_HEADS):
        sl = slice(h * NA_HEAD_DIM, (h + 1) * NA_HEAD_DIM)
        s = _dot_nt(q_ref[:, sl], k_ref[:, sl])
        p = jnp.exp2(s - jnp.max(s, axis=1, keepdims=True))
        l = jnp.sum(p, axis=1, keepdims=True)
        o_ref[:, sl] = (_dot(p.astype(_BF), v_ref[:, sl]) / l).astype(_BF)


def _na_meta_attention(q, k, v, batch):
    spec = pl.BlockSpec((N_META, NA_WIDTH), lambda b: (b, 0))
    return pl.pallas_call(
        _na_meta_kernel,
        grid=(batch,),
        in_specs=[spec, spec, spec],
        out_specs=spec,
        out_shape=jax.ShapeDtypeStruct((batch * N_META, NA_WIDTH), _BF),
        compiler_params=pltpu.CompilerParams(dimension_semantics=("parallel",)),
        name="na_meta_attn",
    )(q, k, v)


V_EXT = V_HEAD + 16
MLA_CHUNK = 256
MLA_LOOKAHEAD = 6
MLA_QSUB = 256
MLA_TQ = 512
MLA_TK = 4096


def _mla_kernel(qt_ref, k_ref, vt_ref, km_ref, vmt_ref, ot_ref, m_sc, acc_sc, *, chunk):
    j = pl.program_id(2)
    tq = qt_ref.shape[2]
    nchunks = k_ref.shape[1] // chunk

    @pl.when(j == 0)
    def _():
        def init_head(h, carry):
            s = _dot(km_ref[h], qt_ref[h])
            m = jnp.max(s, axis=0, keepdims=True)
            p = jnp.exp2(s - m).astype(_BF)
            acc_sc[h] = _dot(vmt_ref[h], p)
            m_sc[h] = jnp.broadcast_to(m, (8, tq))
            return carry
        lax.fori_loop(0, MLA_HEADS, init_head, 0)

    qsub = min(MLA_QSUB, tq)
    steps = [(qs, h, c) for qs in range(tq // qsub) for h in range(MLA_HEADS) for c in range(nchunks)]
    lanes = lambda qs: slice(qs * qsub, (qs + 1) * qsub)
    scores = lambda qs, h, c: _dot(k_ref[h, c * chunk:(c + 1) * chunk, :], qt_ref[h, :, lanes(qs)])
    pending = [scores(*st) for st in steps[:MLA_LOOKAHEAD]]
    for n, (qs, h, c) in enumerate(steps):
        if n + MLA_LOOKAHEAD < len(steps):
            pending.append(scores(*steps[n + MLA_LOOKAHEAD]))
        s = pending.pop(0)
        if c == 0:
            m = m_sc[h, :1, lanes(qs)]
            acc = acc_sc[h, :, lanes(qs)]
        m_new = jnp.maximum(m, jnp.max(s, axis=0, keepdims=True))
        p = jnp.exp2(s - m_new).astype(_BF)
        acc = acc * jnp.exp2(m - m_new) + _dot(vt_ref[h, :, c * chunk:(c + 1) * chunk], p)
        m = m_new
        if c == nchunks - 1:
            m_sc[h, :, lanes(qs)] = jnp.broadcast_to(m, (8, qsub))
            acc_sc[h, :, lanes(qs)] = acc

    @pl.when(j == pl.num_programs(2) - 1)
    def _():
        for h in range(MLA_HEADS):
            acc = acc_sc[h]
            ot_ref[h] = (acc[:V_HEAD] / acc[V_HEAD:V_HEAD + 1]).astype(_BF)


def _mla_attention(qt, k, vt, km, vmt, batch, seq, q_rows, q_blk0, q_stride, tq, tk):
    assert q_rows % tq == 0 and seq % tk == 0 and tk % MLA_CHUNK == 0
    nq = q_rows // tq
    nk = seq // tk
    return pl.pallas_call(
        functools.partial(_mla_kernel, chunk=min(MLA_CHUNK, tk)),
        grid=(batch, nq, nk),
        in_specs=[pl.BlockSpec((MLA_HEADS, LANES, tq), lambda b, i, j: (0, 0, q_blk0 + b * q_stride + i)),
                  pl.BlockSpec((MLA_HEADS, tk, LANES), lambda b, i, j: (0, b * nk + j, 0)),
                  pl.BlockSpec((MLA_HEADS, V_EXT, tk), lambda b, i, j: (0, 0, b * nk + j)),
                  pl.BlockSpec((None, MLA_HEADS, N_META, LANES), lambda b, i, j: (b, 0, 0, 0)),
                  pl.BlockSpec((None, MLA_HEADS, V_EXT, N_META), lambda b, i, j: (b, 0, 0, 0))],
        out_specs=pl.BlockSpec((MLA_HEADS, V_HEAD, tq), lambda b, i, j: (0, 0, b * nq + i)),
        out_shape=jax.ShapeDtypeStruct((MLA_HEADS, V_HEAD, batch * q_rows), _BF),
        scratch_shapes=[pltpu.VMEM((MLA_HEADS, 8, tq), _F32), pltpu.VMEM((MLA_HEADS, V_EXT, tq), _F32)],
        compiler_params=pltpu.CompilerParams(dimension_semantics=("parallel", "parallel", "arbitrary"),
                                             vmem_limit_bytes=VMEM_LIMIT),
        name="mla_attn",
    )(qt, k, vt, km, vmt)


def _outproj_kernel(x_ref, xm_ref, ong_ref, onm_ref, omg_ref, omm_ref, wo_na_ref, wo_mla_ref, g_ref, wr_hi_ref,
                    wr_lo_ref, h_ref, hn_ref, aff_ref, *, n_grid_tiles, n_valid):
    i = pl.program_id(0)
    is_meta = i == n_grid_tiles
    tn = lambda a_t, b: lax.dot_general(a_t, b, (((0,), (0,)), ((), ())), preferred_element_type=_F32)
    wr_hi = wr_hi_ref[...]
    subs = [slice(r0, r0 + OUT_SUB) for r0 in range(0, ROW_TILE, OUT_SUB)]
    hs = []
    for rows in subs:
        o_na_t = jnp.where(is_meta, onm_ref[:, rows], ong_ref[:, rows])
        o_mla_t = jnp.where(is_meta, omm_ref[:, rows], omg_ref[:, rows])
        x = jnp.where(is_meta, xm_ref[rows, :], x_ref[rows, :])
        hs.append(x + tn(o_na_t, wo_na_ref[...]) + tn(o_mla_t, wo_mla_ref[...]))
    split = []
    for rows, h in zip(subs, hs):
        h_ref[rows, :] = h
        hn = _rms(h, g_ref[...])
        hn_hi = hn.astype(_BF)
        hn_ref[rows, :] = hn_hi
        split.append((hn_hi, (hn - hn_hi.astype(_F32)).astype(_BF)))
    for rows, (hn_hi, hn_lo) in zip(subs, split):
        r0 = rows.start
        logits = _dot_nt(wr_hi, hn_hi) + _dot_nt(wr_hi, hn_lo) + _dot_nt(wr_lo_ref[...], hn_hi)
        e = jnp.exp(logits - jnp.max(logits, axis=0, keepdims=True))
        aff = e / jnp.sum(e, axis=0, keepdims=True)
        row = i * ROW_TILE + r0 + lax.broadcasted_iota(jnp.int32, aff.shape, 1)
        aff_ref[:, rows] = jnp.where(row < n_valid, aff, -1.0)


def _outproj(x, x_meta, o_na_g, o_na_m, o_mla_g, o_mla_m, wo_na, wo_mla, g_ffn, wr_hi, wr_lo, n_valid):
    n_grid_tiles = x.shape[0] // ROW_TILE
    n_tiles = n_grid_tiles + 1
    np_rows = n_tiles * ROW_TILE
    row = lambda i: (i, 0)
    grid_col = lambda i: (0, jnp.minimum(i, n_grid_tiles - 1))
    full = lambda i: (0, 0)
    w = lambda a: pl.BlockSpec(a.shape, full)
    return pl.pallas_call(
        functools.partial(_outproj_kernel, n_grid_tiles=n_grid_tiles, n_valid=n_valid),
        grid=(n_tiles,),
        in_specs=[pl.BlockSpec((ROW_TILE, D_MODEL), lambda i: (jnp.minimum(i, n_grid_tiles - 1), 0)), w(x_meta),
                  pl.BlockSpec((NA_WIDTH, ROW_TILE), grid_col), w(o_na_m),
                  pl.BlockSpec((MLA_WIDTH, ROW_TILE), grid_col), w(o_mla_m),
                  w(wo_na), w(wo_mla), w(g_ffn), w(wr_hi), w(wr_lo)],
        out_specs=[pl.BlockSpec((ROW_TILE, D_MODEL), row), pl.BlockSpec((ROW_TILE, D_MODEL), row),
                   pl.BlockSpec((None, N_EXPERTS, ROW_TILE), lambda i: (i, 0, 0))],
        out_shape=[jax.ShapeDtypeStruct((np_rows, D_MODEL), _F32), jax.ShapeDtypeStruct((np_rows, D_MODEL), _BF),
                   jax.ShapeDtypeStruct((n_tiles, N_EXPERTS, ROW_TILE), _F32)],
        compiler_params=pltpu.CompilerParams(dimension_semantics=("parallel",), vmem_limit_bytes=VMEM_LIMIT),
        name="outproj_router",
    )(x, x_meta, o_na_g, o_na_m, o_mla_g, o_mla_m, wo_na, wo_mla, g_ffn, wr_hi, wr_lo)


XE_ALIGN = 16
SLOT_W = 64
WIN = 64


def _route_kernel(aff_ref, slot_ref, lo_ref, *, cap):
    n_tiles = aff_ref.shape[0]
    bits = lambda: pltpu.bitcast(aff_ref[...], jnp.int32)

    def bit_step(i, cur):
        cand = cur | jnp.left_shift(jnp.int32(1), 30 - i)
        cnt = jnp.sum((bits() >= cand).astype(jnp.int32), axis=(0, 2), keepdims=True)
        return jnp.where(cnt >= cap, cand, cur)
    thr = lax.fori_loop(0, 31, bit_step, jnp.zeros((1, N_EXPERTS, 1), jnp.int32))
    n_gt = jnp.sum((bits() > thr).astype(jnp.int32), axis=(0, 2), keepdims=True)
    need = (cap - n_gt)[0].astype(_F32)
    thr = thr[0]
    r = lax.broadcasted_iota(jnp.int32, (ROW_TILE, ROW_TILE), 0)
    c = lax.broadcasted_iota(jnp.int32, (ROW_TILE, ROW_TILE), 1)
    tri = jnp.where(r <= c, 1.0, 0.0).astype(_BF)
    as_bf = lambda m: jnp.where(m, 1.0, 0.0).astype(_BF)

    def tile_step(t, carry):
        tie_before, kept_before = carry
        b = pltpu.bitcast(aff_ref[t], jnp.int32)
        eq = b == thr
        tie_rank = _dot(as_bf(eq), tri) + tie_before
        keep = (b > thr) | (eq & (tie_rank <= need))
        kept_incl = _dot(as_bf(keep), tri) + kept_before
        slot_ref[t] = jnp.where(keep, kept_incl - 1.0, -1.0).astype(jnp.int32)
        lo_ref[t] = jnp.broadcast_to(kept_before, (N_EXPERTS, LANES)).astype(jnp.int32)
        return tie_rank[:, ROW_TILE - 1:], kept_incl[:, ROW_TILE - 1:]
    zero = jnp.zeros((N_EXPERTS, 1), _F32)
    lax.fori_loop(0, n_tiles, tile_step, (zero, zero))


def _route(aff, cap):
    n_tiles = aff.shape[0]
    full = lambda shape: pl.BlockSpec(shape, lambda i: (0, 0, 0))
    return pl.pallas_call(
        functools.partial(_route_kernel, cap=cap),
        grid=(1,),
        in_specs=[full(aff.shape)],
        out_specs=[full(aff.shape), full((n_tiles, N_EXPERTS, LANES))],
        out_shape=[jax.ShapeDtypeStruct(aff.shape, jnp.int32),
                   jax.ShapeDtypeStruct((n_tiles, N_EXPERTS, LANES), jnp.int32)],
        compiler_params=pltpu.CompilerParams(dimension_semantics=("arbitrary",), vmem_limit_bytes=VMEM_LIMIT),
        name="route",
    )(aff)


def _tile_max(lo_ref, i, start_of):
    m = lo_ref[(i + 1) * N_EXPERTS] - start_of(0)
    for e in range(1, N_EXPERTS):
        m = jnp.maximum(m, lo_ref[(i + 1) * N_EXPERTS + e] - start_of(e))
    return m


def _dispatch_kernel(lo_ref, hn_ref, slot_ref, xe_ref, stage, stage_extra, carry, sem, sem_extra, *, zero_from,
                     zero_rows):
    i = pl.program_id(0)
    s = i % 2
    base = lambda e: (lo_ref[i * N_EXPERTS + e] // XE_ALIGN) * XE_ALIGN
    next_base = lambda e: (lo_ref[(i + 1) * N_EXPERTS + e] // XE_ALIGN) * XE_ALIGN

    def compact(k, dst):
        w = lax.broadcasted_iota(jnp.int32, (SLOT_W, ROW_TILE), 0) + k * SLOT_W
        onehot = jnp.concatenate(
            [jnp.where(slot_ref[e:e + 1, :] - base(e) == w, 1.0, 0.0).astype(_BF) for e in range(N_EXPERTS)], axis=0)
        dst[...] = _dot(onehot, hn_ref[...]).astype(_BF)

    def copy(e, src, k, sm):
        dst_row = pl.multiple_of(base(e) + k * SLOT_W, XE_ALIGN)
        return pltpu.make_async_copy(src.at[pl.ds(e * SLOT_W, SLOT_W)], xe_ref.at[e, pl.ds(dst_row, SLOT_W)], sm)

    def save_carry(k, src):
        for e in range(N_EXPERTS):
            g = next_base(e) - base(e) - k * SLOT_W

            @pl.when((g >= 0) & (g < SLOT_W))
            def _():
                carry[e] = src[pl.ds(pl.multiple_of(e * SLOT_W + g, XE_ALIGN), XE_ALIGN), :]

    @pl.when(i == 0)
    def _():
        carry[...] = jnp.zeros(carry.shape, _BF)
        stage_extra[...] = jnp.zeros(stage_extra.shape, _BF)
        zero_copies = [pltpu.make_async_copy(stage_extra.at[pl.ds(0, zero_rows)],
                                             xe_ref.at[e, pl.ds(zero_from, zero_rows)], sem_extra)
                       for e in range(N_EXPERTS)]
        for cp in zero_copies:
            cp.start()
        for cp in zero_copies:
            cp.wait()

    compact(0, stage.at[s])
    for e in range(N_EXPERTS):
        stage[s, e * SLOT_W:e * SLOT_W + XE_ALIGN, :] += carry[e]
    save_carry(0, stage.at[s])

    @pl.when(i > 0)
    def _():
        for e in range(N_EXPERTS):
            copy(e, stage.at[1 - s], 0, sem.at[1 - s]).wait()

    for e in range(N_EXPERTS):
        copy(e, stage.at[s], 0, sem.at[s]).start()

    def extra_pass(k, c):
        compact(k, stage_extra)
        save_carry(k, stage_extra)
        for wait in (False, True):
            for e in range(N_EXPERTS):
                @pl.when(lo_ref[(i + 1) * N_EXPERTS + e] - base(e) >= k * SLOT_W)
                def _():
                    cp = copy(e, stage_extra, k, sem_extra)
                    cp.wait() if wait else cp.start()
        return c
    lax.fori_loop(1, _tile_max(lo_ref, i, base) // SLOT_W + 1, extra_pass, 0)

    @pl.when(i == pl.num_programs(0) - 1)
    def _():
        for e in range(N_EXPERTS):
            copy(e, stage.at[s], 0, sem.at[s]).wait()


def _dispatch(lo, hn, slot, cap, cap_pad):
    n_tiles = slot.shape[0]
    zero_from = cap // XE_ALIGN * XE_ALIGN
    zero_rows = cap_pad + SLOT_W - zero_from
    assert zero_rows <= N_EXPERTS * SLOT_W and SLOT_W % XE_ALIGN == 0 and cap_pad % XE_ALIGN == 0
    return pl.pallas_call(
        functools.partial(_dispatch_kernel, zero_from=zero_from, zero_rows=zero_rows),
        grid_spec=pltpu.PrefetchScalarGridSpec(
            num_scalar_prefetch=1,
            grid=(n_tiles,),
            in_specs=[pl.BlockSpec((ROW_TILE, D_MODEL), lambda i, lo: (i, 0)),
                      pl.BlockSpec((None, N_EXPERTS, ROW_TILE), lambda i, lo: (i, 0, 0))],
            out_specs=pl.BlockSpec(memory_space=pl.ANY),
            scratch_shapes=[pltpu.VMEM((2, N_EXPERTS * SLOT_W, D_MODEL), _BF),
                            pltpu.VMEM((N_EXPERTS * SLOT_W, D_MODEL), _BF),
                            pltpu.VMEM((N_EXPERTS, XE_ALIGN, D_MODEL), _BF),
                            pltpu.SemaphoreType.DMA((2,)), pltpu.SemaphoreType.DMA(())]),
        out_shape=jax.ShapeDtypeStruct((N_EXPERTS, cap_pad + SLOT_W, D_MODEL), _BF),
        compiler_params=pltpu.CompilerParams(dimension_semantics=("arbitrary",), vmem_limit_bytes=VMEM_LIMIT),
        name="dispatch",
    )(lo, hn, slot)


def _expert_kernel(x_ref, wg_ref, wu_ref, wd_ref, y_ref):
    x = x_ref[...]
    g = _dot(x, wg_ref[...])
    u = _dot(x, wu_ref[...].astype(_BF))
    hid = (g * jax.nn.sigmoid(g) * u).astype(_BF)
    y_ref[...] = _dot(hid, wd_ref[...].astype(_BF)).astype(_BF)


def _experts(xe, wg, wu, wd, cap_pad, tile):
    n_e = xe.shape[0]
    wspec = lambda a: pl.BlockSpec((None,) + a.shape[1:], lambda e, t: (e, 0, 0))
    return pl.pallas_call(
        _expert_kernel,
        grid=(n_e, cap_pad // tile),
        in_specs=[pl.BlockSpec((None, tile, D_MODEL), lambda e, t: (e, t, 0)), wspec(wg), wspec(wu), wspec(wd)],
        out_specs=pl.BlockSpec((None, tile, D_MODEL), lambda e, t: (e, t, 0)),
        out_shape=jax.ShapeDtypeStruct((n_e, cap_pad, D_MODEL), _BF),
        compiler_params=pltpu.CompilerParams(dimension_semantics=("parallel", "arbitrary"),
                                             vmem_limit_bytes=VMEM_LIMIT),
        name="experts",
    )(xe, wg, wu, wd)


def _combine_kernel(lo_ref, h_ref, slot_ref, aff_ref, g_ref, ye_ref, o_ref, win, win_extra, f_acc, sem, sem_extra, *,
                    cap_pad):
    i = pl.program_id(0)
    s = i % 2
    first = lambda t, e: (lo_ref[t * N_EXPERTS + e] // 16) * 16

    def start_row(t, e, k):
        return pl.multiple_of(jnp.minimum(first(t, e) + k * WIN, cap_pad - WIN), 16)

    def window_copies(t, k, buf, sm):
        return [pltpu.make_async_copy(ye_ref.at[e, pl.ds(start_row(t, e, k), WIN)], buf.at[pl.ds(e * WIN, WIN)], sm)
                for e in range(N_EXPERTS)]

    def weights(k):
        w = lax.broadcasted_iota(jnp.int32, (WIN, ROW_TILE), 0)
        parts = []
        for e in range(N_EXPERTS):
            slot = slot_ref[e:e + 1, :]
            hit = (slot == w + start_row(i, e, k)) & (slot >= first(i, e) + k * WIN)
            parts.append(jnp.where(hit, aff_ref[e:e + 1, :], 0.0).astype(_BF))
        return jnp.concatenate(parts, axis=0)

    scatter = lambda wts, rows: lax.dot_general(wts, rows, (((0,), (0,)), ((), ())), preferred_element_type=_F32)

    @pl.when(i == 0)
    def _():
        for cp in window_copies(0, 0, win.at[0], sem.at[0]):
            cp.start()

    @pl.when(i + 1 < pl.num_programs(0))
    def _():
        for cp in window_copies(i + 1, 0, win.at[1 - s], sem.at[1 - s]):
            cp.start()

    wts = weights(0)
    for cp in window_copies(i, 0, win.at[s], sem.at[s]):
        cp.wait()
    f_acc[...] = scatter(wts, win[s])

    def extra_pass(k, c):
        copies = window_copies(i, k, win_extra, sem_extra)
        for cp in copies:
            cp.start()
        wts_k = weights(k)
        for cp in copies:
            cp.wait()
        f_acc[...] += scatter(wts_k, win_extra[...])
        return c
    reach = _tile_max(lo_ref, i, lambda e: first(i, e))
    lax.fori_loop(1, (reach + WIN - 1) // WIN, extra_pass, 0)
    o_ref[...] = _rms(h_ref[...] + f_acc[...], g_ref[...])


def _combine(lo, h, slot, aff, g_final, ye, n_rows):
    cap_pad = ye.shape[1]
    assert cap_pad % 16 == 0 and WIN % 16 == 0 and cap_pad >= WIN
    tile3 = pl.BlockSpec((None, N_EXPERTS, ROW_TILE), lambda i, lo: (i, 0, 0))
    return pl.pallas_call(
        functools.partial(_combine_kernel, cap_pad=cap_pad),
        grid_spec=pltpu.PrefetchScalarGridSpec(
            num_scalar_prefetch=1,
            grid=(n_rows // ROW_TILE,),
            in_specs=[pl.BlockSpec((ROW_TILE, D_MODEL), lambda i, lo: (i, 0)), tile3, tile3,
                      pl.BlockSpec(g_final.shape, lambda i, lo: (0, 0)), pl.BlockSpec(memory_space=pl.ANY)],
            out_specs=pl.BlockSpec((ROW_TILE, D_MODEL), lambda i, lo: (i, 0)),
            scratch_shapes=[pltpu.VMEM((2, N_EXPERTS * WIN, D_MODEL), _BF), pltpu.VMEM((N_EXPERTS * WIN, D_MODEL), _BF),
                            pltpu.VMEM((ROW_TILE, D_MODEL), _F32),
                            pltpu.SemaphoreType.DMA((2,)), pltpu.SemaphoreType.DMA(())]),
        out_shape=jax.ShapeDtypeStruct((n_rows, D_MODEL), _F32),
        compiler_params=pltpu.CompilerParams(dimension_semantics=("arbitrary",), vmem_limit_bytes=VMEM_LIMIT),
        name="combine_norm",
    )(lo, h, slot, aff, g_final, ye)


def _rotate_half_cols(w):
    half = QK_ROPE // 2
    return jnp.concatenate([-w[..., half:], w[..., :half]], axis=-1)


def _prepare_weights(w_in, g_attn, g_q, w_uq, g_kv, w_ukv, w_o, g_ffn, w_router, w_gate, w_up, w_down, na_rpb):
    na_scale = 1.0 / math.sqrt(NA_HEAD_DIM)
    c0, c2 = 3 * NA_WIDTH, 3 * NA_WIDTH + Q_LORA + KV_LORA
    w_kr = w_in[:, c2:]
    zpad = lambda n: jnp.zeros((D_MODEL, n), _F32)
    slot = lambda w: jnp.concatenate([zpad(QK_NOPE), w, zpad(LANES - QK_NOPE - QK_ROPE)], axis=1)
    w1 = jnp.concatenate([w_in[:, NA_WIDTH:2 * NA_WIDTH], w_in[:, c0:c2], slot(w_kr),
                          slot(_rotate_half_cols(w_kr))], axis=1).astype(_BF)
    q_scale = LOG2E / math.sqrt(QK_NOPE + QK_ROPE)
    wq = (w_uq * q_scale).reshape(Q_LORA, MLA_HEADS, QK_NOPE + QK_ROPE)
    pad = LANES - QK_NOPE - QK_ROPE
    plain = jnp.pad(wq, ((0, 0), (0, 0), (0, pad)))
    rot = jnp.pad(_rotate_half_cols(wq[..., QK_NOPE:]), ((0, 0), (0, 0), (QK_NOPE, pad)))
    w2t = jnp.concatenate([plain.reshape(Q_LORA, -1), rot.reshape(Q_LORA, -1)], axis=1).T.astype(_BF)
    wkv = w_ukv.reshape(KV_LORA, MLA_HEADS, QK_NOPE + V_HEAD)
    w3k = jnp.pad(wkv[..., :QK_NOPE], ((0, 0), (0, 0), (0, LANES - QK_NOPE))).reshape(KV_LORA, -1).astype(_BF)
    w3vt = wkv[..., QK_NOPE:].reshape(KV_LORA, -1).T.astype(_BF)
    wr = w_router.T
    wr_hi = wr.astype(_BF)
    wr_lo = (wr - wr_hi.astype(_F32)).astype(_BF)
    return dict(
        w1=w1, w2t=w2t, w3k=w3k, w3vt=w3vt,
        g_attn=g_attn[None], g_q=g_q[None], g_kv=g_kv[None], g_ffn=g_ffn[None],
        wo_na=w_o[:NA_WIDTH].astype(_BF), wo_mla=w_o[NA_WIDTH:].astype(_BF),
        wr_hi=wr_hi, wr_lo=wr_lo,
        wg=w_gate.astype(_BF), wu=w_up, wd=w_down,
        wqvt=jnp.concatenate([w_in[:, :NA_WIDTH] * (na_scale * LOG2E), w_in[:, 2 * NA_WIDTH:3 * NA_WIDTH]],
                             axis=1).T.astype(_BF),
        na_bias_t=_na_bias_table(na_rpb),
    )


def _na_bias_table(rpb):
    ql = np.arange(NA_BLK)
    kl = np.arange(3 * NA_BLK)
    ri, qc = ql // GRID_W, ql % GRID_W
    kj, kc = kl // GRID_W, kl % GRID_W
    qcs = np.clip(qc - WIN_C // 2, 0, GRID_W - WIN_C)
    col_ok = (kc[None, :] >= qcs[:, None]) & (kc[None, :] < qcs[:, None] + WIN_C)
    lo = np.stack([np.full(NA_BLK, NA_ROWS), ri, np.zeros(NA_BLK, np.int64)])
    row_ok = (kj[None, None, :] >= lo[:, :, None]) & (kj[None, None, :] < lo[:, :, None] + WIN_R)
    ok = row_ok & col_ok[None]
    rows_q, rows_k, cols = np.arange(NA_ROWS), np.arange(3 * NA_ROWS), np.arange(GRID_W)
    dr = np.clip(rows_k[None, :] - rows_q[:, None] + (WIN_R - 1) - NA_ROWS, 0, 2 * WIN_R - 2)
    dc = np.clip(cols[None, :] - cols[:, None] + (WIN_C - 1), 0, 2 * WIN_C - 2)
    pick_r = (dr[..., None] == np.arange(2 * WIN_R - 1)).astype(np.float32)
    pick_c = (dc[..., None] == np.arange(2 * WIN_C - 1)).astype(np.float32)
    vals = jnp.einsum("hrc,ijr,qkc->hiqjk", rpb.astype(_F32), pick_r, pick_c, precision=lax.Precision.HIGHEST)
    vals = vals.reshape(NA_HEADS, NA_BLK, 3 * NA_BLK)
    return jnp.where(jnp.asarray(ok)[:, None], vals[None] * LOG2E, NEG).transpose(0, 1, 3, 2)


def _rope_tables(seq):
    half = QK_ROPE // 2
    inv_freq = ROPE_THETA ** (-jnp.arange(half, dtype=_F32) * 2.0 / QK_ROPE)
    pos = jnp.concatenate([jnp.arange(seq, dtype=_F32) + N_META,
                           jnp.tile(jnp.arange(N_META, dtype=_F32), ROW_TILE // N_META)])
    n = seq + ROW_TILE
    pad = LANES - QK_NOPE - QK_ROPE

    def lay(a, nope_fill, axis):
        shape = lambda k: (n, k) if axis == 1 else (k, n)
        return jnp.concatenate([jnp.full(shape(QK_NOPE), nope_fill, _F32), a, a, jnp.zeros(shape(pad), _F32)], axis=axis)
    ang = pos[:, None] * inv_freq[None, :]
    dense = ang.reshape(-1, LANES)
    cos, sin = jnp.cos(dense).reshape(n, half), jnp.sin(dense).reshape(n, half)
    return lay(cos, 1.0, 1), lay(sin, 0.0, 1), lay(cos.T, 1.0, 0), lay(sin.T, 0.0, 0)


def _expert_tile(cap):
    if cap <= 1024:
        return -(-cap // 16) * 16
    return min(range(512, 1025, 16), key=lambda t: (-(-cap // t) * t - cap, -t))


def _run_group(x, meta_tokens, prm, rope, g_final, tq, tk):
    batch, seq, _ = x.shape
    n_grid = batch * seq
    n_valid = n_grid + batch * N_META
    assert seq % NA_BLK == 0 and seq // GRID_W >= WIN_R and batch * N_META <= ROW_TILE
    meta = jnp.broadcast_to(meta_tokens[None], (batch, N_META, D_MODEL)).reshape(batch * N_META, D_MODEL)
    xg = x.reshape(n_grid, D_MODEL)
    xm = jnp.pad(meta, ((0, ROW_TILE - batch * N_META), (0, 0)))
    assert rope[0].shape[0] >= seq + ROW_TILE
    kna, qt, k3, vt, qnat, vnat = _proj(xg, xm, seq, *rope, prm["g_attn"], prm["g_q"], prm["g_kv"],
                                        prm["w1"], prm["w2t"], prm["w3k"], prm["w3vt"], prm["wqvt"])
    n_meta = batch * N_META
    meta_cols = slice(n_grid, n_grid + n_meta)
    vmt_na = vnat[:, :, meta_cols].reshape(NA_HEADS, V_EXT, batch, N_META).transpose(2, 0, 1, 3)
    o_na_g = _na_attention(qnat, kna, vnat, vmt_na, prm["na_bias_t"], batch, seq).reshape(NA_WIDTH, n_grid)
    q_own = jnp.stack([qnat[h, (h % 2) * NA_HEAD_DIM:(h % 2 + 1) * NA_HEAD_DIM, meta_cols] for h in range(NA_HEADS)])
    rows_of = lambda t: t.transpose(2, 0, 1).reshape(n_meta, NA_WIDTH)
    o_na_m = _na_meta_attention(rows_of(q_own), kna[meta_cols], rows_of(vnat[:, :NA_HEAD_DIM, meta_cols]), batch)
    km = k3[:, n_grid:n_grid + n_meta].reshape(MLA_HEADS, batch, N_META, LANES).transpose(1, 0, 2, 3)
    vmt = vt[:, :, n_grid:n_grid + n_meta].reshape(MLA_HEADS, V_EXT, batch, N_META).transpose(2, 0, 1, 3)
    tk = min(tk, seq)
    ot_g = _mla_attention(qt, k3, vt, km, vmt, batch, seq, seq, 0, seq // tq, tq, tk)
    ot_m = _mla_attention(qt, k3, vt, km, vmt, batch, seq, ROW_TILE, n_grid // ROW_TILE, 0, ROW_TILE, tk)
    ot_m = ot_m.reshape(MLA_WIDTH, batch, ROW_TILE)
    ot_m = jnp.concatenate([ot_m[:, b, b * N_META:(b + 1) * N_META] for b in range(batch)], axis=1)
    ot_m = jnp.pad(ot_m, ((0, 0), (0, ROW_TILE - n_meta)))
    o_na_m = jnp.pad(o_na_m.T, ((0, 0), (0, ROW_TILE - n_meta)))
    h, hn, aff = _outproj(xg, xm, o_na_g, o_na_m, ot_g.reshape(MLA_WIDTH, n_grid), ot_m, prm["wo_na"], prm["wo_mla"],
                          prm["g_ffn"], prm["wr_hi"], prm["wr_lo"], n_valid)
    cap = EC_CAPACITY * n_valid // N_EXPERTS
    tile = _expert_tile(cap)
    cap_pad = -(-cap // tile) * tile
    slot, lo3 = _route(aff, cap)
    lo = jnp.concatenate([lo3[:, :, 0], jnp.full((1, N_EXPERTS), cap, jnp.int32)]).reshape(-1)
    xe = _dispatch(lo, hn, slot, cap, cap_pad)
    ye = _experts(xe, prm["wg"], prm["wu"], prm["wd"], cap_pad, tile)
    return _combine(lo, h, slot, aff, g_final[None], ye, n_grid).reshape(batch, seq, D_MODEL)


def kernel(x_prompt, x_sample, meta_tokens, g_attn, w_in, na_rpb, g_q, w_uq, g_kv, w_ukv, w_o, g_ffn, w_router,
           w_gate, w_up, w_down, g_final):
    prm = _prepare_weights(w_in[0], g_attn[0], g_q[0], w_uq[0], g_kv[0], w_ukv[0], w_o[0], g_ffn[0], w_router[0],
                           w_gate[0], w_up[0], w_down[0], na_rpb[0])
    rope = _rope_tables(max(x_prompt.shape[1], x_sample.shape[1]))
    y_prompt = _run_group(x_prompt, meta_tokens, prm, rope, g_final, MLA_TQ, MLA_TK)
    y_sample = _run_group(x_sample, meta_tokens, prm, rope, g_final, MLA_TQ, MLA_TK)
    return (y_prompt, y_sample)
```

```python
import functools
import math

import jax
import jax.numpy as jnp
import numpy as np
from jax import lax
from jax.experimental import pallas as pl
from jax.experimental.pallas import tpu as pltpu

D_MODEL = 1024
GRID_W = 64
N_META = 16
WIN_R = 8
WIN_C = 16
NA_HEADS = 8
NA_HEAD_DIM = 64
MLA_HEADS = 8
QK_NOPE = 64
QK_ROPE = 32
V_HEAD = 64
Q_LORA = 768
KV_LORA = 256
ROPE_THETA = 10000.0
N_EXPERTS = 16
EC_CAPACITY = 2
D_EXPERT = 2048
NORM_EPS = 1e-6
NA_WIDTH = NA_HEADS * NA_HEAD_DIM
MLA_WIDTH = MLA_HEADS * V_HEAD

LANES = 128
ROW_TILE = 256
OUT_SUB = 128
NA_ROWS = 4
NA_BLK = NA_ROWS * GRID_W
NA_LOOKAHEAD = 5
NEG = -1e30
LOG2E = 1.4426950408889634
VMEM_LIMIT = 56 * 1024 * 1024

_BF = jnp.bfloat16
_F32 = jnp.float32


def _dot(a, b):
    return jnp.dot(a, b, preferred_element_type=_F32)


def _dot_nt(a, b):
    return lax.dot_general(a, b, (((1,), (1,)), ((), ())), preferred_element_type=_F32)


def _rms(x, g):
    return x * lax.rsqrt(jnp.mean(x * x, axis=-1, keepdims=True) + NORM_EPS) * g


def _proj_kernel(x_ref, xm_ref, cos_ref, sin_ref, cost_ref, sint_ref, g_attn_ref, g_q_ref, g_kv_ref, w1_ref, w2t_ref,
                 w3k_ref, w3vt_ref, wqvt_ref, kna_ref, qt_ref, k_ref, vt_ref, qnat_ref, vnat_ref):
    x = jnp.where(pl.program_id(0) == pl.num_programs(0) - 1, xm_ref[...], x_ref[...])
    a = _rms(x, g_attn_ref[...]).astype(_BF)
    p = _dot(a, w1_ref[...])
    kna_ref[...] = p[:, :NA_WIDTH].astype(_BF)
    o = NA_WIDTH
    cqn = _rms(p[:, o:o + Q_LORA], g_q_ref[...]).astype(_BF)
    o += Q_LORA
    ckvn = _rms(p[:, o:o + KV_LORA], g_kv_ref[...]).astype(_BF)
    o += KV_LORA
    kr = p[:, o:o + LANES] * cos_ref[...] + p[:, o + LANES:o + 2 * LANES] * sin_ref[...]
    k3 = _dot(ckvn, w3k_ref[...])
    q2t = _dot_nt(w2t_ref[...], cqn)
    vt = _dot_nt(w3vt_ref[...], ckvn)
    cos_t = cost_ref[...]
    sin_t = sint_ref[...]
    hw = MLA_HEADS * LANES
    ones_row = jnp.where(lax.broadcasted_iota(jnp.int32, (V_EXT - V_HEAD, x_ref.shape[0]), 0) == 0, 1.0, 0.0)
    for h in range(MLA_HEADS):
        sl = slice(h * LANES, (h + 1) * LANES)
        qt_ref[h] = (q2t[sl] * cos_t + q2t[hw + h * LANES:hw + (h + 1) * LANES] * sin_t).astype(_BF)
        k_ref[h] = (k3[:, sl] + kr).astype(_BF)
        vt_ref[h, :V_HEAD, :] = vt[h * V_HEAD:(h + 1) * V_HEAD].astype(_BF)
        vt_ref[h, V_HEAD:, :] = ones_row.astype(_BF)
    qvt = _dot_nt(wqvt_ref[...], a)
    zero_half = jnp.zeros((NA_HEAD_DIM, x_ref.shape[0]), _BF)
    for h in range(NA_HEADS):
        own = slice((h % 2) * NA_HEAD_DIM, (h % 2 + 1) * NA_HEAD_DIM)
        other = slice((1 - h % 2) * NA_HEAD_DIM, (2 - h % 2) * NA_HEAD_DIM)
        qnat_ref[h, own, :] = qvt[h * NA_HEAD_DIM:(h + 1) * NA_HEAD_DIM].astype(_BF)
        qnat_ref[h, other, :] = zero_half
        vnat_ref[h, :NA_HEAD_DIM, :] = qvt[NA_WIDTH + h * NA_HEAD_DIM:NA_WIDTH + (h + 1) * NA_HEAD_DIM].astype(_BF)
        vnat_ref[h, NA_HEAD_DIM:, :] = ones_row.astype(_BF)


def _proj(x, x_meta, seq, cos, sin, cos_t, sin_t, g_attn, g_q, g_kv, w1, w2t, w3k, w3vt, wqvt):
    n_grid_tiles = x.shape[0] // ROW_TILE
    np_rows = x.shape[0] + ROW_TILE
    tiles_per_seq = seq // ROW_TILE
    table_meta_tile = cos.shape[0] // ROW_TILE - 1
    table_tile = lambda i: jnp.where(i < n_grid_tiles, i % tiles_per_seq, table_meta_tile)
    row = lambda i: (table_tile(i), 0)
    col = lambda i: (0, table_tile(i))
    full = lambda i: (0, 0)
    w = lambda a: pl.BlockSpec(a.shape, full)
    return pl.pallas_call(
        _proj_kernel,
        grid=(np_rows // ROW_TILE,),
        in_specs=[pl.BlockSpec((ROW_TILE, D_MODEL), lambda i: (jnp.minimum(i, n_grid_tiles - 1), 0)), w(x_meta),
                  pl.BlockSpec((ROW_TILE, LANES), row),
                  pl.BlockSpec((ROW_TILE, LANES), row), pl.BlockSpec((LANES, ROW_TILE), col),
                  pl.BlockSpec((LANES, ROW_TILE), col), w(g_attn), w(g_q), w(g_kv), w(w1), w(w2t), w(w3k), w(w3vt),
                  w(wqvt)],
        out_specs=[pl.BlockSpec((ROW_TILE, NA_WIDTH), lambda i: (i, 0)),
                   pl.BlockSpec((MLA_HEADS, LANES, ROW_TILE), lambda i: (0, 0, i)),
                   pl.BlockSpec((MLA_HEADS, ROW_TILE, LANES), lambda i: (0, i, 0)),
                   pl.BlockSpec((MLA_HEADS, V_EXT, ROW_TILE), lambda i: (0, 0, i)),
                   pl.BlockSpec((NA_HEADS, LANES, ROW_TILE), lambda i: (0, 0, i)),
                   pl.BlockSpec((NA_HEADS, V_EXT, ROW_TILE), lambda i: (0, 0, i))],
        out_shape=[jax.ShapeDtypeStruct((np_rows, NA_WIDTH), _BF),
                   jax.ShapeDtypeStruct((MLA_HEADS, LANES, np_rows), _BF),
                   jax.ShapeDtypeStruct((MLA_HEADS, np_rows, LANES), _BF),
                   jax.ShapeDtypeStruct((MLA_HEADS, V_EXT, np_rows), _BF),
                   jax.ShapeDtypeStruct((NA_HEADS, LANES, np_rows), _BF),
                   jax.ShapeDtypeStruct((NA_HEADS, V_EXT, np_rows), _BF)],
        compiler_params=pltpu.CompilerParams(dimension_semantics=("parallel",), vmem_limit_bytes=VMEM_LIMIT),
        name="proj",
    )(x, x_meta, cos, sin, cos_t, sin_t, g_attn, g_q, g_kv, w1, w2t, w3k, w3vt, wqvt)


def _na_kernel(qt_ref, kp_ref, kc_ref, kn_ref, vtp_ref, vtc_ref, vtn_ref, km_ref, vmt_ref, bias_ref, ot_ref):
    ks = (kp_ref, kc_ref, kn_ref)
    vts = (vtp_ref, vtc_ref, vtn_ref)
    steps = [(h, c) for h in range(NA_HEADS) for c in range(3)]
    lanes = lambda h: slice((h // 2) * LANES, (h // 2 + 1) * LANES)
    scores = lambda h, c: _dot(ks[c][:, lanes(h)], qt_ref[h])
    meta_s = [_dot(km_ref[:, lanes(h)], qt_ref[h]) for h in range(NA_HEADS)]
    meta_m = [jnp.max(sm, axis=0, keepdims=True) for sm in meta_s]
    meta_acc = [_dot(vmt_ref[h], jnp.exp2(meta_s[h] - meta_m[h]).astype(_BF)) for h in range(NA_HEADS)]
    pending = [scores(*st) for st in steps[:NA_LOOKAHEAD]]
    for n, (h, c) in enumerate(steps):
        if n + NA_LOOKAHEAD < len(steps):
            pending.append(scores(*steps[n + NA_LOOKAHEAD]))
        s = bias_ref[h, c * NA_BLK:(c + 1) * NA_BLK, :] + pending.pop(0)
        if c == 0:
            m, acc = meta_m[h], meta_acc[h]
        m_new = jnp.maximum(m, jnp.max(s, axis=0, keepdims=True))
        p = jnp.exp2(s - m_new).astype(_BF)
        acc = acc * jnp.exp2(m - m_new) + _dot(vts[c][h], p)
        m = m_new
        if c == 2:
            ot_ref[h] = (acc[:NA_HEAD_DIM] / acc[NA_HEAD_DIM:NA_HEAD_DIM + 1]).astype(_BF)


def _na_attention(qt, k, vt, vmt, bias_t, batch, seq):
    nblk = seq // NA_BLK
    meta_blk0 = batch * seq // N_META
    near = lambda b, j, off: b * nblk + jnp.clip(j + off, 0, nblk - 1)
    k_spec = lambda off: pl.BlockSpec((NA_BLK, NA_WIDTH), lambda b, j: (near(b, j, off), 0))
    vt_spec = lambda off: pl.BlockSpec((NA_HEADS, V_EXT, NA_BLK), lambda b, j: (0, 0, near(b, j, off)))
    cls = lambda b, j: (jnp.where(j == 0, 0, jnp.where(j == nblk - 1, 2, 1)), 0, 0, 0)
    return pl.pallas_call(
        _na_kernel,
        grid=(batch, nblk),
        in_specs=[pl.BlockSpec((NA_HEADS, LANES, NA_BLK), lambda b, j: (0, 0, b * nblk + j)),
                  k_spec(-1), k_spec(0), k_spec(1), vt_spec(-1), vt_spec(0), vt_spec(1),
                  pl.BlockSpec((N_META, NA_WIDTH), lambda b, j: (meta_blk0 + b, 0)),
                  pl.BlockSpec((None, NA_HEADS, V_EXT, N_META), lambda b, j: (b, 0, 0, 0)),
                  pl.BlockSpec((None, NA_HEADS, 3 * NA_BLK, NA_BLK), cls)],
        out_specs=pl.BlockSpec((NA_HEADS, NA_HEAD_DIM, NA_BLK), lambda b, j: (0, 0, b * nblk + j)),
        out_shape=jax.ShapeDtypeStruct((NA_HEADS, NA_HEAD_DIM, batch * seq), _BF),
        compiler_params=pltpu.CompilerParams(dimension_semantics=("parallel", "arbitrary"),
                                             vmem_limit_bytes=VMEM_LIMIT),
        name="na_attn",
    )(qt, k, k, k, vt, vt, vt, k, vmt, bias_t)


def _na_meta_kernel(q_ref, k_ref, v_ref, o_ref):
    for h in range(NA_HEADS):
        sl = slice(h * NA_HEAD_DIM, (h + 1) * NA_HEAD_DIM)
        s = _dot_nt(q_ref[:, sl], k_ref[:, sl])
        p = jnp.exp2(s - jnp.max(s, axis=1, keepdims=True))
        l = jnp.sum(p, axis=1, keepdims=True)
        o_ref[:, sl] = (_dot(p.astype(_BF), v_ref[:, sl]) / l).astype(_BF)


def _na_meta_attention(q, k, v, batch):
    spec = pl.BlockSpec((N_META, NA_WIDTH), lambda b: (b, 0))
    return pl.pallas_call(
        _na_meta_kernel,
        grid=(batch,),
        in_specs=[spec, spec, spec],
        out_specs=spec,
        out_shape=jax.ShapeDtypeStruct((batch * N_META, NA_WIDTH), _BF),
        compiler_params=pltpu.CompilerParams(dimension_semantics=("parallel",)),
        name="na_meta_attn",
    )(q, k, v)


V_EXT = V_HEAD + 16
MLA_CHUNK = 256
MLA_LOOKAHEAD = 6
MLA_QSUB = 256
MLA_TQ = 512
MLA_KV_SLOTS = 3
MLA_TK = 4096


def _mla_kernel(qt_ref, k_hbm, vt_hbm, km_ref, vmt_ref, ot_ref, kbuf, vbuf, sem, m_sc, acc_sc, *, chunk, tk):
    j = pl.program_id(2)
    nq, nk = pl.num_programs(1), pl.num_programs(2)
    tq = qt_ref.shape[2]
    nchunks = tk // chunk
    step = (pl.program_id(0) * nq + pl.program_id(1)) * nk + j
    n_steps = pl.num_programs(0) * nq * nk

    def kv_copies(t):
        blk = (t // (nq * nk)) * nk + t % nk
        rows = pl.ds(pl.multiple_of(blk * tk, tk), tk)
        slot = t % MLA_KV_SLOTS
        return (pltpu.make_async_copy(k_hbm.at[:, rows, :], kbuf.at[slot], sem.at[0, slot]),
                pltpu.make_async_copy(vt_hbm.at[:, :, rows], vbuf.at[slot], sem.at[1, slot]))

    for ahead in range(MLA_KV_SLOTS - 1):
        @pl.when((step == 0) & (ahead < n_steps))
        def _():
            for cp in kv_copies(ahead):
                cp.start()

    @pl.when(step + MLA_KV_SLOTS - 1 < n_steps)
    def _():
        for cp in kv_copies(step + MLA_KV_SLOTS - 1):
            cp.start()

    for cp in kv_copies(step):
        cp.wait()
    k_ref = kbuf.at[step % MLA_KV_SLOTS]
    vt_ref = vbuf.at[step % MLA_KV_SLOTS]

    @pl.when(j == 0)
    def _():
        def init_head(h, carry):
            s = _dot(km_ref[h], qt_ref[h])
            m = jnp.max(s, axis=0, keepdims=True)
            p = jnp.exp2(s - m).astype(_BF)
            acc_sc[h] = _dot(vmt_ref[h], p)
            m_sc[h] = jnp.broadcast_to(m, (8, tq))
            return carry
        lax.fori_loop(0, MLA_HEADS, init_head, 0)

    qsub = min(MLA_QSUB, tq)
    steps = [(qs, h, c) for qs in range(tq // qsub) for h in range(MLA_HEADS) for c in range(nchunks)]
    lanes = lambda qs: slice(qs * qsub, (qs + 1) * qsub)
    scores = lambda qs, h, c: _dot(k_ref[h, c * chunk:(c + 1) * chunk, :], qt_ref[h, :, lanes(qs)])
    pending = [scores(*st) for st in steps[:MLA_LOOKAHEAD]]
    for n, (qs, h, c) in enumerate(steps):
        if n + MLA_LOOKAHEAD < len(steps):
            pending.append(scores(*steps[n + MLA_LOOKAHEAD]))
        s = pending.pop(0)
        if c == 0:
            m = m_sc[h, :1, lanes(qs)]
            acc = acc_sc[h, :, lanes(qs)]
        m_new = jnp.maximum(m, jnp.max(s, axis=0, keepdims=True))
        p = jnp.exp2(s - m_new).astype(_BF)
        acc = acc * jnp.exp2(m - m_new) + _dot(vt_ref[h, :, c * chunk:(c + 1) * chunk], p)
        m = m_new
        if c == nchunks - 1:
            m_sc[h, :, lanes(qs)] = jnp.broadcast_to(m, (8, qsub))
            acc_sc[h, :, lanes(qs)] = acc

    @pl.when(j == pl.num_programs(2) - 1)
    def _():
        for h in range(MLA_HEADS):
            acc = acc_sc[h]
            ot_ref[h] = (acc[:V_HEAD] / acc[V_HEAD:V_HEAD + 1]).astype(_BF)


def _mla_attention(qt, k, vt, km, vmt, batch, seq, q_rows, q_blk0, q_stride, tq, tk):
    assert q_rows % tq == 0 and seq % tk == 0 and tk % MLA_CHUNK == 0
    nq = q_rows // tq
    nk = seq // tk
    return pl.pallas_call(
        functools.partial(_mla_kernel, chunk=min(MLA_CHUNK, tk), tk=tk),
        grid=(batch, nq, nk),
        in_specs=[pl.BlockSpec((MLA_HEADS, LANES, tq), lambda b, i, j: (0, 0, q_blk0 + b * q_stride + i)),
                  pl.BlockSpec(memory_space=pl.ANY), pl.BlockSpec(memory_space=pl.ANY),
                  pl.BlockSpec((None, MLA_HEADS, N_META, LANES), lambda b, i, j: (b, 0, 0, 0)),
                  pl.BlockSpec((None, MLA_HEADS, V_EXT, N_META), lambda b, i, j: (b, 0, 0, 0))],
        out_specs=pl.BlockSpec((MLA_HEADS, V_HEAD, tq), lambda b, i, j: (0, 0, b * nq + i)),
        out_shape=jax.ShapeDtypeStruct((MLA_HEADS, V_HEAD, batch * q_rows), _BF),
        scratch_shapes=[pltpu.VMEM((MLA_KV_SLOTS, MLA_HEADS, tk, LANES), _BF),
                        pltpu.VMEM((MLA_KV_SLOTS, MLA_HEADS, V_EXT, tk), _BF),
                        pltpu.SemaphoreType.DMA((2, MLA_KV_SLOTS)),
                        pltpu.VMEM((MLA_HEADS, 8, tq), _F32), pltpu.VMEM((MLA_HEADS, V_EXT, tq), _F32)],
        compiler_params=pltpu.CompilerParams(dimension_semantics=("arbitrary", "arbitrary", "arbitrary"),
                                             vmem_limit_bytes=VMEM_LIMIT),
        name="mla_attn",
    )(qt, k, vt, km, vmt)


def _outproj_kernel(x_ref, xm_ref, ong_ref, onm_ref, omg_ref, omm_ref, wo_na_ref, wo_mla_ref, g_ref, wr_hi_ref,
                    wr_lo_ref, h_ref, hn_ref, aff_ref, *, n_grid_tiles, n_valid):
    i = pl.program_id(0)
    is_meta = i == n_grid_tiles
    tn = lambda a_t, b: lax.dot_general(a_t, b, (((0,), (0,)), ((), ())), preferred_element_type=_F32)
    wr_hi = wr_hi_ref[...]
    subs = [slice(r0, r0 + OUT_SUB) for r0 in range(0, ROW_TILE, OUT_SUB)]
    hs = []
    for rows in subs:
        o_na_t = jnp.where(is_meta, onm_ref[:, rows], ong_ref[:, rows])
        o_mla_t = jnp.where(is_meta, omm_ref[:, rows], omg_ref[:, rows])
        x = jnp.where(is_meta, xm_ref[rows, :], x_ref[rows, :])
        hs.append(x + tn(o_na_t, wo_na_ref[...]) + tn(o_mla_t, wo_mla_ref[...]))
    split = []
    for rows, h in zip(subs, hs):
        h_ref[rows, :] = h
        hn = _rms(h, g_ref[...])
        hn_hi = hn.astype(_BF)
        hn_ref[rows, :] = hn_hi
        split.append((hn_hi, (hn - hn_hi.astype(_F32)).astype(_BF)))
    for rows, (hn_hi, hn_lo) in zip(subs, split):
        r0 = rows.start
        logits = _dot_nt(wr_hi, hn_hi) + _dot_nt(wr_hi, hn_lo) + _dot_nt(wr_lo_ref[...], hn_hi)
        e = jnp.exp(logits - jnp.max(logits, axis=0, keepdims=True))
        aff = e / jnp.sum(e, axis=0, keepdims=True)
        row = i * ROW_TILE + r0 + lax.broadcasted_iota(jnp.int32, aff.shape, 1)
        aff_ref[:, rows] = jnp.where(row < n_valid, aff, -1.0)


def _outproj(x, x_meta, o_na_g, o_na_m, o_mla_g, o_mla_m, wo_na, wo_mla, g_ffn, wr_hi, wr_lo, n_valid):
    n_grid_tiles = x.shape[0] // ROW_TILE
    n_tiles = n_grid_tiles + 1
    np_rows = n_tiles * ROW_TILE
    row = lambda i: (i, 0)
    grid_col = lambda i: (0, jnp.minimum(i, n_grid_tiles - 1))
    full = lambda i: (0, 0)
    w = lambda a: pl.BlockSpec(a.shape, full)
    return pl.pallas_call(
        functools.partial(_outproj_kernel, n_grid_tiles=n_grid_tiles, n_valid=n_valid),
        grid=(n_tiles,),
        in_specs=[pl.BlockSpec((ROW_TILE, D_MODEL), lambda i: (jnp.minimum(i, n_grid_tiles - 1), 0)), w(x_meta),
                  pl.BlockSpec((NA_WIDTH, ROW_TILE), grid_col), w(o_na_m),
                  pl.BlockSpec((MLA_WIDTH, ROW_TILE), grid_col), w(o_mla_m),
                  w(wo_na), w(wo_mla), w(g_ffn), w(wr_hi), w(wr_lo)],
        out_specs=[pl.BlockSpec((ROW_TILE, D_MODEL), row), pl.BlockSpec((ROW_TILE, D_MODEL), row),
                   pl.BlockSpec((None, N_EXPERTS, ROW_TILE), lambda i: (i, 0, 0))],
        out_shape=[jax.ShapeDtypeStruct((np_rows, D_MODEL), _F32), jax.ShapeDtypeStruct((np_rows, D_MODEL), _BF),
                   jax.ShapeDtypeStruct((n_tiles, N_EXPERTS, ROW_TILE), _F32)],
        compiler_params=pltpu.CompilerParams(dimension_semantics=("parallel",), vmem_limit_bytes=VMEM_LIMIT),
        name="outproj_router",
    )(x, x_meta, o_na_g, o_na_m, o_mla_g, o_mla_m, wo_na, wo_mla, g_ffn, wr_hi, wr_lo)


XE_ALIGN = 16
SLOT_W = 64
WIN = 64


def _route_kernel(aff_ref, slot_ref, lo_ref, *, cap):
    n_tiles = aff_ref.shape[0]
    bits = lambda: pltpu.bitcast(aff_ref[...], jnp.int32)

    def bit_step(i, cur):
        cand = cur | jnp.left_shift(jnp.int32(1), 30 - i)
        cnt = jnp.sum((bits() >= cand).astype(jnp.int32), axis=(0, 2), keepdims=True)
        return jnp.where(cnt >= cap, cand, cur)
    thr = lax.fori_loop(0, 31, bit_step, jnp.zeros((1, N_EXPERTS, 1), jnp.int32))
    n_gt = jnp.sum((bits() > thr).astype(jnp.int32), axis=(0, 2), keepdims=True)
    need = (cap - n_gt)[0].astype(_F32)
    thr = thr[0]
    r = lax.broadcasted_iota(jnp.int32, (ROW_TILE, ROW_TILE), 0)
    c = lax.broadcasted_iota(jnp.int32, (ROW_TILE, ROW_TILE), 1)
    tri = jnp.where(r <= c, 1.0, 0.0).astype(_BF)
    as_bf = lambda m: jnp.where(m, 1.0, 0.0).astype(_BF)

    def tile_step(t, carry):
        tie_before, kept_before = carry
        b = pltpu.bitcast(aff_ref[t], jnp.int32)
        eq = b == thr
        tie_rank = _dot(as_bf(eq), tri) + tie_before
        keep = (b > thr) | (eq & (tie_rank <= need))
        kept_incl = _dot(as_bf(keep), tri) + kept_before
        slot_ref[t] = jnp.where(keep, kept_incl - 1.0, -1.0).astype(jnp.int32)
        lo_ref[t] = jnp.broadcast_to(kept_before, (N_EXPERTS, LANES)).astype(jnp.int32)
        return tie_rank[:, ROW_TILE - 1:], kept_incl[:, ROW_TILE - 1:]
    zero = jnp.zeros((N_EXPERTS, 1), _F32)
    lax.fori_loop(0, n_tiles, tile_step, (zero, zero))


def _route(aff, cap):
    n_tiles = aff.shape[0]
    full = lambda shape: pl.BlockSpec(shape, lambda i: (0, 0, 0))
    return pl.pallas_call(
        functools.partial(_route_kernel, cap=cap),
        grid=(1,),
        in_specs=[full(aff.shape)],
        out_specs=[full(aff.shape), full((n_tiles, N_EXPERTS, LANES))],
        out_shape=[jax.ShapeDtypeStruct(aff.shape, jnp.int32),
                   jax.ShapeDtypeStruct((n_tiles, N_EXPERTS, LANES), jnp.int32)],
        compiler_params=pltpu.CompilerParams(dimension_semantics=("arbitrary",), vmem_limit_bytes=VMEM_LIMIT),
        name="route",
    )(aff)


def _tile_max(lo_ref, i, start_of):
    m = lo_ref[(i + 1) * N_EXPERTS] - start_of(0)
    for e in range(1, N_EXPERTS):
        m = jnp.maximum(m, lo_ref[(i + 1) * N_EXPERTS + e] - start_of(e))
    return m


def _dispatch_kernel(lo_ref, hn_ref, slot_ref, xe_ref, stage, stage_extra, carry, sem, sem_extra, *, zero_from,
                     zero_rows):
    i = pl.program_id(0)
    s = i % 2
    base = lambda e: (lo_ref[i * N_EXPERTS + e] // XE_ALIGN) * XE_ALIGN
    next_base = lambda e: (lo_ref[(i + 1) * N_EXPERTS + e] // XE_ALIGN) * XE_ALIGN

    def compact(k, dst):
        w = lax.broadcasted_iota(jnp.int32, (SLOT_W, ROW_TILE), 0) + k * SLOT_W
        onehot = jnp.concatenate(
            [jnp.where(slot_ref[e:e + 1, :] - base(e) == w, 1.0, 0.0).astype(_BF) for e in range(N_EXPERTS)], axis=0)
        dst[...] = _dot(onehot, hn_ref[...]).astype(_BF)

    def copy(e, src, k, sm):
        dst_row = pl.multiple_of(base(e) + k * SLOT_W, XE_ALIGN)
        return pltpu.make_async_copy(src.at[pl.ds(e * SLOT_W, SLOT_W)], xe_ref.at[e, pl.ds(dst_row, SLOT_W)], sm)

    def save_carry(k, src):
        for e in range(N_EXPERTS):
            g = next_base(e) - base(e) - k * SLOT_W

            @pl.when((g >= 0) & (g < SLOT_W))
            def _():
                carry[e] = src[pl.ds(pl.multiple_of(e * SLOT_W + g, XE_ALIGN), XE_ALIGN), :]

    @pl.when(i == 0)
    def _():
        carry[...] = jnp.zeros(carry.shape, _BF)
        stage_extra[...] = jnp.zeros(stage_extra.shape, _BF)
        zero_copies = [pltpu.make_async_copy(stage_extra.at[pl.ds(0, zero_rows)],
                                             xe_ref.at[e, pl.ds(zero_from, zero_rows)], sem_extra)
                       for e in range(N_EXPERTS)]
        for cp in zero_copies:
            cp.start()
        for cp in zero_copies:
            cp.wait()

    compact(0, stage.at[s])
    for e in range(N_EXPERTS):
        stage[s, e * SLOT_W:e * SLOT_W + XE_ALIGN, :] += carry[e]
    save_carry(0, stage.at[s])

    @pl.when(i > 0)
    def _():
        for e in range(N_EXPERTS):
            copy(e, stage.at[1 - s], 0, sem.at[1 - s]).wait()

    for e in range(N_EXPERTS):
        copy(e, stage.at[s], 0, sem.at[s]).start()

    def extra_pass(k, c):
        compact(k, stage_extra)
        save_carry(k, stage_extra)
        for wait in (False, True):
            for e in range(N_EXPERTS):
                @pl.when(lo_ref[(i + 1) * N_EXPERTS + e] - base(e) >= k * SLOT_W)
                def _():
                    cp = copy(e, stage_extra, k, sem_extra)
                    cp.wait() if wait else cp.start()
        return c
    lax.fori_loop(1, _tile_max(lo_ref, i, base) // SLOT_W + 1, extra_pass, 0)

    @pl.when(i == pl.num_programs(0) - 1)
    def _():
        for e in range(N_EXPERTS):
            copy(e, stage.at[s], 0, sem.at[s]).wait()


def _dispatch(lo, hn, slot, cap, cap_pad):
    n_tiles = slot.shape[0]
    zero_from = cap // XE_ALIGN * XE_ALIGN
    zero_rows = cap_pad + SLOT_W - zero_from
    assert zero_rows <= N_EXPERTS * SLOT_W and SLOT_W % XE_ALIGN == 0 and cap_pad % XE_ALIGN == 0
    return pl.pallas_call(
        functools.partial(_dispatch_kernel, zero_from=zero_from, zero_rows=zero_rows),
        grid_spec=pltpu.PrefetchScalarGridSpec(
            num_scalar_prefetch=1,
            grid=(n_tiles,),
            in_specs=[pl.BlockSpec((ROW_TILE, D_MODEL), lambda i, lo: (i, 0)),
                      pl.BlockSpec((None, N_EXPERTS, ROW_TILE), lambda i, lo: (i, 0, 0))],
            out_specs=pl.BlockSpec(memory_space=pl.ANY),
            scratch_shapes=[pltpu.VMEM((2, N_EXPERTS * SLOT_W, D_MODEL), _BF),
                            pltpu.VMEM((N_EXPERTS * SLOT_W, D_MODEL), _BF),
                            pltpu.VMEM((N_EXPERTS, XE_ALIGN, D_MODEL), _BF),
                            pltpu.SemaphoreType.DMA((2,)), pltpu.SemaphoreType.DMA(())]),
        out_shape=jax.ShapeDtypeStruct((N_EXPERTS, cap_pad + SLOT_W, D_MODEL), _BF),
        compiler_params=pltpu.CompilerParams(dimension_semantics=("arbitrary",), vmem_limit_bytes=VMEM_LIMIT),
        name="dispatch",
    )(lo, hn, slot)


def _expert_kernel(x_ref, wg_ref, wu_ref, wd_ref, y_ref):
    x = x_ref[...]
    g = _dot(x, wg_ref[...])
    u = _dot(x, wu_ref[...].astype(_BF))
    hid = (g * jax.nn.sigmoid(g) * u).astype(_BF)
    y_ref[...] = _dot(hid, wd_ref[...].astype(_BF)).astype(_BF)


def _experts(xe, wg, wu, wd, cap_pad, tile):
    n_e = xe.shape[0]
    wspec = lambda a: pl.BlockSpec((None,) + a.shape[1:], lambda e, t: (e, 0, 0))
    return pl.pallas_call(
        _expert_kernel,
        grid=(n_e, cap_pad // tile),
        in_specs=[pl.BlockSpec((None, tile, D_MODEL), lambda e, t: (e, t, 0)), wspec(wg), wspec(wu), wspec(wd)],
        out_specs=pl.BlockSpec((None, tile, D_MODEL), lambda e, t: (e, t, 0)),
        out_shape=jax.ShapeDtypeStruct((n_e, cap_pad, D_MODEL), _BF),
        compiler_params=pltpu.CompilerParams(dimension_semantics=("parallel", "arbitrary"),
                                             vmem_limit_bytes=VMEM_LIMIT),
        name="experts",
    )(xe, wg, wu, wd)


def _combine_kernel(lo_ref, h_ref, slot_ref, aff_ref, g_ref, ye_ref, o_ref, win, win_extra, f_acc, sem, sem_extra, *,
                    cap_pad):
    i = pl.program_id(0)
    s = i % 2
    first = lambda t, e: (lo_ref[t * N_EXPERTS + e] // 16) * 16

    def start_row(t, e, k):
        return pl.multiple_of(jnp.minimum(first(t, e) + k * WIN, cap_pad - WIN), 16)

    def window_copies(t, k, buf, sm):
        return [pltpu.make_async_copy(ye_ref.at[e, pl.ds(start_row(t, e, k), WIN)], buf.at[pl.ds(e * WIN, WIN)], sm)
                for e in range(N_EXPERTS)]

    def weights(k):
        w = lax.broadcasted_iota(jnp.int32, (WIN, ROW_TILE), 0)
        parts = []
        for e in range(N_EXPERTS):
            slot = slot_ref[e:e + 1, :]
            hit = (slot == w + start_row(i, e, k)) & (slot >= first(i, e) + k * WIN)
            parts.append(jnp.where(hit, aff_ref[e:e + 1, :], 0.0).astype(_BF))
        return jnp.concatenate(parts, axis=0)

    scatter = lambda wts, rows: lax.dot_general(wts, rows, (((0,), (0,)), ((), ())), preferred_element_type=_F32)

    @pl.when(i == 0)
    def _():
        for cp in window_copies(0, 0, win.at[0], sem.at[0]):
            cp.start()

    @pl.when(i + 1 < pl.num_programs(0))
    def _():
        for cp in window_copies(i + 1, 0, win.at[1 - s], sem.at[1 - s]):
            cp.start()

    wts = weights(0)
    for cp in window_copies(i, 0, win.at[s], sem.at[s]):
        cp.wait()
    f_acc[...] = scatter(wts, win[s])

    def extra_pass(k, c):
        copies = window_copies(i, k, win_extra, sem_extra)
        for cp in copies:
            cp.start()
        wts_k = weights(k)
        for cp in copies:
            cp.wait()
        f_acc[...] += scatter(wts_k, win_extra[...])
        return c
    reach = _tile_max(lo_ref, i, lambda e: first(i, e))
    lax.fori_loop(1, (reach + WIN - 1) // WIN, extra_pass, 0)
    o_ref[...] = _rms(h_ref[...] + f_acc[...], g_ref[...])


def _combine(lo, h, slot, aff, g_final, ye, n_rows):
    cap_pad = ye.shape[1]
    assert cap_pad % 16 == 0 and WIN % 16 == 0 and cap_pad >= WIN
    tile3 = pl.BlockSpec((None, N_EXPERTS, ROW_TILE), lambda i, lo: (i, 0, 0))
    return pl.pallas_call(
        functools.partial(_combine_kernel, cap_pad=cap_pad),
        grid_spec=pltpu.PrefetchScalarGridSpec(
            num_scalar_prefetch=1,
            grid=(n_rows // ROW_TILE,),
            in_specs=[pl.BlockSpec((ROW_TILE, D_MODEL), lambda i, lo: (i, 0)), tile3, tile3,
                      pl.BlockSpec(g_final.shape, lambda i, lo: (0, 0)), pl.BlockSpec(memory_space=pl.ANY)],
            out_specs=pl.BlockSpec((ROW_TILE, D_MODEL), lambda i, lo: (i, 0)),
            scratch_shapes=[pltpu.VMEM((2, N_EXPERTS * WIN, D_MODEL), _BF), pltpu.VMEM((N_EXPERTS * WIN, D_MODEL), _BF),
                            pltpu.VMEM((ROW_TILE, D_MODEL), _F32),
                            pltpu.SemaphoreType.DMA((2,)), pltpu.SemaphoreType.DMA(())]),
        out_shape=jax.ShapeDtypeStruct((n_rows, D_MODEL), _F32),
        compiler_params=pltpu.CompilerParams(dimension_semantics=("arbitrary",), vmem_limit_bytes=VMEM_LIMIT),
        name="combine_norm",
    )(lo, h, slot, aff, g_final, ye)


def _rotate_half_cols(w):
    half = QK_ROPE // 2
    return jnp.concatenate([-w[..., half:], w[..., :half]], axis=-1)


def _prepare_weights(w_in, g_attn, g_q, w_uq, g_kv, w_ukv, w_o, g_ffn, w_router, w_gate, w_up, w_down, na_rpb):
    na_scale = 1.0 / math.sqrt(NA_HEAD_DIM)
    c0, c2 = 3 * NA_WIDTH, 3 * NA_WIDTH + Q_LORA + KV_LORA
    w_kr = w_in[:, c2:]
    zpad = lambda n: jnp.zeros((D_MODEL, n), _F32)
    slot = lambda w: jnp.concatenate([zpad(QK_NOPE), w, zpad(LANES - QK_NOPE - QK_ROPE)], axis=1)
    w1 = jnp.concatenate([w_in[:, NA_WIDTH:2 * NA_WIDTH], w_in[:, c0:c2], slot(w_kr),
                          slot(_rotate_half_cols(w_kr))], axis=1).astype(_BF)
    q_scale = LOG2E / math.sqrt(QK_NOPE + QK_ROPE)
    wq = (w_uq * q_scale).reshape(Q_LORA, MLA_HEADS, QK_NOPE + QK_ROPE)
    pad = LANES - QK_NOPE - QK_ROPE
    plain = jnp.pad(wq, ((0, 0), (0, 0), (0, pad)))
    rot = jnp.pad(_rotate_half_cols(wq[..., QK_NOPE:]), ((0, 0), (0, 0), (QK_NOPE, pad)))
    w2t = jnp.concatenate([plain.reshape(Q_LORA, -1), rot.reshape(Q_LORA, -1)], axis=1).T.astype(_BF)
    wkv = w_ukv.reshape(KV_LORA, MLA_HEADS, QK_NOPE + V_HEAD)
    w3k = jnp.pad(wkv[..., :QK_NOPE], ((0, 0), (0, 0), (0, LANES - QK_NOPE))).reshape(KV_LORA, -1).astype(_BF)
    w3vt = wkv[..., QK_NOPE:].reshape(KV_LORA, -1).T.astype(_BF)
    wr = w_router.T
    wr_hi = wr.astype(_BF)
    wr_lo = (wr - wr_hi.astype(_F32)).astype(_BF)
    return dict(
        w1=w1, w2t=w2t, w3k=w3k, w3vt=w3vt,
        g_attn=g_attn[None], g_q=g_q[None], g_kv=g_kv[None], g_ffn=g_ffn[None],
        wo_na=w_o[:NA_WIDTH].astype(_BF), wo_mla=w_o[NA_WIDTH:].astype(_BF),
        wr_hi=wr_hi, wr_lo=wr_lo,
        wg=w_gate.astype(_BF), wu=w_up, wd=w_down,
        wqvt=jnp.concatenate([w_in[:, :NA_WIDTH] * (na_scale * LOG2E), w_in[:, 2 * NA_WIDTH:3 * NA_WIDTH]],
                             axis=1).T.astype(_BF),
        na_bias_t=_na_bias_table(na_rpb),
    )


def _na_bias_table(rpb):
    ql = np.arange(NA_BLK)
    kl = np.arange(3 * NA_BLK)
    ri, qc = ql // GRID_W, ql % GRID_W
    kj, kc = kl // GRID_W, kl % GRID_W
    qcs = np.clip(qc - WIN_C // 2, 0, GRID_W - WIN_C)
    col_ok = (kc[None, :] >= qcs[:, None]) & (kc[None, :] < qcs[:, None] + WIN_C)
    lo = np.stack([np.full(NA_BLK, NA_ROWS), ri, np.zeros(NA_BLK, np.int64)])
    row_ok = (kj[None, None, :] >= lo[:, :, None]) & (kj[None, None, :] < lo[:, :, None] + WIN_R)
    ok = row_ok & col_ok[None]
    rows_q, rows_k, cols = np.arange(NA_ROWS), np.arange(3 * NA_ROWS), np.arange(GRID_W)
    dr = np.clip(rows_k[None, :] - rows_q[:, None] + (WIN_R - 1) - NA_ROWS, 0, 2 * WIN_R - 2)
    dc = np.clip(cols[None, :] - cols[:, None] + (WIN_C - 1), 0, 2 * WIN_C - 2)
    pick_r = (dr[..., None] == np.arange(2 * WIN_R - 1)).astype(np.float32)
    pick_c = (dc[..., None] == np.arange(2 * WIN_C - 1)).astype(np.float32)
    vals = jnp.einsum("hrc,ijr,qkc->hiqjk", rpb.astype(_F32), pick_r, pick_c, precision=lax.Precision.HIGHEST)
    vals = vals.reshape(NA_HEADS, NA_BLK, 3 * NA_BLK)
    return jnp.where(jnp.asarray(ok)[:, None], vals[None] * LOG2E, NEG).transpose(0, 1, 3, 2)


def _rope_tables(seq):
    half = QK_ROPE // 2
    inv_freq = ROPE_THETA ** (-jnp.arange(half, dtype=_F32) * 2.0 / QK_ROPE)
    pos = jnp.concatenate([jnp.arange(seq, dtype=_F32) + N_META,
                           jnp.tile(jnp.arange(N_META, dtype=_F32), ROW_TILE // N_META)])
    n = seq + ROW_TILE
    pad = LANES - QK_NOPE - QK_ROPE

    def lay(a, nope_fill, axis):
        shape = lambda k: (n, k) if axis == 1 else (k, n)
        return jnp.concatenate([jnp.full(shape(QK_NOPE), nope_fill, _F32), a, a, jnp.zeros(shape(pad), _F32)], axis=axis)
    ang = pos[:, None] * inv_freq[None, :]
    dense = ang.reshape(-1, LANES)
    cos, sin = jnp.cos(dense).reshape(n, half), jnp.sin(dense).reshape(n, half)
    return lay(cos, 1.0, 1), lay(sin, 0.0, 1), lay(cos.T, 1.0, 0), lay(sin.T, 0.0, 0)


def _expert_tile(cap):
    if cap <= 1024:
        return -(-cap // 16) * 16
    return min(range(512, 1025, 16), key=lambda t: (-(-cap // t) * t - cap, -t))


def _run_group(x, meta_tokens, prm, rope, g_final, tq, tk):
    batch, seq, _ = x.shape
    n_grid = batch * seq
    n_valid = n_grid + batch * N_META
    assert seq % NA_BLK == 0 and seq // GRID_W >= WIN_R and batch * N_META <= ROW_TILE
    meta = jnp.broadcast_to(meta_tokens[None], (batch, N_META, D_MODEL)).reshape(batch * N_META, D_MODEL)
    xg = x.reshape(n_grid, D_MODEL)
    xm = jnp.pad(meta, ((0, ROW_TILE - batch * N_META), (0, 0)))
    assert rope[0].shape[0] >= seq + ROW_TILE
    kna, qt, k3, vt, qnat, vnat = _proj(xg, xm, seq, *rope, prm["g_attn"], prm["g_q"], prm["g_kv"],
                                        prm["w1"], prm["w2t"], prm["w3k"], prm["w3vt"], prm["wqvt"])
    n_meta = batch * N_META
    meta_cols = slice(n_grid, n_grid + n_meta)
    vmt_na = vnat[:, :, meta_cols].reshape(NA_HEADS, V_EXT, batch, N_META).transpose(2, 0, 1, 3)
    o_na_g = _na_attention(qnat, kna, vnat, vmt_na, prm["na_bias_t"], batch, seq).reshape(NA_WIDTH, n_grid)
    q_own = jnp.stack([qnat[h, (h % 2) * NA_HEAD_DIM:(h % 2 + 1) * NA_HEAD_DIM, meta_cols] for h in range(NA_HEADS)])
    rows_of = lambda t: t.transpose(2, 0, 1).reshape(n_meta, NA_WIDTH)
    o_na_m = _na_meta_attention(rows_of(q_own), kna[meta_cols], rows_of(vnat[:, :NA_HEAD_DIM, meta_cols]), batch)
    km = k3[:, n_grid:n_grid + n_meta].reshape(MLA_HEADS, batch, N_META, LANES).transpose(1, 0, 2, 3)
    vmt = vt[:, :, n_grid:n_grid + n_meta].reshape(MLA_HEADS, V_EXT, batch, N_META).transpose(2, 0, 1, 3)
    tk = min(tk, seq)
    ot_g = _mla_attention(qt, k3, vt, km, vmt, batch, seq, seq, 0, seq // tq, tq, tk)
    ot_m = _mla_attention(qt, k3, vt, km, vmt, batch, seq, ROW_TILE, n_grid // ROW_TILE, 0, ROW_TILE, tk)
    ot_m = ot_m.reshape(MLA_WIDTH, batch, ROW_TILE)
    ot_m = jnp.concatenate([ot_m[:, b, b * N_META:(b + 1) * N_META] for b in range(batch)], axis=1)
    ot_m = jnp.pad(ot_m, ((0, 0), (0, ROW_TILE - n_meta)))
    o_na_m = jnp.pad(o_na_m.T, ((0, 0), (0, ROW_TILE - n_meta)))
    h, hn, aff = _outproj(xg, xm, o_na_g, o_na_m, ot_g.reshape(MLA_WIDTH, n_grid), ot_m, prm["wo_na"], prm["wo_mla"],
                          prm["g_ffn"], prm["wr_hi"], prm["wr_lo"], n_valid)
    cap = EC_CAPACITY * n_valid // N_EXPERTS
    tile = _expert_tile(cap)
    cap_pad = -(-cap // tile) * tile
    slot, lo3 = _route(aff, cap)
    lo = jnp.concatenate([lo3[:, :, 0], jnp.full((1, N_EXPERTS), cap, jnp.int32)]).reshape(-1)
    xe = _dispatch(lo, hn, slot, cap, cap_pad)
    ye = _experts(xe, prm["wg"], prm["wu"], prm["wd"], cap_pad, tile)
    return _combine(lo, h, slot, aff, g_final[None], ye, n_grid).reshape(batch, seq, D_MODEL)


def kernel(x_prompt, x_sample, meta_tokens, g_attn, w_in, na_rpb, g_q, w_uq, g_kv, w_ukv, w_o, g_ffn, w_router,
           w_gate, w_up, w_down, g_final):
    prm = _prepare_weights(w_in[0], g_attn[0], g_q[0], w_uq[0], g_kv[0], w_ukv[0], w_o[0], g_ffn[0], w_router[0],
                           w_gate[0], w_up[0], w_down[0], na_rpb[0])
    rope = _rope_tables(max(x_prompt.shape[1], x_sample.shape[1]))
    y_prompt = _run_group(x_prompt, meta_tokens, prm, rope, g_final, MLA_TQ, MLA_TK)
    y_sample = _run_group(x_sample, meta_tokens, prm, rope, g_final, MLA_TQ, MLA_TK)
    return (y_prompt, y_sample)
```
